```python
import math
import jax
import jax.numpy as jnp
from jax import lax
import numpy as np


D_MODEL = 1024
BATCH = 16
SEQ = 2048
DEPTH = 1

GRID_W = 64
CTX_LEN = 256
DA_HEAD_DIM = 64
DA_V_DIM = 2 * DA_HEAD_DIM
DA_WIDTH = D_MODEL // 2
DA_HEADS = DA_WIDTH // DA_V_DIM
QK_COLS = DA_HEADS * DA_HEAD_DIM
SG_CHUNK = 128
SG_GROUP_DIM = 128
SG_WIDTH = D_MODEL - DA_WIDTH
SG_GROUPS = SG_WIDTH // SG_GROUP_DIM
MIX_WIDTH = DA_WIDTH + SG_WIDTH
KV_LO = 2 * QK_COLS
KV_HI = 4 * QK_COLS + DA_WIDTH
IN_COLS = KV_HI + 2 * SG_WIDTH
Q_BLOCK = 128
ROPE_THETA = 10000.0
MOE_GROUPS = 4
MOE_EXPERTS_PER_GROUP = 8
N_EXPERTS = MOE_GROUPS * MOE_EXPERTS_PER_GROUP
MOE_TOP_K = 2
D_EXPERT = D_MODEL
MOE_BLOCK = 128
EPS = 1e-5
DEEPNORM_ALPHA = (2.0 * DEPTH) ** 0.25
DEEPNORM_BETA = (8.0 * DEPTH) ** -0.25

kernel_name = 'hybrid_diffattn_sgmlp_hmoe_dit_block'


def _layer_norm(x, g, b):
    xf = x.astype(jnp.float32)
    mu = jnp.mean(xf, -1, keepdims=True)
    var = jnp.mean(jnp.square(xf - mu), -1, keepdims=True)
    y = (xf - mu) * lax.rsqrt(var + EPS) * g.astype(jnp.float32) + b.astype(jnp.float32)
    return y.astype(x.dtype)


def _modulate(h, shift, scale):
    return h * (1 + scale[..., None, :]) + shift[..., None, :]


def _heads(t, dh):
    return t.reshape(t.shape[:-1] + (DA_HEADS, dh))


def _rope_1d(x, pos):
    half = x.shape[-1] // 2
    inv = ROPE_THETA ** (-jnp.arange(half, dtype=jnp.float32) / half)
    ang = pos[:, None] * inv[None, :]
    cos = jnp.cos(ang)[None, :, None, :].astype(x.dtype)
    sin = jnp.sin(ang)[None, :, None, :].astype(x.dtype)
    x1, x2 = x[..., :half], x[..., half:]
    return jnp.concatenate([x1 * cos - x2 * sin, x2 * cos + x1 * sin], -1)


def _axial_rope(x, rows, cols):
    a = x.shape[-1] // 2
    return jnp.concatenate([_rope_1d(x[..., :a], rows), _rope_1d(x[..., a:], cols)], -1)


def _diff_attend(q1, q2, k1, k2, v, lam):
    s1 = jnp.einsum('bqhd,bkhd->bhqk', q1, k1).astype(jnp.float32)
    s2 = jnp.einsum('bqhd,bkhd->bhqk', q2, k2).astype(jnp.float32)
    a = jax.nn.softmax(s1, axis=-1) - lam * jax.nn.softmax(s2, axis=-1)
    return jnp.einsum('bhqk,bkhe->bqhe', a.astype(v.dtype), v)


def _diff_attend_blocked(q1, q2, k1, k2, v, lam):
    B, L, H, d = q1.shape
    nb = L // Q_BLOCK

    def to_blocks(q):
        return q.reshape(B, nb, Q_BLOCK, H, d).swapaxes(0, 1)

    o = lax.map(lambda qs: _diff_attend(qs[0], qs[1], k1, k2, v, lam),
                (to_blocks(q1), to_blocks(q2)))
    return o.swapaxes(0, 1).reshape(B, L, H, v.shape[-1])


def _diff_head_out(o, g, lam_init):
    of = o.astype(jnp.float32)
    of = of * lax.rsqrt(jnp.mean(of * of, -1, keepdims=True) + EPS) * g.astype(jnp.float32)
    of = of * (1.0 - lam_init)
    return of.astype(o.dtype).reshape(o.shape[:-2] + (DA_WIDTH,))


def _spatial_gate(z, ln_g, ln_b, w_s, b_s):
    z = jax.nn.gelu(z, approximate=False)
    u, v = z[..., :SG_WIDTH], z[..., SG_WIDTH:]
    v = _layer_norm(v, ln_g, ln_b)
    B, L, _ = v.shape
    v = v.reshape(B, L // SG_CHUNK, SG_CHUNK, SG_GROUPS, SG_GROUP_DIM)
    s = jnp.einsum('gpq,bnqgc->bnpgc', w_s, v) + b_s.T[None, None, :, :, None]
    return u * s.reshape(B, L, SG_WIDTH)


def _expert_dispatch(t, eid, w, w_gate, w_up, w_down):
    N, D = t.shape
    S = N * MOE_TOP_K
    flat_e = eid.reshape(S)
    flat_w = w.reshape(S)
    order = jnp.argsort(flat_e)
    sorted_e = flat_e[order]
    counts = jnp.zeros((N_EXPERTS,), jnp.int32).at[flat_e].add(1)
    padded = (counts + MOE_BLOCK - 1) // MOE_BLOCK * MOE_BLOCK
    start = jnp.cumsum(counts) - counts
    pad_end = jnp.cumsum(padded)
    pad_start = pad_end - padded
    dest = pad_start[sorted_e] + (jnp.arange(S, dtype=jnp.int32) - start[sorted_e])
    n_blocks = -(-S // MOE_BLOCK) + N_EXPERTS
    P = n_blocks * MOE_BLOCK
    buf_tok = jnp.zeros((P,), jnp.int32).at[dest].set((order // MOE_TOP_K).astype(jnp.int32))
    buf_w = jnp.zeros((P,), t.dtype).at[dest].set(flat_w[order])
    blk_start = jnp.arange(n_blocks, dtype=jnp.int32) * MOE_BLOCK
    blk_e = jnp.minimum(jnp.searchsorted(pad_end, blk_start, side='right'), N_EXPERTS - 1)
    xb = t[buf_tok].reshape(n_blocks, MOE_BLOCK, D)

    def run(args):
        xe, e = args
        hid = jax.nn.silu(xe @ w_gate[e]) * (xe @ w_up[e])
        return hid @ w_down[e]

    yb = lax.map(run, (xb, blk_e)).reshape(P, D)
    return jnp.zeros_like(t).at[buf_tok].add(yb * buf_w[:, None])


def _hier_moe(h, wg, bg, we, be, w_gate, w_up, w_down):
    B, L, D = h.shape
    t = h.reshape(B * L, D)
    N = t.shape[0]
    g_logits = (t @ wg).astype(jnp.float32) + bg.astype(jnp.float32)
    g_prob = jax.nn.softmax(g_logits, axis=-1)
    g_sel = jnp.argmax(g_logits, axis=-1).astype(jnp.int32)
    g_w = jnp.take_along_axis(g_prob, g_sel[:, None], axis=-1)
    e_logits = ((t @ we).astype(jnp.float32) + be.astype(jnp.float32)).reshape(
        N, MOE_GROUPS, MOE_EXPERTS_PER_GROUP)
    e_logits = jnp.take_along_axis(e_logits, g_sel[:, None, None], axis=1)[:, 0]
    top_v, top_i = lax.top_k(e_logits, MOE_TOP_K)
    w = jax.nn.softmax(top_v, axis=-1) * g_w
    eid = g_sel[:, None] * MOE_EXPERTS_PER_GROUP + top_i.astype(jnp.int32)
    out = _expert_dispatch(t, eid, w.astype(t.dtype), w_gate, w_up, w_down)
    return out.reshape(B, L, D)


def setup_inputs(seed: int = 0) -> dict:
    key = jax.random.key(seed)
    ks = jax.random.split(key, 28)

    def nrm(k, shape, scale):
        return jax.random.normal(k, shape, jnp.float32) * scale

    return {
        'x': nrm(ks[0], (BATCH, SEQ, D_MODEL), 1.0),
        'c': nrm(ks[1], (BATCH, D_MODEL), 1.0),
        'ctx': nrm(ks[2], (BATCH, CTX_LEN, D_MODEL), 1.0),
        'c_ctx': nrm(ks[3], (D_MODEL,), 1.0),
        'w_mod': nrm(ks[4], (DEPTH, D_MODEL, 6 * D_MODEL), 0.5 * D_MODEL ** -0.5),
        'b_mod': nrm(ks[5], (DEPTH, 6 * D_MODEL), 0.01),
        'w_in': nrm(ks[6], (DEPTH, D_MODEL, IN_COLS), D_MODEL ** -0.5),
        'lam_q1': nrm(ks[7], (DEPTH, DA_HEAD_DIM), 0.1),
        'lam_k1': nrm(ks[8], (DEPTH, DA_HEAD_DIM), 0.1),
        'lam_q2': nrm(ks[9], (DEPTH, DA_HEAD_DIM), 0.1),
        'lam_k2': nrm(ks[10], (DEPTH, DA_HEAD_DIM), 0.1),
        'subln_g': 1.0 + nrm(ks[11], (DEPTH, DA_V_DIM), 0.01),
        'sg_ln_g': 1.0 + nrm(ks[12], (DEPTH, SG_WIDTH), 0.01),
        'sg_ln_b': nrm(ks[13], (DEPTH, SG_WIDTH), 0.01),
        'sg_w': nrm(ks[14], (DEPTH, SG_GROUPS, SG_CHUNK, SG_CHUNK), SG_CHUNK ** -0.5),
        'sg_b': 1.0 + nrm(ks[15], (DEPTH, SG_GROUPS, SG_CHUNK), 0.01),
        'w_out': nrm(ks[16], (DEPTH, MIX_WIDTH, D_MODEL), DEEPNORM_BETA * MIX_WIDTH ** -0.5),
        'ln1_g': 1.0 + nrm(ks[17], (DEPTH, D_MODEL), 0.01),
        'ln1_b': nrm(ks[18], (DEPTH, D_MODEL), 0.01),
        'router_group_w': nrm(ks[19], (DEPTH, D_MODEL, MOE_GROUPS), D_MODEL ** -0.5),
        'router_group_b': nrm(ks[20], (DEPTH, MOE_GROUPS), 0.01),
        'router_expert_w': nrm(ks[21], (DEPTH, D_MODEL, N_EXPERTS), D_MODEL ** -0.5),
        'router_expert_b': nrm(ks[22], (DEPTH, N_EXPERTS), 0.01),
        'exp_w_gate': nrm(ks[23], (DEPTH, N_EXPERTS, D_MODEL, D_EXPERT), D_MODEL ** -0.5),
        'exp_w_up': nrm(ks[24], (DEPTH, N_EXPERTS, D_MODEL, D_EXPERT), D_MODEL ** -0.5),
        'exp_w_down': nrm(ks[25], (DEPTH, N_EXPERTS, D_EXPERT, D_MODEL), DEEPNORM_BETA * D_EXPERT ** -0.5),
        'ln2_g': 1.0 + nrm(ks[26], (DEPTH, D_MODEL), 0.01),
        'ln2_b': nrm(ks[27], (DEPTH, D_MODEL), 0.01),
    }


def reference(x, c, ctx, c_ctx, w_mod, b_mod, w_in, lam_q1, lam_k1, lam_q2, lam_k2,
              subln_g, sg_ln_g, sg_ln_b, sg_w, sg_b, w_out, ln1_g, ln1_b,
              router_group_w, router_group_b, router_expert_w, router_expert_b,
              exp_w_gate, exp_w_up, exp_w_down, ln2_g, ln2_b):
    B, L, D = x.shape
    rows_n = L // GRID_W
    rows = jnp.repeat(jnp.arange(rows_n, dtype=jnp.float32), GRID_W)
    cols = jnp.tile(jnp.arange(GRID_W, dtype=jnp.float32), rows_n)
    q_scale = DA_HEAD_DIM ** -0.5
    for l in range(DEPTH):
        last = l == DEPTH - 1
        wi = w_in[l]
        mod = jnp.einsum('bd,de->be', jax.nn.silu(c), w_mod[l]) + b_mod[l]
        mod_c = jax.nn.silu(c_ctx) @ w_mod[l] + b_mod[l]
        sh1, sc1, g1, sh2, sc2, g2 = jnp.split(mod, 6, axis=-1)
        csh1, csc1, cg1, csh2, csc2, cg2 = jnp.split(mod_c, 6, axis=-1)
        lam_init = 0.8 - 0.6 * math.exp(-0.3 * l)
        lam = (jnp.exp(jnp.sum(lam_q1[l].astype(jnp.float32) * lam_k1[l].astype(jnp.float32)))
               - jnp.exp(jnp.sum(lam_q2[l].astype(jnp.float32) * lam_k2[l].astype(jnp.float32)))
               + lam_init)

        h_c = _modulate(ctx, csh1, csc1)
        kv_c = jnp.einsum('bld,dc->blc', h_c, wi[:, KV_LO:KV_HI])
        k1_c = _heads(kv_c[..., :QK_COLS], DA_HEAD_DIM)
        k2_c = _heads(kv_c[..., QK_COLS:2 * QK_COLS], DA_HEAD_DIM)
        v_c = _heads(kv_c[..., 2 * QK_COLS:], DA_V_DIM)

        h = _modulate(x, sh1, sc1)
        p = jnp.einsum('bld,dc->blc', h, wi)
        q1 = _axial_rope(_heads(p[..., :QK_COLS], DA_HEAD_DIM) * q_scale, rows, cols)
        q2 = _axial_rope(_heads(p[..., QK_COLS:KV_LO], DA_HEAD_DIM) * q_scale, rows, cols)
        k1 = _axial_rope(_heads(p[..., KV_LO:KV_LO + QK_COLS], DA_HEAD_DIM), rows, cols)
        k2 = _axial_rope(_heads(p[..., KV_LO + QK_COLS:2 * KV_LO], DA_HEAD_DIM), rows, cols)
        v = _heads(p[..., 2 * KV_LO:KV_HI], DA_V_DIM)
        k1_all = jnp.concatenate([k1, k1_c], axis=1)
        k2_all = jnp.concatenate([k2, k2_c], axis=1)
        v_all = jnp.concatenate([v, v_c], axis=1)
        da = _diff_head_out(_diff_attend_blocked(q1, q2, k1_all, k2_all, v_all, lam),
                            subln_g[l], lam_init)
        sg = _spatial_gate(p[..., KV_HI:], sg_ln_g[l], sg_ln_b[l], sg_w[l], sg_b[l])
        y = jnp.einsum('blc,cd->bld', jnp.concatenate([da, sg], axis=-1), w_out[l])
        x = _layer_norm(DEEPNORM_ALPHA * x + g1[:, None, :] * y, ln1_g[l], ln1_b[l])

        f = _hier_moe(_modulate(x, sh2, sc2), router_group_w[l], router_group_b[l],
                      router_expert_w[l], router_expert_b[l],
                      exp_w_gate[l], exp_w_up[l], exp_w_down[l])
        x = _layer_norm(DEEPNORM_ALPHA * x + g2[:, None, :] * f, ln2_g[l], ln2_b[l])

        if not last:
            qc = jnp.einsum('bld,dc->blc', h_c, wi[:, :KV_LO])
            q1_c = _heads(qc[..., :QK_COLS], DA_HEAD_DIM) * q_scale
            q2_c = _heads(qc[..., QK_COLS:], DA_HEAD_DIM) * q_scale
            da_c = _diff_head_out(_diff_attend(q1_c, q2_c, k1_c, k2_c, v_c, lam), subln_g[l], lam_init)
            sg_c = _spatial_gate(jnp.einsum('bld,dc->blc', h_c, wi[:, KV_HI:]),
                                 sg_ln_g[l], sg_ln_b[l], sg_w[l], sg_b[l])
            y_c = jnp.einsum('blc,cd->bld', jnp.concatenate([da_c, sg_c], axis=-1), w_out[l])
            ctx = _layer_norm(DEEPNORM_ALPHA * ctx + cg1 * y_c, ln1_g[l], ln1_b[l])
            f_c = _hier_moe(_modulate(ctx, csh2, csc2), router_group_w[l], router_group_b[l],
                            router_expert_w[l], router_expert_b[l],
                            exp_w_gate[l], exp_w_up[l], exp_w_down[l])
            ctx = _layer_norm(DEEPNORM_ALPHA * ctx + cg2 * f_c, ln2_g[l], ln2_b[l])
    return x
```

```python
import math

import jax
import jax.numpy as jnp
from jax import lax
from jax.experimental import pallas as pl
from jax.experimental.pallas import tpu as pltpu

F32 = jnp.float32
BF16 = jnp.bfloat16

D_MODEL = 1024
GRID_W = 64
DA_HEAD_DIM = 64
DA_V_DIM = 128
DA_WIDTH = 512
DA_HEADS = 4
QK_COLS = 256
SG_CHUNK = 128
SG_WIDTH = 512
SG_GROUPS = 4
KV_LO = 512
KV_HI = 1536
IN_COLS = 2560
ROPE_THETA = 10000.0
MOE_GROUPS = 4
MOE_EXPERTS_PER_GROUP = 8
N_EXPERTS = 32
MOE_BLOCK = 128
EPS = 1e-5
DEPTH = 1
DEEPNORM_ALPHA = (2.0 * DEPTH) ** 0.25
LAM_INIT = 0.8 - 0.6 * math.exp(-0.3 * 0)
Q_SCALE = DA_HEAD_DIM ** -0.5
SQRT_HALF = math.sqrt(0.5)

LANES = 128
TOK_TILE = 512
ATTN_TQ = 512
VMEM_LIMIT = 56 * 1024 * 1024


def _cparams(sem):
    return pltpu.CompilerParams(dimension_semantics=sem, vmem_limit_bytes=VMEM_LIMIT)


def _dot(a, b):
    return jnp.dot(a, b, preferred_element_type=F32)


def _dot_hi(a, b):
    return jnp.dot(a, b, preferred_element_type=F32, precision=lax.Precision.HIGHEST)


def _layer_norm(y, g, b):
    mu = jnp.mean(y, -1, keepdims=True)
    yc = y - mu
    var = jnp.mean(yc * yc, -1, keepdims=True)
    return yc * lax.rsqrt(var + EPS) * g + b


def _mod_kernel(c_ref, w_ref, b_ref, o_ref):
    c = c_ref[...]
    s = c * (1.0 / (1.0 + jnp.exp(-c)))
    o_ref[...] = _dot_hi(s, w_ref[...]) + b_ref[...]


def _mod_call(cc, w_mod, b_mod):
    rows = cc.shape[0]
    n_out = w_mod.shape[1]
    bn = 1024
    return pl.pallas_call(
        _mod_kernel,
        grid=(n_out // bn,),
        in_specs=[
            pl.BlockSpec((rows, D_MODEL), lambda j: (0, 0)),
            pl.BlockSpec((D_MODEL, bn), lambda j: (0, j)),
            pl.BlockSpec((1, bn), lambda j: (0, j)),
        ],
        out_specs=pl.BlockSpec((rows, bn), lambda j: (0, j)),
        out_shape=jax.ShapeDtypeStruct((rows, n_out), F32),
        compiler_params=_cparams(("arbitrary",)),
        name="mod",
    )(cc, w_mod, b_mod)


def _ctx_kernel(c_ref, sc_ref, sh_ref, w_ref, kc_ref, vc_ref):
    h = (c_ref[0] * (1.0 + sc_ref[...]) + sh_ref[...]).astype(BF16)
    p = _dot(h, w_ref[...])
    kc_ref[0] = p[:, :2 * QK_COLS].astype(BF16)
    vc_ref[0] = p[:, 2 * QK_COLS:].astype(BF16)


def _ctx_call(ctx, csc1, csh1, w_kv):
    b, cl, _ = ctx.shape
    return pl.pallas_call(
        _ctx_kernel,
        grid=(b,),
        in_specs=[
            pl.BlockSpec((1, cl, D_MODEL), lambda i: (i, 0, 0)),
            pl.BlockSpec((1, D_MODEL), lambda i: (0, 0)),
            pl.BlockSpec((1, D_MODEL), lambda i: (0, 0)),
            pl.BlockSpec((D_MODEL, 2 * QK_COLS + DA_WIDTH), lambda i: (0, 0)),
        ],
        out_specs=[
            pl.BlockSpec((1, cl, 2 * QK_COLS), lambda i: (i, 0, 0)),
            pl.BlockSpec((1, cl, DA_WIDTH), lambda i: (i, 0, 0)),
        ],
        out_shape=[
            jax.ShapeDtypeStruct((b, cl, 2 * QK_COLS), BF16),
            jax.ShapeDtypeStruct((b, cl, DA_WIDTH), BF16),
        ],
        compiler_params=_cparams(("arbitrary",)),
        name="ctx_kv",
    )(ctx, csc1, csh1, w_kv)


def _in_kernel(x_ref, sc_ref, sh_ref, w_ref, cos_ref, sin_ref, lng_ref, lnb_ref,
               sw_ref, sb_ref, q_ref, k_ref, v_ref, sg_ref):
    tm = x_ref.shape[1]
    h = (x_ref[0] * (1.0 + sc_ref[0]) + sh_ref[0]).astype(BF16)
    p = _dot(h, w_ref[...])

    cos = cos_ref[...]
    sin = sin_ref[...]
    lane = lax.broadcasted_iota(jnp.int32, (tm, LANES), 1)
    first = (lane % 32) < 16

    def rope(t):
        partner = jnp.where(first, pltpu.roll(t, LANES - 16, 1), pltpu.roll(t, 16, 1))
        return t * cos + partner * sin

    for c in range(4):
        cs = slice(c * LANES, (c + 1) * LANES)
        q_ref[0, :, cs] = rope(p[:, cs] * Q_SCALE).astype(BF16)
        k_ref[0, :, cs] = rope(p[:, KV_LO + c * LANES:KV_LO + (c + 1) * LANES]).astype(BF16)
    v_ref[0] = p[:, 2 * KV_LO:KV_HI].astype(BF16)

    z = p[:, KV_HI:]
    gz = 0.5 * z * (1.0 + lax.erf(z * SQRT_HALF))
    u = gz[:, :SG_WIDTH]
    vn = _layer_norm(gz[:, SG_WIDTH:], lng_ref[...], lnb_ref[...]).astype(BF16)
    for c in range(tm // SG_CHUNK):
        rs = slice(c * SG_CHUNK, (c + 1) * SG_CHUNK)
        for g in range(SG_GROUPS):
            cs = slice(g * LANES, (g + 1) * LANES)
            s = _dot(sw_ref[g], vn[rs, cs]) + sb_ref[:, cs]
            sg_ref[0, rs, cs] = (u[rs, cs] * s).astype(BF16)


def _in_call(x, sc1, sh1, w_in, cos_t, sin_t, lng, lnb, sw, sbias):
    b, l, _ = x.shape
    tm = TOK_TILE
    full = lambda bi, i: (0, 0)
    return pl.pallas_call(
        _in_kernel,
        grid=(b, l // tm),
        in_specs=[
            pl.BlockSpec((1, tm, D_MODEL), lambda bi, i: (bi, i, 0)),
            pl.BlockSpec((1, 1, D_MODEL), lambda bi, i: (bi, 0, 0)),
            pl.BlockSpec((1, 1, D_MODEL), lambda bi, i: (bi, 0, 0)),
            pl.BlockSpec((D_MODEL, IN_COLS), full),
            pl.BlockSpec((tm, LANES), lambda bi, i: (i, 0)),
            pl.BlockSpec((tm, LANES), lambda bi, i: (i, 0)),
            pl.BlockSpec((1, SG_WIDTH), full),
            pl.BlockSpec((1, SG_WIDTH), full),
            pl.BlockSpec((SG_GROUPS, SG_CHUNK, SG_CHUNK), lambda bi, i: (0, 0, 0)),
            pl.BlockSpec((SG_CHUNK, SG_WIDTH), full),
        ],
        out_specs=[pl.BlockSpec((1, tm, 512), lambda bi, i: (bi, i, 0))] * 4,
        out_shape=[jax.ShapeDtypeStruct((b, l, 512), BF16)] * 4,
        compiler_params=_cparams(("arbitrary", "arbitrary")),
        name="in_proj",
    )(x, sc1, sh1, w_in, cos_t, sin_t, lng, lnb, sw, sbias)


def _attn_kernel(lam_ref, q_ref, k_ref, kc_ref, v_ref, vc_ref, g_ref, o_ref):
    lv = lam_ref[...]
    lam = (jnp.exp(jnp.sum(lv[0:1] * lv[1:2], -1, keepdims=True))
           - jnp.exp(jnp.sum(lv[2:3] * lv[3:4], -1, keepdims=True)) + LAM_INIT)

    q = q_ref[0]
    lane = lax.broadcasted_iota(jnp.int32, q.shape, 1)
    zero = jnp.zeros_like(q)
    nt = (((1,), (1,)), ((), ()))

    def branch(qm):
        sa = lax.dot_general(qm, k_ref[0], nt, preferred_element_type=F32)
        sb = lax.dot_general(qm, kc_ref[0], nt, preferred_element_type=F32)
        m = jnp.maximum(jnp.max(sa, -1, keepdims=True), jnp.max(sb, -1, keepdims=True))
        ea = jnp.exp(sa - m)
        eb = jnp.exp(sb - m)
        den = jnp.sum(ea, -1, keepdims=True) + jnp.sum(eb, -1, keepdims=True)
        o = _dot(ea.astype(BF16), v_ref[0]) + _dot(eb.astype(BF16), vc_ref[0])
        return o / den

    o = branch(jnp.where(lane < DA_HEAD_DIM, q, zero)) - lam * branch(jnp.where(lane >= DA_HEAD_DIM, q, zero))
    of = o * lax.rsqrt(jnp.mean(o * o, -1, keepdims=True) + EPS) * g_ref[...]
    o_ref[0] = (of * (1.0 - LAM_INIT)).astype(BF16)


def _attn_call(lamv, q, k, kc, v, vc, subln_g):
    b, l, _ = q.shape
    cl = kc.shape[1]
    tq = ATTN_TQ
    return pl.pallas_call(
        _attn_kernel,
        grid=(b, DA_HEADS, l // tq),
        in_specs=[
            pl.BlockSpec((4, DA_HEAD_DIM), lambda bi, h, i: (0, 0)),
            pl.BlockSpec((1, tq, LANES), lambda bi, h, i: (bi, i, h)),
            pl.BlockSpec((1, l, LANES), lambda bi, h, i: (bi, 0, h)),
            pl.BlockSpec((1, cl, LANES), lambda bi, h, i: (bi, 0, h)),
            pl.BlockSpec((1, l, LANES), lambda bi, h, i: (bi, 0, h)),
            pl.BlockSpec((1, cl, LANES), lambda bi, h, i: (bi, 0, h)),
            pl.BlockSpec((1, DA_V_DIM), lambda bi, h, i: (0, 0)),
        ],
        out_specs=pl.BlockSpec((1, tq, LANES), lambda bi, h, i: (bi, i, h)),
        out_shape=jax.ShapeDtypeStruct((b, l, DA_WIDTH), BF16),
        compiler_params=_cparams(("arbitrary", "arbitrary", "arbitrary")),
        name="attn",
    )(lamv, q, k, kc, v, vc, subln_g)


def _out_kernel(da_ref, sg_ref, x_ref, g1_ref, sc_ref, sh_ref, w_ref, lg_ref, lb_ref,
                wr_ref, br_ref, x1_ref, t_ref, route_ref, wts_ref, cnt_ref):
    tm = x_ref.shape[0]
    y = _dot(da_ref[...], w_ref[:DA_WIDTH, :]) + _dot(sg_ref[...], w_ref[DA_WIDTH:, :])
    x1 = _layer_norm(DEEPNORM_ALPHA * x_ref[...] + g1_ref[0] * y, lg_ref[...], lb_ref[...])
    x1_ref[...] = x1
    t = x1 * (1.0 + sc_ref[0]) + sh_ref[0]
    t_ref[...] = t

    logits = _dot_hi(t, wr_ref[...]) + br_ref[...]
    lane = lax.broadcasted_iota(jnp.int32, (tm, LANES), 1).astype(F32)
    ninf = jnp.float32(-jnp.inf)
    big = jnp.float32(LANES)
    gmask = lane < MOE_GROUPS
    gl = jnp.where(gmask, logits, ninf)
    gmax = jnp.max(gl, -1, keepdims=True)
    gsel = jnp.min(jnp.where(gl == gmax, lane, big), -1, keepdims=True)
    gsum = jnp.sum(jnp.where(gmask, jnp.exp(gl - gmax), 0.0), -1, keepdims=True)
    gw = 1.0 / gsum
    lo = MOE_GROUPS + MOE_EXPERTS_PER_GROUP * gsel
    emask = (lane >= lo) & (lane < lo + MOE_EXPERTS_PER_GROUP)
    el = jnp.where(emask, logits, ninf)
    v1 = jnp.max(el, -1, keepdims=True)
    i1 = jnp.min(jnp.where(el == v1, lane, big), -1, keepdims=True)
    el2 = jnp.where(lane == i1, ninf, el)
    v2 = jnp.max(el2, -1, keepdims=True)
    i2 = jnp.min(jnp.where(el2 == v2, lane, big), -1, keepdims=True)
    e = jnp.exp(v2 - v1)
    w1 = gw / (1.0 + e)
    w2 = gw * e / (1.0 + e)
    e1 = i1 - MOE_GROUPS
    e2 = i2 - MOE_GROUPS

    oh1 = lane == e1
    oh2 = lane == e2
    oh = jnp.where(oh1 | oh2, 1.0, 0.0)
    r_i = lax.broadcasted_iota(jnp.int32, (tm, tm), 0)
    c_i = lax.broadcasted_iota(jnp.int32, (tm, tm), 1)
    lower = jnp.where(c_i < r_i, 1.0, 0.0).astype(BF16)
    pref = _dot(lower, oh.astype(BF16))
    r1 = jnp.sum(jnp.where(oh1, pref, 0.0), -1, keepdims=True)
    r2 = jnp.sum(jnp.where(oh2, pref, 0.0), -1, keepdims=True)
    cnt_ref[0] = jnp.sum(oh, 0, keepdims=True)

    wts_ref[...] = jnp.where(lane < 64, w1, w2)
    qm = jnp.where(lane == 0, e1, jnp.where(lane == 1, e2, jnp.where(lane == 2, r1, jnp.where(lane == 3, r2, 0.0))))
    route_ref[...] = qm.T[0:8, :]


def _out_call(da, sg, x2d, g1, sc2, sh2, w_out, lg, lb, wr, br, tiles_per_batch):
    n = x2d.shape[0]
    tm = TOK_TILE
    nt = n // tm
    tpb = tiles_per_batch
    row = lambda i: (i, 0)
    full = lambda i: (0, 0)
    per_b = lambda i: (i // tpb, 0, 0)
    return pl.pallas_call(
        _out_kernel,
        grid=(nt,),
        in_specs=[
            pl.BlockSpec((tm, DA_WIDTH), row),
            pl.BlockSpec((tm, SG_WIDTH), row),
            pl.BlockSpec((tm, D_MODEL), row),
            pl.BlockSpec((1, 1, D_MODEL), per_b),
            pl.BlockSpec((1, 1, D_MODEL), per_b),
            pl.BlockSpec((1, 1, D_MODEL), per_b),
            pl.BlockSpec((D_MODEL, D_MODEL), full),
            pl.BlockSpec((1, D_MODEL), full),
            pl.BlockSpec((1, D_MODEL), full),
            pl.BlockSpec((D_MODEL, LANES), full),
            pl.BlockSpec((1, LANES), full),
        ],
        out_specs=[
            pl.BlockSpec((tm, D_MODEL), row),
            pl.BlockSpec((tm, D_MODEL), row),
            pl.BlockSpec((8, tm), lambda i: (0, i)),
            pl.BlockSpec((tm, LANES), row),
            pl.BlockSpec((1, 1, LANES), lambda i: (i, 0, 0)),
        ],
        out_shape=[
            jax.ShapeDtypeStruct((n, D_MODEL), F32),
            jax.ShapeDtypeStruct((n, D_MODEL), F32),
            jax.ShapeDtypeStruct((8, n), F32),
            jax.ShapeDtypeStruct((n, LANES), F32),
            jax.ShapeDtypeStruct((nt, 1, LANES), F32),
        ],
        compiler_params=_cparams(("arbitrary",)),
        name="out_proj",
    )(da, sg, x2d, g1, sc2, sh2, w_out, lg, lb, wr, br)


def _row_copy(src, s_row, dst, d_row, sem):
    return pltpu.make_async_copy(src.at[pl.ds(s_row, 1)], dst.at[pl.ds(d_row, 1)], sem)


def _dispatch_kernel(zs_ref, ze_ref, dest_ref, t_ref, xb_ref, zero_ref, sem):
    i = pl.program_id(0)
    tm = t_ref.shape[0]

    @pl.when(i == 0)
    def _():
        zero_ref[...] = jnp.zeros_like(zero_ref)

        def per_expert(e, carry):
            def start(r, c):
                _row_copy(zero_ref, 0, xb_ref, r, sem).start()
                return c

            def wait(r, c):
                _row_copy(zero_ref, 0, xb_ref, r, sem).wait()
                return c

            lax.fori_loop(zs_ref[e], ze_ref[e], start, 0)
            lax.fori_loop(zs_ref[e], ze_ref[e], wait, 0)
            return carry

        lax.fori_loop(0, N_EXPERTS, per_expert, 0)

        def blk_copy(j):
            return pltpu.make_async_copy(zero_ref, xb_ref.at[pl.ds(pl.multiple_of(j * MOE_BLOCK, MOE_BLOCK), MOE_BLOCK)], sem)

        def blk_start(j, c):
            blk_copy(j).start()
            return c

        def blk_wait(j, c):
            blk_copy(j).wait()
            return c

        first_unused = ze_ref[N_EXPERTS - 1] // MOE_BLOCK
        n_blocks = xb_ref.shape[0] // MOE_BLOCK
        lax.fori_loop(first_unused, n_blocks, blk_start, 0)
        lax.fori_loop(first_unused, n_blocks, blk_wait, 0)

    def start(n, c):
        _row_copy(t_ref, n, xb_ref, dest_ref[0, 0, n], sem).start()
        _row_copy(t_ref, n, xb_ref, dest_ref[0, 0, tm + n], sem).start()
        return c

    def wait(n, c):
        _row_copy(t_ref, n, xb_ref, dest_ref[0, 0, n], sem).wait()
        _row_copy(t_ref, n, xb_ref, dest_ref[0, 0, tm + n], sem).wait()
        return c

    lax.fori_loop(0, tm, start, 0)
    lax.fori_loop(0, tm, wait, 0)


def _dispatch_call(zs, ze, dest3, t, p_rows):
    n = t.shape[0]
    tm = TOK_TILE
    grid_spec = pltpu.PrefetchScalarGridSpec(
        num_scalar_prefetch=2,
        grid=(n // tm,),
        in_specs=[
            pl.BlockSpec((1, 1, 2 * tm), lambda i, zs, ze: (i, 0, 0), memory_space=pltpu.SMEM),
            pl.BlockSpec((tm, D_MODEL), lambda i, zs, ze: (i, 0)),
        ],
        out_specs=pl.BlockSpec(memory_space=pl.ANY),
        scratch_shapes=[pltpu.VMEM((MOE_BLOCK, D_MODEL), F32), pltpu.SemaphoreType.DMA(())],
    )
    return pl.pallas_call(
        _dispatch_kernel,
        grid_spec=grid_spec,
        out_shape=jax.ShapeDtypeStruct((p_rows, D_MODEL), F32),
        compiler_params=_cparams(("arbitrary",)),
        name="dispatch",
    )(zs, ze, dest3, t)


def _expert_kernel(be_ref, nu_ref, x_ref, wg_ref, wu_ref, wd_ref, y_ref, wgb, wub, wdb):
    i = pl.program_id(0)

    @pl.when(i < nu_ref[0])
    def _():
        changed = (i == 0) | (be_ref[i] != be_ref[jnp.maximum(i - 1, 0)])

        @pl.when(changed)
        def _():
            wgb[...] = wg_ref[0].astype(BF16)
            wub[...] = wu_ref[0].astype(BF16)
            wdb[...] = wd_ref[0].astype(BF16)

        x = x_ref[...].astype(BF16)
        g = _dot(x, wgb[...])
        u = _dot(x, wub[...])
        hid = (g * (1.0 / (1.0 + jnp.exp(-g))) * u).astype(BF16)
        y_ref[...] = _dot(hid, wdb[...])

    @pl.when(i >= nu_ref[0])
    def _():
        y_ref[...] = jnp.zeros_like(y_ref)


def _expert_call(blk_e, n_used, xb, wg, wu, wd):
    p_rows = xb.shape[0]
    nb = p_rows // MOE_BLOCK

    def xmap(i, be, nu):
        return (jnp.minimum(i, nu[0] - 1), 0)

    def ymap(i, be, nu):
        return (i, 0)

    def wmap(i, be, nu):
        return (be[jnp.minimum(i, nu[0] - 1)], 0, 0)

    grid_spec = pltpu.PrefetchScalarGridSpec(
        num_scalar_prefetch=2,
        grid=(nb,),
        in_specs=[
            pl.BlockSpec((MOE_BLOCK, D_MODEL), xmap),
            pl.BlockSpec((1, D_MODEL, D_MODEL), wmap),
            pl.BlockSpec((1, D_MODEL, D_MODEL), wmap),
            pl.BlockSpec((1, D_MODEL, D_MODEL), wmap),
        ],
        out_specs=pl.BlockSpec((MOE_BLOCK, D_MODEL), ymap),
        scratch_shapes=[pltpu.VMEM((D_MODEL, D_MODEL), BF16)] * 3,
    )
    return pl.pallas_call(
        _expert_kernel,
        grid_spec=grid_spec,
        out_shape=jax.ShapeDtypeStruct((p_rows, D_MODEL), F32),
        compiler_params=_cparams(("arbitrary",)),
        name="experts",
    )(blk_e, n_used, xb, wg, wu, wd)


def _combine_kernel(dest_ref, yb_ref, x1_ref, wts_ref, g2_ref, lg_ref, lb_ref, o_ref, buf, sem):
    tm = x1_ref.shape[0]

    def start(n, c):
        _row_copy(yb_ref, dest_ref[0, 0, n], buf.at[0], n, sem).start()
        _row_copy(yb_ref, dest_ref[0, 0, tm + n], buf.at[1], n, sem).start()
        return c

    def wait(n, c):
        _row_copy(yb_ref, dest_ref[0, 0, n], buf.at[0], n, sem).wait()
        _row_copy(yb_ref, dest_ref[0, 0, tm + n], buf.at[1], n, sem).wait()
        return c

    lax.fori_loop(0, tm, start, 0)
    lax.fori_loop(0, tm, wait, 0)
    w = wts_ref[...]
    f = w[:, 0:1] * buf[0] + w[:, 64:65] * buf[1]
    o_ref[...] = _layer_norm(DEEPNORM_ALPHA * x1_ref[...] + g2_ref[0] * f, lg_ref[...], lb_ref[...])


def _combine_call(dest3, yb, x1, wts, g2, lg, lb, tiles_per_batch):
    n = x1.shape[0]
    tm = TOK_TILE
    tpb = tiles_per_batch
    row = lambda i: (i, 0)
    full = lambda i: (0, 0)
    return pl.pallas_call(
        _combine_kernel,
        grid=(n // tm,),
        in_specs=[
            pl.BlockSpec((1, 1, 2 * tm), lambda i: (i, 0, 0), memory_space=pltpu.SMEM),
            pl.BlockSpec(memory_space=pl.ANY),
            pl.BlockSpec((tm, D_MODEL), row),
            pl.BlockSpec((tm, LANES), row),
            pl.BlockSpec((1, 1, D_MODEL), lambda i: (i // tpb, 0, 0)),
            pl.BlockSpec((1, D_MODEL), full),
            pl.BlockSpec((1, D_MODEL), full),
        ],
        out_specs=pl.BlockSpec((tm, D_MODEL), row),
        out_shape=jax.ShapeDtypeStruct((n, D_MODEL), F32),
        scratch_shapes=[pltpu.VMEM((2, tm, D_MODEL), F32), pltpu.SemaphoreType.DMA(())],
        compiler_params=_cparams(("arbitrary",)),
        name="combine",
    )(dest3, yb, x1, wts, g2, lg, lb)


def _head_interleave(w, lo):
    blk = w[:, lo:lo + 2 * QK_COLS].reshape(D_MODEL, 2, DA_HEADS, DA_HEAD_DIM)
    return blk.transpose(0, 2, 1, 3).reshape(D_MODEL, 2 * QK_COLS)


def _rope_tables(seq):
    rows_n = seq // GRID_W
    rows = jnp.repeat(jnp.arange(rows_n, dtype=F32), GRID_W)
    cols = jnp.tile(jnp.arange(GRID_W, dtype=F32), rows_n)
    half = DA_HEAD_DIM // 4
    inv = ROPE_THETA ** (-jnp.arange(half, dtype=F32) / half)
    ang_r = rows[:, None] * inv[None, :]
    ang_c = cols[:, None] * inv[None, :]
    cos64 = jnp.concatenate([jnp.cos(ang_r), jnp.cos(ang_r), jnp.cos(ang_c), jnp.cos(ang_c)], -1)
    sin64 = jnp.concatenate([-jnp.sin(ang_r), jnp.sin(ang_r), -jnp.sin(ang_c), jnp.sin(ang_c)], -1)
    return jnp.tile(cos64, (1, 2)), jnp.tile(sin64, (1, 2))


def kernel(x, c, ctx, c_ctx, w_mod, b_mod, w_in, lam_q1, lam_k1, lam_q2, lam_k2, subln_g, sg_ln_g, sg_ln_b, sg_w, sg_b, w_out, ln1_g, ln1_b, router_group_w, router_group_b, router_expert_w, router_expert_b, exp_w_gate, exp_w_up, exp_w_down, ln2_g, ln2_b):
    b, l, d = x.shape
    n = b * l
    tm = TOK_TILE
    nt = n // tm

    cc = jnp.zeros((b + 8, d), F32).at[:b].set(c).at[b].set(c_ctx)
    mod = _mod_call(cc, w_mod[0], b_mod[0][None, :])
    sh1, sc1, g1, sh2, sc2, g2 = [mod[:b, j * d:(j + 1) * d].reshape(b, 1, d) for j in range(6)]
    csh1 = mod[b:b + 1, 0:d]
    csc1 = mod[b:b + 1, d:2 * d]

    wi = w_in[0]
    w_all = jnp.concatenate([_head_interleave(wi, 0), _head_interleave(wi, KV_LO), wi[:, 2 * KV_LO:]], -1).astype(BF16)
    kc, vc = _ctx_call(ctx, csc1, csh1, w_all[:, KV_LO:KV_HI])

    cos_t, sin_t = _rope_tables(l)
    sbias = jnp.repeat(sg_b[0].T, LANES, axis=1)
    q, k, v, sg = _in_call(x, sc1, sh1, w_all, cos_t, sin_t, sg_ln_g[0][None, :], sg_ln_b[0][None, :],
                           sg_w[0].astype(BF16), sbias)

    lamv = jnp.stack([lam_q1[0], lam_k1[0], lam_q2[0], lam_k2[0]]).astype(F32)
    da = _attn_call(lamv, q, k, kc, v, vc, subln_g[0][None, :])

    wr = jnp.zeros((d, LANES), F32).at[:, :MOE_GROUPS].set(router_group_w[0]).at[:, MOE_GROUPS:MOE_GROUPS + N_EXPERTS].set(router_expert_w[0])
    br = jnp.zeros((1, LANES), F32).at[0, :MOE_GROUPS].set(router_group_b[0]).at[0, MOE_GROUPS:MOE_GROUPS + N_EXPERTS].set(router_expert_b[0])
    x1, t, route, wts, tcnt = _out_call(da.reshape(n, DA_WIDTH), sg.reshape(n, SG_WIDTH), x.reshape(n, d), g1, sc2, sh2,
                                        w_out[0].astype(BF16), ln1_g[0][None, :], ln1_b[0][None, :], wr, br, l // tm)

    cnt_te = tcnt[:, 0, :N_EXPERTS].astype(jnp.int32)
    counts = jnp.sum(cnt_te, 0)
    padded = (counts + MOE_BLOCK - 1) // MOE_BLOCK * MOE_BLOCK
    pad_end = jnp.cumsum(padded)
    pad_start = pad_end - padded
    base = pad_start[None, :] + jnp.cumsum(cnt_te, 0) - cnt_te
    ridx = route[:4].astype(jnp.int32).reshape(4, nt, tm)
    ex = jnp.arange(N_EXPERTS, dtype=jnp.int32)

    def slot_dest(eid, rank):
        return jnp.sum(jnp.where(eid[..., None] == ex, base[:, None, :], 0), -1) + rank

    dest3 = jnp.concatenate([slot_dest(ridx[0], ridx[2]), slot_dest(ridx[1], ridx[3])], -1).reshape(nt, 1, 2 * tm)
    n_blocks = (n * 2) // MOE_BLOCK + N_EXPERTS
    p_rows = n_blocks * MOE_BLOCK
    blk_start = jnp.arange(n_blocks, dtype=jnp.int32) * MOE_BLOCK
    blk_e = jnp.minimum(jnp.searchsorted(pad_end, blk_start, side='right'), N_EXPERTS - 1).astype(jnp.int32)
    n_used = (pad_end[-1:] // MOE_BLOCK).astype(jnp.int32)

    xb = _dispatch_call((pad_start + counts).astype(jnp.int32), pad_end.astype(jnp.int32), dest3, t, p_rows)
    yb = _expert_call(blk_e, n_used, xb, exp_w_gate[0], exp_w_up[0], exp_w_down[0])
    out = _combine_call(dest3, yb, x1, wts, g2, ln2_g[0][None, :], ln2_b[0][None, :], l // tm)
    return out.reshape(b, l, d)
```

```python
import math

import jax
import jax.numpy as jnp
from jax import lax
from jax.experimental import pallas as pl
from jax.experimental.pallas import tpu as pltpu

F32 = jnp.float32
BF16 = jnp.bfloat16

D_MODEL = 1024
GRID_W = 64
DA_HEAD_DIM = 64
DA_V_DIM = 128
DA_WIDTH = 512
DA_HEADS = 4
QK_COLS = 256
SG_CHUNK = 128
SG_WIDTH = 512
SG_GROUPS = 4
KV_LO = 512
KV_HI = 1536
IN_COLS = 2560
ROPE_THETA = 10000.0
MOE_GROUPS = 4
MOE_EXPERTS_PER_GROUP = 8
N_EXPERTS = 32
MOE_BLOCK = 128
EPS = 1e-5
DEPTH = 1
DEEPNORM_ALPHA = (2.0 * DEPTH) ** 0.25
LAM_INIT = 0.8 - 0.6 * math.exp(-0.3 * 0)
Q_SCALE = DA_HEAD_DIM ** -0.5
SQRT_HALF = math.sqrt(0.5)

LANES = 128
TOK_TILE = 512
ATTN_TQ = 512
VMEM_LIMIT = 56 * 1024 * 1024


def _cparams(sem):
    return pltpu.CompilerParams(dimension_semantics=sem, vmem_limit_bytes=VMEM_LIMIT)


def _dot(a, b):
    return jnp.dot(a, b, preferred_element_type=F32)


def _dot_hi(a, b):
    return jnp.dot(a, b, preferred_element_type=F32, precision=lax.Precision.HIGHEST)


def _layer_norm(y, g, b):
    mu = jnp.mean(y, -1, keepdims=True)
    yc = y - mu
    var = jnp.mean(yc * yc, -1, keepdims=True)
    return yc * lax.rsqrt(var + EPS) * g + b


ROW_TILE = (8, LANES)


def _store_token_major(ref, val):
    for s in range(ROW_TILE[0]):
        ref[:, s, :] = val[:, s * LANES:(s + 1) * LANES]


def _load_token_major(ref):
    return jnp.concatenate([ref[:, s, :] for s in range(ROW_TILE[0])], axis=1)


def _mod_kernel(c_ref, w_ref, b_ref, o_ref):
    c = c_ref[...]
    s = c * (1.0 / (1.0 + jnp.exp(-c)))
    o_ref[...] = _dot_hi(s, w_ref[...]) + b_ref[...]


def _mod_call(cc, w_mod, b_mod):
    rows = cc.shape[0]
    n_out = w_mod.shape[1]
    bn = 1024
    return pl.pallas_call(
        _mod_kernel,
        grid=(n_out // bn,),
        in_specs=[
            pl.BlockSpec((rows, D_MODEL), lambda j: (0, 0)),
            pl.BlockSpec((D_MODEL, bn), lambda j: (0, j)),
            pl.BlockSpec((1, bn), lambda j: (0, j)),
        ],
        out_specs=pl.BlockSpec((rows, bn), lambda j: (0, j)),
        out_shape=jax.ShapeDtypeStruct((rows, n_out), F32),
        compiler_params=_cparams(("arbitrary",)),
        name="mod",
    )(cc, w_mod, b_mod)


def _ctx_kernel(c_ref, sc_ref, sh_ref, w_ref, kc_ref, vc_ref):
    h = (c_ref[0] * (1.0 + sc_ref[...]) + sh_ref[...]).astype(BF16)
    p = _dot(h, w_ref[...])
    kc_ref[0] = p[:, :2 * QK_COLS].astype(BF16)
    vc_ref[0] = p[:, 2 * QK_COLS:].astype(BF16)


def _ctx_call(ctx, csc1, csh1, w_kv):
    b, cl, _ = ctx.shape
    return pl.pallas_call(
        _ctx_kernel,
        grid=(b,),
        in_specs=[
            pl.BlockSpec((1, cl, D_MODEL), lambda i: (i, 0, 0)),
            pl.BlockSpec((1, D_MODEL), lambda i: (0, 0)),
            pl.BlockSpec((1, D_MODEL), lambda i: (0, 0)),
            pl.BlockSpec((D_MODEL, 2 * QK_COLS + DA_WIDTH), lambda i: (0, 0)),
        ],
        out_specs=[
            pl.BlockSpec((1, cl, 2 * QK_COLS), lambda i: (i, 0, 0)),
            pl.BlockSpec((1, cl, DA_WIDTH), lambda i: (i, 0, 0)),
        ],
        out_shape=[
            jax.ShapeDtypeStruct((b, cl, 2 * QK_COLS), BF16),
            jax.ShapeDtypeStruct((b, cl, DA_WIDTH), BF16),
        ],
        compiler_params=_cparams(("arbitrary",)),
        name="ctx_kv",
    )(ctx, csc1, csh1, w_kv)


def _in_kernel(x_ref, sc_ref, sh_ref, w_ref, cos_ref, sin_ref, lng_ref, lnb_ref,
               sw_ref, sb_ref, q_ref, k_ref, v_ref, sg_ref):
    tm = x_ref.shape[1]
    h = (x_ref[0] * (1.0 + sc_ref[0]) + sh_ref[0]).astype(BF16)
    p = _dot(h, w_ref[...])

    cos = cos_ref[...]
    sin = sin_ref[...]
    lane = lax.broadcasted_iota(jnp.int32, (tm, LANES), 1)
    first = (lane % 32) < 16

    def rope(t):
        partner = jnp.where(first, pltpu.roll(t, LANES - 16, 1), pltpu.roll(t, 16, 1))
        return t * cos + partner * sin

    for c in range(4):
        cs = slice(c * LANES, (c + 1) * LANES)
        q_ref[0, :, cs] = rope(p[:, cs] * Q_SCALE).astype(BF16)
        k_ref[0, :, cs] = rope(p[:, KV_LO + c * LANES:KV_LO + (c + 1) * LANES]).astype(BF16)
    v_ref[0] = p[:, 2 * KV_LO:KV_HI].astype(BF16)

    z = p[:, KV_HI:]
    gz = 0.5 * z * (1.0 + lax.erf(z * SQRT_HALF))
    u = gz[:, :SG_WIDTH]
    vn = _layer_norm(gz[:, SG_WIDTH:], lng_ref[...], lnb_ref[...]).astype(BF16)
    for c in range(tm // SG_CHUNK):
        rs = slice(c * SG_CHUNK, (c + 1) * SG_CHUNK)
        for g in range(SG_GROUPS):
            cs = slice(g * LANES, (g + 1) * LANES)
            s = _dot(sw_ref[g], vn[rs, cs]) + sb_ref[:, cs]
            sg_ref[0, rs, cs] = (u[rs, cs] * s).astype(BF16)


def _in_call(x, sc1, sh1, w_in, cos_t, sin_t, lng, lnb, sw, sbias):
    b, l, _ = x.shape
    tm = TOK_TILE
    full = lambda bi, i: (0, 0)
    return pl.pallas_call(
        _in_kernel,
        grid=(b, l // tm),
        in_specs=[
            pl.BlockSpec((1, tm, D_MODEL), lambda bi, i: (bi, i, 0)),
            pl.BlockSpec((1, 1, D_MODEL), lambda bi, i: (bi, 0, 0)),
            pl.BlockSpec((1, 1, D_MODEL), lambda bi, i: (bi, 0, 0)),
            pl.BlockSpec((D_MODEL, IN_COLS), full),
            pl.BlockSpec((tm, LANES), lambda bi, i: (i, 0)),
            pl.BlockSpec((tm, LANES), lambda bi, i: (i, 0)),
            pl.BlockSpec((1, SG_WIDTH), full),
            pl.BlockSpec((1, SG_WIDTH), full),
            pl.BlockSpec((SG_GROUPS, SG_CHUNK, SG_CHUNK), lambda bi, i: (0, 0, 0)),
            pl.BlockSpec((SG_CHUNK, SG_WIDTH), full),
        ],
        out_specs=[pl.BlockSpec((1, tm, 512), lambda bi, i: (bi, i, 0))] * 4,
        out_shape=[jax.ShapeDtypeStruct((b, l, 512), BF16)] * 4,
        compiler_params=_cparams(("arbitrary", "arbitrary")),
        name="in_proj",
    )(x, sc1, sh1, w_in, cos_t, sin_t, lng, lnb, sw, sbias)


def _attn_kernel(lam_ref, q_ref, k_ref, kc_ref, v_ref, vc_ref, g_ref, o_ref):
    lv = lam_ref[...]
    lam = (jnp.exp(jnp.sum(lv[0:1] * lv[1:2], -1, keepdims=True))
           - jnp.exp(jnp.sum(lv[2:3] * lv[3:4], -1, keepdims=True)) + LAM_INIT)

    q = q_ref[0]
    lane = lax.broadcasted_iota(jnp.int32, q.shape, 1)
    zero = jnp.zeros_like(q)
    nt = (((1,), (1,)), ((), ()))

    def branch(qm):
        sa = lax.dot_general(qm, k_ref[0], nt, preferred_element_type=F32)
        sb = lax.dot_general(qm, kc_ref[0], nt, preferred_element_type=F32)
        m = jnp.maximum(jnp.max(sa, -1, keepdims=True), jnp.max(sb, -1, keepdims=True))
        ea = jnp.exp(sa - m)
        eb = jnp.exp(sb - m)
        den = jnp.sum(ea, -1, keepdims=True) + jnp.sum(eb, -1, keepdims=True)
        o = _dot(ea.astype(BF16), v_ref[0]) + _dot(eb.astype(BF16), vc_ref[0])
        return o / den

    o = branch(jnp.where(lane < DA_HEAD_DIM, q, zero)) - lam * branch(jnp.where(lane >= DA_HEAD_DIM, q, zero))
    of = o * lax.rsqrt(jnp.mean(o * o, -1, keepdims=True) + EPS) * g_ref[...]
    o_ref[0] = (of * (1.0 - LAM_INIT)).astype(BF16)


def _attn_call(lamv, q, k, kc, v, vc, subln_g):
    b, l, _ = q.shape
    cl = kc.shape[1]
    tq = ATTN_TQ
    return pl.pallas_call(
        _attn_kernel,
        grid=(b, DA_HEADS, l // tq),
        in_specs=[
            pl.BlockSpec((4, DA_HEAD_DIM), lambda bi, h, i: (0, 0)),
            pl.BlockSpec((1, tq, LANES), lambda bi, h, i: (bi, i, h)),
            pl.BlockSpec((1, l, LANES), lambda bi, h, i: (bi, 0, h)),
            pl.BlockSpec((1, cl, LANES), lambda bi, h, i: (bi, 0, h)),
            pl.BlockSpec((1, l, LANES), lambda bi, h, i: (bi, 0, h)),
            pl.BlockSpec((1, cl, LANES), lambda bi, h, i: (bi, 0, h)),
            pl.BlockSpec((1, DA_V_DIM), lambda bi, h, i: (0, 0)),
        ],
        out_specs=pl.BlockSpec((1, tq, LANES), lambda bi, h, i: (bi, i, h)),
        out_shape=jax.ShapeDtypeStruct((b, l, DA_WIDTH), BF16),
        compiler_params=_cparams(("arbitrary", "arbitrary", "arbitrary")),
        name="attn",
    )(lamv, q, k, kc, v, vc, subln_g)


def _out_kernel(da_ref, sg_ref, x_ref, g1_ref, sc_ref, sh_ref, w_ref, lg_ref, lb_ref,
                wr_ref, br_ref, x1_ref, t_ref, route_ref, wts_ref, cnt_ref):
    tm = x_ref.shape[0]
    y = _dot(da_ref[...], w_ref[:DA_WIDTH, :]) + _dot(sg_ref[...], w_ref[DA_WIDTH:, :])
    x1 = _layer_norm(DEEPNORM_ALPHA * x_ref[...] + g1_ref[0] * y, lg_ref[...], lb_ref[...])
    x1_ref[...] = x1
    t = x1 * (1.0 + sc_ref[0]) + sh_ref[0]
    _store_token_major(t_ref, t)

    t_hi = t.astype(BF16)
    t_lo = (t - t_hi.astype(F32)).astype(BF16)
    hw = _dot(t_hi, wr_ref[...])
    logits = hw[:, :LANES] + hw[:, LANES:] + _dot(t_lo, wr_ref[:, :LANES]) + br_ref[...]
    lane = lax.broadcasted_iota(jnp.int32, (tm, LANES), 1).astype(F32)
    ninf = jnp.float32(-jnp.inf)
    big = jnp.float32(LANES)
    gmask = lane < MOE_GROUPS
    gl = jnp.where(gmask, logits, ninf)
    gmax = jnp.max(gl, -1, keepdims=True)
    gsel = jnp.min(jnp.where(gl == gmax, lane, big), -1, keepdims=True)
    gsum = jnp.sum(jnp.where(gmask, jnp.exp(gl - gmax), 0.0), -1, keepdims=True)
    gw = 1.0 / gsum
    lo = MOE_GROUPS + MOE_EXPERTS_PER_GROUP * gsel
    emask = (lane >= lo) & (lane < lo + MOE_EXPERTS_PER_GROUP)
    el = jnp.where(emask, logits, ninf)
    v1 = jnp.max(el, -1, keepdims=True)
    i1 = jnp.min(jnp.where(el == v1, lane, big), -1, keepdims=True)
    el2 = jnp.where(lane == i1, ninf, el)
    v2 = jnp.max(el2, -1, keepdims=True)
    i2 = jnp.min(jnp.where(el2 == v2, lane, big), -1, keepdims=True)
    e = jnp.exp(v2 - v1)
    w1 = gw / (1.0 + e)
    w2 = gw * e / (1.0 + e)
    e1 = i1 - MOE_GROUPS
    e2 = i2 - MOE_GROUPS

    oh1 = lane == e1
    oh2 = lane == e2
    oh = jnp.where(oh1 | oh2, 1.0, 0.0)
    r_i = lax.broadcasted_iota(jnp.int32, (tm, tm), 0)
    c_i = lax.broadcasted_iota(jnp.int32, (tm, tm), 1)
    lower = jnp.where(c_i < r_i, 1.0, 0.0).astype(BF16)
    pref = _dot(lower, oh.astype(BF16))
    r1 = jnp.sum(jnp.where(oh1, pref, 0.0), -1, keepdims=True)
    r2 = jnp.sum(jnp.where(oh2, pref, 0.0), -1, keepdims=True)
    cnt_ref[0] = jnp.sum(oh, 0, keepdims=True)

    wts_ref[...] = jnp.where(lane < 64, w1, w2)
    qm = jnp.where(lane == 0, e1, jnp.where(lane == 1, e2, jnp.where(lane == 2, r1, jnp.where(lane == 3, r2, 0.0))))
    route_ref[...] = qm.T[0:8, :]


def _out_call(da, sg, x2d, g1, sc2, sh2, w_out, lg, lb, wr, br, tiles_per_batch):
    n = x2d.shape[0]
    tm = TOK_TILE
    nt = n // tm
    tpb = tiles_per_batch
    row = lambda i: (i, 0)
    full = lambda i: (0, 0)
    per_b = lambda i: (i // tpb, 0, 0)
    return pl.pallas_call(
        _out_kernel,
        grid=(nt,),
        in_specs=[
            pl.BlockSpec((tm, DA_WIDTH), row),
            pl.BlockSpec((tm, SG_WIDTH), row),
            pl.BlockSpec((tm, D_MODEL), row),
            pl.BlockSpec((1, 1, D_MODEL), per_b),
            pl.BlockSpec((1, 1, D_MODEL), per_b),
            pl.BlockSpec((1, 1, D_MODEL), per_b),
            pl.BlockSpec((D_MODEL, D_MODEL), full),
            pl.BlockSpec((1, D_MODEL), full),
            pl.BlockSpec((1, D_MODEL), full),
            pl.BlockSpec((D_MODEL, 2 * LANES), full),
            pl.BlockSpec((1, LANES), full),
        ],
        out_specs=[
            pl.BlockSpec((tm, D_MODEL), row),
            pl.BlockSpec((tm,) + ROW_TILE, lambda i: (i, 0, 0)),
            pl.BlockSpec((8, tm), lambda i: (0, i)),
            pl.BlockSpec((tm, LANES), row),
            pl.BlockSpec((1, 1, LANES), lambda i: (i, 0, 0)),
        ],
        out_shape=[
            jax.ShapeDtypeStruct((n, D_MODEL), F32),
            jax.ShapeDtypeStruct((n,) + ROW_TILE, F32),
            jax.ShapeDtypeStruct((8, n), F32),
            jax.ShapeDtypeStruct((n, LANES), F32),
            jax.ShapeDtypeStruct((nt, 1, LANES), F32),
        ],
        compiler_params=_cparams(("arbitrary",)),
        name="out_proj",
    )(da, sg, x2d, g1, sc2, sh2, w_out, lg, lb, wr, br)


ISSUE_UNROLL = 8


def _dispatch_kernel(zs_ref, ze_ref, dest_ref, t_ref, xb_ref, zero_ref, sem):
    i = pl.program_id(0)
    tm = t_ref.shape[0]

    @pl.when(i == 0)
    def _():
        zero_ref[...] = jnp.zeros_like(zero_ref)

        def per_expert(e, carry):
            def start(r, c):
                pltpu.make_async_copy(zero_ref.at[0], xb_ref.at[r], sem).start()
                return c

            def wait(r, c):
                pltpu.make_async_copy(zero_ref.at[0], xb_ref.at[r], sem).wait()
                return c

            lax.fori_loop(zs_ref[e], ze_ref[e], start, 0)
            lax.fori_loop(zs_ref[e], ze_ref[e], wait, 0)
            return carry

        lax.fori_loop(0, N_EXPERTS, per_expert, 0)

        def blk_copy(j):
            return pltpu.make_async_copy(zero_ref, xb_ref.at[pl.ds(pl.multiple_of(j * MOE_BLOCK, MOE_BLOCK), MOE_BLOCK)], sem)

        def blk_start(j, c):
            blk_copy(j).start()
            return c

        def blk_wait(j, c):
            blk_copy(j).wait()
            return c

        first_unused = ze_ref[N_EXPERTS - 1] // MOE_BLOCK
        n_blocks = xb_ref.shape[0] // MOE_BLOCK
        lax.fori_loop(first_unused, n_blocks, blk_start, 0)
        lax.fori_loop(first_unused, n_blocks, blk_wait, 0)

    def start(n, c):
        pltpu.make_async_copy(t_ref.at[n], xb_ref.at[dest_ref[0, 0, n]], sem).start()
        pltpu.make_async_copy(t_ref.at[n], xb_ref.at[dest_ref[0, 0, tm + n]], sem).start()
        return c

    lax.fori_loop(0, tm, start, 0, unroll=ISSUE_UNROLL)
    for _ in range(2):
        pltpu.make_async_copy(t_ref, xb_ref.at[pl.ds(0, tm)], sem).wait()


def _dispatch_call(zs, ze, dest3, t, p_rows):
    n = t.shape[0]
    tm = TOK_TILE
    grid_spec = pltpu.PrefetchScalarGridSpec(
        num_scalar_prefetch=2,
        grid=(n // tm,),
        in_specs=[
            pl.BlockSpec((1, 1, 2 * tm), lambda i, zs, ze: (i, 0, 0), memory_space=pltpu.SMEM),
            pl.BlockSpec((tm,) + ROW_TILE, lambda i, zs, ze: (i, 0, 0)),
        ],
        out_specs=pl.BlockSpec(memory_space=pl.ANY),
        scratch_shapes=[pltpu.VMEM((MOE_BLOCK,) + ROW_TILE, F32), pltpu.SemaphoreType.DMA(())],
    )
    return pl.pallas_call(
        _dispatch_kernel,
        grid_spec=grid_spec,
        out_shape=jax.ShapeDtypeStruct((p_rows,) + ROW_TILE, F32),
        compiler_params=_cparams(("arbitrary",)),
        name="dispatch",
    )(zs, ze, dest3, t)


def _expert_kernel(be_ref, nu_ref, x_ref, wg_ref, wu_ref, wd_ref, y_ref, wgb, wub, wdb):
    i = pl.program_id(0)

    @pl.when(i < nu_ref[0])
    def _():
        changed = (i == 0) | (be_ref[i] != be_ref[jnp.maximum(i - 1, 0)])

        @pl.when(changed)
        def _():
            wgb[...] = wg_ref[0].astype(BF16)
            wub[...] = wu_ref[0].astype(BF16)
            wdb[...] = wd_ref[0].astype(BF16)

        x = _load_token_major(x_ref).astype(BF16)
        g = _dot(x, wgb[...])
        u = _dot(x, wub[...])
        hid = (g * (1.0 / (1.0 + jnp.exp(-g))) * u).astype(BF16)
        _store_token_major(y_ref, _dot(hid, wdb[...]))

    @pl.when(i >= nu_ref[0])
    def _():
        y_ref[...] = jnp.zeros_like(y_ref)


def _expert_call(blk_e, n_used, xb, wg, wu, wd):
    p_rows = xb.shape[0]
    nb = p_rows // MOE_BLOCK

    def xmap(i, be, nu):
        return (jnp.minimum(i, nu[0] - 1), 0, 0)

    def ymap(i, be, nu):
        return (i, 0, 0)

    def wmap(i, be, nu):
        return (be[jnp.minimum(i, nu[0] - 1)], 0, 0)

    grid_spec = pltpu.PrefetchScalarGridSpec(
        num_scalar_prefetch=2,
        grid=(nb,),
        in_specs=[
            pl.BlockSpec((MOE_BLOCK,) + ROW_TILE, xmap),
            pl.BlockSpec((1, D_MODEL, D_MODEL), wmap),
            pl.BlockSpec((1, D_MODEL, D_MODEL), wmap),
            pl.BlockSpec((1, D_MODEL, D_MODEL), wmap),
        ],
        out_specs=pl.BlockSpec((MOE_BLOCK,) + ROW_TILE, ymap),
        scratch_shapes=[pltpu.VMEM((D_MODEL, D_MODEL), BF16)] * 3,
    )
    return pl.pallas_call(
        _expert_kernel,
        grid_spec=grid_spec,
        out_shape=jax.ShapeDtypeStruct((p_rows,) + ROW_TILE, F32),
        compiler_params=_cparams(("arbitrary",)),
        name="experts",
    )(blk_e, n_used, xb, wg, wu, wd)


def _combine_kernel(dest_ref, yb_ref, x1_ref, wts_ref, g2_ref, lg_ref, lb_ref, o_ref, buf, sem):
    tm = x1_ref.shape[0]

    def start(n, c):
        pltpu.make_async_copy(yb_ref.at[dest_ref[0, 0, n]], buf.at[0, n], sem).start()
        pltpu.make_async_copy(yb_ref.at[dest_ref[0, 0, tm + n]], buf.at[1, n], sem).start()
        return c

    lax.fori_loop(0, tm, start, 0, unroll=ISSUE_UNROLL)
    for k in range(2):
        pltpu.make_async_copy(yb_ref.at[pl.ds(0, tm)], buf.at[k], sem).wait()
    w = wts_ref[...]
    f = w[:, 0:1] * _load_token_major(buf.at[0]) + w[:, 64:65] * _load_token_major(buf.at[1])
    o_ref[...] = _layer_norm(DEEPNORM_ALPHA * x1_ref[...] + g2_ref[0] * f, lg_ref[...], lb_ref[...])


def _combine_call(dest3, yb, x1, wts, g2, lg, lb, tiles_per_batch):
    n = x1.shape[0]
    tm = TOK_TILE
    tpb = tiles_per_batch
    row = lambda i: (i, 0)
    full = lambda i: (0, 0)
    return pl.pallas_call(
        _combine_kernel,
        grid=(n // tm,),
        in_specs=[
            pl.BlockSpec((1, 1, 2 * tm), lambda i: (i, 0, 0), memory_space=pltpu.SMEM),
            pl.BlockSpec(memory_space=pl.ANY),
            pl.BlockSpec((tm, D_MODEL), row),
            pl.BlockSpec((tm, LANES), row),
            pl.BlockSpec((1, 1, D_MODEL), lambda i: (i // tpb, 0, 0)),
            pl.BlockSpec((1, D_MODEL), full),
            pl.BlockSpec((1, D_MODEL), full),
        ],
        out_specs=pl.BlockSpec((tm, D_MODEL), row),
        out_shape=jax.ShapeDtypeStruct((n, D_MODEL), F32),
        scratch_shapes=[pltpu.VMEM((2, tm) + ROW_TILE, F32), pltpu.SemaphoreType.DMA(())],
        compiler_params=_cparams(("arbitrary",)),
        name="combine",
    )(dest3, yb, x1, wts, g2, lg, lb)


def _head_interleave(w, lo):
    blk = w[:, lo:lo + 2 * QK_COLS].reshape(D_MODEL, 2, DA_HEADS, DA_HEAD_DIM)
    return blk.transpose(0, 2, 1, 3).reshape(D_MODEL, 2 * QK_COLS)


def _rope_tables(seq):
    rows_n = seq // GRID_W
    rows = jnp.repeat(jnp.arange(rows_n, dtype=F32), GRID_W)
    cols = jnp.tile(jnp.arange(GRID_W, dtype=F32), rows_n)
    half = DA_HEAD_DIM // 4
    inv = ROPE_THETA ** (-jnp.arange(half, dtype=F32) / half)
    ang_r = rows[:, None] * inv[None, :]
    ang_c = cols[:, None] * inv[None, :]
    cos64 = jnp.concatenate([jnp.cos(ang_r), jnp.cos(ang_r), jnp.cos(ang_c), jnp.cos(ang_c)], -1)
    sin64 = jnp.concatenate([-jnp.sin(ang_r), jnp.sin(ang_r), -jnp.sin(ang_c), jnp.sin(ang_c)], -1)
    return jnp.tile(cos64, (1, 2)), jnp.tile(sin64, (1, 2))


def kernel(x, c, ctx, c_ctx, w_mod, b_mod, w_in, lam_q1, lam_k1, lam_q2, lam_k2, subln_g, sg_ln_g, sg_ln_b, sg_w, sg_b, w_out, ln1_g, ln1_b, router_group_w, router_group_b, router_expert_w, router_expert_b, exp_w_gate, exp_w_up, exp_w_down, ln2_g, ln2_b):
    b, l, d = x.shape
    n = b * l
    tm = TOK_TILE
    nt = n // tm

    cc = jnp.zeros((b + 8, d), F32).at[:b].set(c).at[b].set(c_ctx)
    mod = _mod_call(cc, w_mod[0], b_mod[0][None, :])
    sh1, sc1, g1, sh2, sc2, g2 = [mod[:b, j * d:(j + 1) * d].reshape(b, 1, d) for j in range(6)]
    csh1 = mod[b:b + 1, 0:d]
    csc1 = mod[b:b + 1, d:2 * d]

    wi = w_in[0]
    w_all = jnp.concatenate([_head_interleave(wi, 0), _head_interleave(wi, KV_LO), wi[:, 2 * KV_LO:]], -1).astype(BF16)
    kc, vc = _ctx_call(ctx, csc1, csh1, w_all[:, KV_LO:KV_HI])

    cos_t, sin_t = _rope_tables(l)
    sbias = jnp.repeat(sg_b[0].T, LANES, axis=1)
    q, k, v, sg = _in_call(x, sc1, sh1, w_all, cos_t, sin_t, sg_ln_g[0][None, :], sg_ln_b[0][None, :],
                           sg_w[0].astype(BF16), sbias)

    lamv = jnp.stack([lam_q1[0], lam_k1[0], lam_q2[0], lam_k2[0]]).astype(F32)
    da = _attn_call(lamv, q, k, kc, v, vc, subln_g[0][None, :])

    wr = jnp.zeros((d, LANES), F32).at[:, :MOE_GROUPS].set(router_group_w[0]).at[:, MOE_GROUPS:MOE_GROUPS + N_EXPERTS].set(router_expert_w[0])
    br = jnp.zeros((1, LANES), F32).at[0, :MOE_GROUPS].set(router_group_b[0]).at[0, MOE_GROUPS:MOE_GROUPS + N_EXPERTS].set(router_expert_b[0])
    wr_hi = wr.astype(BF16)
    wr_split = jnp.concatenate([wr_hi, (wr - wr_hi.astype(F32)).astype(BF16)], -1)
    x1, t, route, wts, tcnt = _out_call(da.reshape(n, DA_WIDTH), sg.reshape(n, SG_WIDTH), x.reshape(n, d), g1, sc2, sh2,
                                        w_out[0].astype(BF16), ln1_g[0][None, :], ln1_b[0][None, :], wr_split, br, l // tm)

    cnt_te = tcnt[:, 0, :N_EXPERTS].astype(jnp.int32)
    counts = jnp.sum(cnt_te, 0)
    padded = (counts + MOE_BLOCK - 1) // MOE_BLOCK * MOE_BLOCK
    pad_end = jnp.cumsum(padded)
    pad_start = pad_end - padded
    base = pad_start[None, :] + jnp.cumsum(cnt_te, 0) - cnt_te
    ridx = route[:4].astype(jnp.int32).reshape(4, nt, tm)
    ex = jnp.arange(N_EXPERTS, dtype=jnp.int32)

    def slot_dest(eid, rank):
        return jnp.sum(jnp.where(eid[..., None] == ex, base[:, None, :], 0), -1) + rank

    dest3 = jnp.concatenate([slot_dest(ridx[0], ridx[2]), slot_dest(ridx[1], ridx[3])], -1).reshape(nt, 1, 2 * tm)
    n_blocks = (n * 2) // MOE_BLOCK + N_EXPERTS
    p_rows = n_blocks * MOE_BLOCK
    blk_start = jnp.arange(n_blocks, dtype=jnp.int32) * MOE_BLOCK
    blk_e = jnp.minimum(jnp.sum((blk_start[:, None] >= pad_end[None, :]).astype(jnp.int32), -1), N_EXPERTS - 1)
    n_used = (pad_end[-1:] // MOE_BLOCK).astype(jnp.int32)

    xb = _dispatch_call((pad_start + counts).astype(jnp.int32), pad_end.astype(jnp.int32), dest3, t, p_rows)
    yb = _expert_call(blk_e, n_used, xb, exp_w_gate[0], exp_w_up[0], exp_w_down[0])
    out = _combine_call(dest3, yb, x1, wts, g2, ln2_g[0][None, :], ln2_b[0][None, :], l // tm)
    return out.reshape(b, l, d)
```

```python
import math

import jax
import jax.numpy as jnp
from jax import lax
from jax.experimental import pallas as pl
from jax.experimental.pallas import tpu as pltpu

F32 = jnp.float32
BF16 = jnp.bfloat16

D_MODEL = 1024
GRID_W = 64
DA_HEAD_DIM = 64
DA_V_DIM = 128
DA_WIDTH = 512
DA_HEADS = 4
QK_COLS = 256
SG_CHUNK = 128
SG_WIDTH = 512
SG_GROUPS = 4
KV_LO = 512
KV_HI = 1536
IN_COLS = 2560
ROPE_THETA = 10000.0
MOE_GROUPS = 4
MOE_EXPERTS_PER_GROUP = 8
N_EXPERTS = 32
MOE_BLOCK = 128
EPS = 1e-5
DEPTH = 1
DEEPNORM_ALPHA = (2.0 * DEPTH) ** 0.25
LAM_INIT = 0.8 - 0.6 * math.exp(-0.3 * 0)
Q_SCALE = DA_HEAD_DIM ** -0.5
SQRT_HALF = math.sqrt(0.5)

LANES = 128
TOK_TILE = 512
ATTN_TQ = 512
VMEM_LIMIT = 56 * 1024 * 1024


def _cparams(sem):
    return pltpu.CompilerParams(dimension_semantics=sem, vmem_limit_bytes=VMEM_LIMIT)


def _dot(a, b):
    return jnp.dot(a, b, preferred_element_type=F32)


def _dot_hi(a, b):
    return jnp.dot(a, b, preferred_element_type=F32, precision=lax.Precision.HIGHEST)


def _layer_norm(y, g, b):
    mu = jnp.mean(y, -1, keepdims=True)
    yc = y - mu
    var = jnp.mean(yc * yc, -1, keepdims=True)
    return yc * lax.rsqrt(var + EPS) * g + b


TOKEN_MAJOR_ROWS = False
ROW_TILE = (8, LANES) if TOKEN_MAJOR_ROWS else (D_MODEL,)
ROW_ZEROS = (0,) * len(ROW_TILE)


def _row(ref, n):
    return ref.at[n] if TOKEN_MAJOR_ROWS else ref.at[pl.ds(n, 1)]


def _store_token_major(ref, val):
    if not TOKEN_MAJOR_ROWS:
        ref[...] = val
        return
    for s in range(ROW_TILE[0]):
        ref[:, s, :] = val[:, s * LANES:(s + 1) * LANES]


def _load_token_major(ref):
    if not TOKEN_MAJOR_ROWS:
        return ref[...]
    return jnp.concatenate([ref[:, s, :] for s in range(ROW_TILE[0])], axis=1)


def _mod_kernel(c_ref, w_ref, b_ref, o_ref):
    c = c_ref[...]
    s = c * (1.0 / (1.0 + jnp.exp(-c)))
    o_ref[...] = _dot_hi(s, w_ref[...]) + b_ref[...]


def _mod_call(cc, w_mod, b_mod):
    rows = cc.shape[0]
    n_out = w_mod.shape[1]
    bn = 1024
    return pl.pallas_call(
        _mod_kernel,
        grid=(n_out // bn,),
        in_specs=[
            pl.BlockSpec((rows, D_MODEL), lambda j: (0, 0)),
            pl.BlockSpec((D_MODEL, bn), lambda j: (0, j)),
            pl.BlockSpec((1, bn), lambda j: (0, j)),
        ],
        out_specs=pl.BlockSpec((rows, bn), lambda j: (0, j)),
        out_shape=jax.ShapeDtypeStruct((rows, n_out), F32),
        compiler_params=_cparams(("arbitrary",)),
        name="mod",
    )(cc, w_mod, b_mod)


def _ctx_kernel(c_ref, sc_ref, sh_ref, w_ref, kc_ref, vc_ref):
    h = (c_ref[0] * (1.0 + sc_ref[...]) + sh_ref[...]).astype(BF16)
    p = _dot(h, w_ref[...])
    kc_ref[0] = p[:, :2 * QK_COLS].astype(BF16)
    vc_ref[0] = p[:, 2 * QK_COLS:].astype(BF16)


def _ctx_call(ctx, csc1, csh1, w_kv):
    b, cl, _ = ctx.shape
    return pl.pallas_call(
        _ctx_kernel,
        grid=(b,),
        in_specs=[
            pl.BlockSpec((1, cl, D_MODEL), lambda i: (i, 0, 0)),
            pl.BlockSpec((1, D_MODEL), lambda i: (0, 0)),
            pl.BlockSpec((1, D_MODEL), lambda i: (0, 0)),
            pl.BlockSpec((D_MODEL, 2 * QK_COLS + DA_WIDTH), lambda i: (0, 0)),
        ],
        out_specs=[
            pl.BlockSpec((1, cl, 2 * QK_COLS), lambda i: (i, 0, 0)),
            pl.BlockSpec((1, cl, DA_WIDTH), lambda i: (i, 0, 0)),
        ],
        out_shape=[
            jax.ShapeDtypeStruct((b, cl, 2 * QK_COLS), BF16),
            jax.ShapeDtypeStruct((b, cl, DA_WIDTH), BF16),
        ],
        compiler_params=_cparams(("arbitrary",)),
        name="ctx_kv",
    )(ctx, csc1, csh1, w_kv)


def _in_kernel(x_ref, sc_ref, sh_ref, w_ref, cos_ref, sin_ref, lng_ref, lnb_ref,
               sw_ref, sb_ref, q_ref, k_ref, v_ref, sg_ref):
    tm = x_ref.shape[1]
    h = (x_ref[0] * (1.0 + sc_ref[0]) + sh_ref[0]).astype(BF16)
    p = _dot(h, w_ref[...])

    cos = cos_ref[...]
    sin = sin_ref[...]
    lane = lax.broadcasted_iota(jnp.int32, (tm, LANES), 1)
    first = (lane % 32) < 16

    def rope(t):
        partner = jnp.where(first, pltpu.roll(t, LANES - 16, 1), pltpu.roll(t, 16, 1))
        return t * cos + partner * sin

    for c in range(4):
        cs = slice(c * LANES, (c + 1) * LANES)
        q_ref[0, :, cs] = rope(p[:, cs] * Q_SCALE).astype(BF16)
        k_ref[0, :, cs] = rope(p[:, KV_LO + c * LANES:KV_LO + (c + 1) * LANES]).astype(BF16)
    v_ref[0] = p[:, 2 * KV_LO:KV_HI].astype(BF16)

    z = p[:, KV_HI:]
    gz = 0.5 * z * (1.0 + lax.erf(z * SQRT_HALF))
    u = gz[:, :SG_WIDTH]
    vn = _layer_norm(gz[:, SG_WIDTH:], lng_ref[...], lnb_ref[...]).astype(BF16)
    for c in range(tm // SG_CHUNK):
        rs = slice(c * SG_CHUNK, (c + 1) * SG_CHUNK)
        for g in range(SG_GROUPS):
            cs = slice(g * LANES, (g + 1) * LANES)
            s = _dot(sw_ref[g], vn[rs, cs]) + sb_ref[:, cs]
            sg_ref[0, rs, cs] = (u[rs, cs] * s).astype(BF16)


def _in_call(x, sc1, sh1, w_in, cos_t, sin_t, lng, lnb, sw, sbias):
    b, l, _ = x.shape
    tm = TOK_TILE
    full = lambda bi, i: (0, 0)
    return pl.pallas_call(
        _in_kernel,
        grid=(b, l // tm),
        in_specs=[
            pl.BlockSpec((1, tm, D_MODEL), lambda bi, i: (bi, i, 0)),
            pl.BlockSpec((1, 1, D_MODEL), lambda bi, i: (bi, 0, 0)),
            pl.BlockSpec((1, 1, D_MODEL), lambda bi, i: (bi, 0, 0)),
            pl.BlockSpec((D_MODEL, IN_COLS), full),
            pl.BlockSpec((tm, LANES), lambda bi, i: (i, 0)),
            pl.BlockSpec((tm, LANES), lambda bi, i: (i, 0)),
            pl.BlockSpec((1, SG_WIDTH), full),
            pl.BlockSpec((1, SG_WIDTH), full),
            pl.BlockSpec((SG_GROUPS, SG_CHUNK, SG_CHUNK), lambda bi, i: (0, 0, 0)),
            pl.BlockSpec((SG_CHUNK, SG_WIDTH), full),
        ],
        out_specs=[pl.BlockSpec((1, tm, 512), lambda bi, i: (bi, i, 0))] * 4,
        out_shape=[jax.ShapeDtypeStruct((b, l, 512), BF16)] * 4,
        compiler_params=_cparams(("arbitrary", "arbitrary")),
        name="in_proj",
    )(x, sc1, sh1, w_in, cos_t, sin_t, lng, lnb, sw, sbias)


def _attn_kernel(lam_ref, q_ref, k_ref, kc_ref, v_ref, vc_ref, g_ref, o_ref):
    lv = lam_ref[...]
    lam = (jnp.exp(jnp.sum(lv[0:1] * lv[1:2], -1, keepdims=True))
           - jnp.exp(jnp.sum(lv[2:3] * lv[3:4], -1, keepdims=True)) + LAM_INIT)

    q = q_ref[0]
    lane = lax.broadcasted_iota(jnp.int32, q.shape, 1)
    zero = jnp.zeros_like(q)
    nt = (((1,), (1,)), ((), ()))

    def branch(qm):
        sa = lax.dot_general(qm, k_ref[0], nt, preferred_element_type=F32)
        sb = lax.dot_general(qm, kc_ref[0], nt, preferred_element_type=F32)
        m = jnp.maximum(jnp.max(sa, -1, keepdims=True), jnp.max(sb, -1, keepdims=True))
        ea = jnp.exp(sa - m)
        eb = jnp.exp(sb - m)
        den = jnp.sum(ea, -1, keepdims=True) + jnp.sum(eb, -1, keepdims=True)
        o = _dot(ea.astype(BF16), v_ref[0]) + _dot(eb.astype(BF16), vc_ref[0])
        return o / den

    o = branch(jnp.where(lane < DA_HEAD_DIM, q, zero)) - lam * branch(jnp.where(lane >= DA_HEAD_DIM, q, zero))
    of = o * lax.rsqrt(jnp.mean(o * o, -1, keepdims=True) + EPS) * g_ref[...]
    o_ref[0] = (of * (1.0 - LAM_INIT)).astype(BF16)


def _attn_call(lamv, q, k, kc, v, vc, subln_g):
    b, l, _ = q.shape
    cl = kc.shape[1]
    tq = ATTN_TQ
    return pl.pallas_call(
        _attn_kernel,
        grid=(b, DA_HEADS, l // tq),
        in_specs=[
            pl.BlockSpec((4, DA_HEAD_DIM), lambda bi, h, i: (0, 0)),
            pl.BlockSpec((1, tq, LANES), lambda bi, h, i: (bi, i, h)),
            pl.BlockSpec((1, l, LANES), lambda bi, h, i: (bi, 0, h)),
            pl.BlockSpec((1, cl, LANES), lambda bi, h, i: (bi, 0, h)),
            pl.BlockSpec((1, l, LANES), lambda bi, h, i: (bi, 0, h)),
            pl.BlockSpec((1, cl, LANES), lambda bi, h, i: (bi, 0, h)),
            pl.BlockSpec((1, DA_V_DIM), lambda bi, h, i: (0, 0)),
        ],
        out_specs=pl.BlockSpec((1, tq, LANES), lambda bi, h, i: (bi, i, h)),
        out_shape=jax.ShapeDtypeStruct((b, l, DA_WIDTH), BF16),
        compiler_params=_cparams(("arbitrary", "arbitrary", "arbitrary")),
        name="attn",
    )(lamv, q, k, kc, v, vc, subln_g)


def _out_kernel(da_ref, sg_ref, x_ref, g1_ref, sc_ref, sh_ref, w_ref, lg_ref, lb_ref,
                wr_ref, br_ref, x1_ref, t_ref, route_ref, wts_ref, cnt_ref):
    tm = x_ref.shape[0]
    y = _dot(da_ref[...], w_ref[:DA_WIDTH, :]) + _dot(sg_ref[...], w_ref[DA_WIDTH:, :])
    x1 = _layer_norm(DEEPNORM_ALPHA * x_ref[...] + g1_ref[0] * y, lg_ref[...], lb_ref[...])
    x1_ref[...] = x1
    t = x1 * (1.0 + sc_ref[0]) + sh_ref[0]
    _store_token_major(t_ref, t)

    t_hi = t.astype(BF16)
    t_lo = (t - t_hi.astype(F32)).astype(BF16)
    hw = _dot(t_hi, wr_ref[...])
    logits = hw[:, :LANES] + hw[:, LANES:] + _dot(t_lo, wr_ref[:, :LANES]) + br_ref[...]
    lane = lax.broadcasted_iota(jnp.int32, (tm, LANES), 1).astype(F32)
    ninf = jnp.float32(-jnp.inf)
    big = jnp.float32(LANES)
    gmask = lane < MOE_GROUPS
    gl = jnp.where(gmask, logits, ninf)
    gmax = jnp.max(gl, -1, keepdims=True)
    gsel = jnp.min(jnp.where(gl == gmax, lane, big), -1, keepdims=True)
    gsum = jnp.sum(jnp.where(gmask, jnp.exp(gl - gmax), 0.0), -1, keepdims=True)
    gw = 1.0 / gsum
    lo = MOE_GROUPS + MOE_EXPERTS_PER_GROUP * gsel
    emask = (lane >= lo) & (lane < lo + MOE_EXPERTS_PER_GROUP)
    el = jnp.where(emask, logits, ninf)
    v1 = jnp.max(el, -1, keepdims=True)
    i1 = jnp.min(jnp.where(el == v1, lane, big), -1, keepdims=True)
    el2 = jnp.where(lane == i1, ninf, el)
    v2 = jnp.max(el2, -1, keepdims=True)
    i2 = jnp.min(jnp.where(el2 == v2, lane, big), -1, keepdims=True)
    e = jnp.exp(v2 - v1)
    w1 = gw / (1.0 + e)
    w2 = gw * e / (1.0 + e)
    e1 = i1 - MOE_GROUPS
    e2 = i2 - MOE_GROUPS

    oh1 = lane == e1
    oh2 = lane == e2
    oh = jnp.where(oh1 | oh2, 1.0, 0.0)
    r_i = lax.broadcasted_iota(jnp.int32, (tm, tm), 0)
    c_i = lax.broadcasted_iota(jnp.int32, (tm, tm), 1)
    lower = jnp.where(c_i < r_i, 1.0, 0.0).astype(BF16)
    pref = _dot(lower, oh.astype(BF16))
    r1 = jnp.sum(jnp.where(oh1, pref, 0.0), -1, keepdims=True)
    r2 = jnp.sum(jnp.where(oh2, pref, 0.0), -1, keepdims=True)
    cnt_ref[0] = jnp.sum(oh, 0, keepdims=True)

    wts_ref[...] = jnp.where(lane < 64, w1, w2)
    qm = jnp.where(lane == 0, e1, jnp.where(lane == 1, e2, jnp.where(lane == 2, r1, jnp.where(lane == 3, r2, 0.0))))
    route_ref[...] = qm.T[0:8, :]


def _out_call(da, sg, x2d, g1, sc2, sh2, w_out, lg, lb, wr, br, tiles_per_batch):
    n = x2d.shape[0]
    tm = TOK_TILE
    nt = n // tm
    tpb = tiles_per_batch
    row = lambda i: (i, 0)
    full = lambda i: (0, 0)
    per_b = lambda i: (i // tpb, 0, 0)
    return pl.pallas_call(
        _out_kernel,
        grid=(nt,),
        in_specs=[
            pl.BlockSpec((tm, DA_WIDTH), row),
            pl.BlockSpec((tm, SG_WIDTH), row),
            pl.BlockSpec((tm, D_MODEL), row),
            pl.BlockSpec((1, 1, D_MODEL), per_b),
            pl.BlockSpec((1, 1, D_MODEL), per_b),
            pl.BlockSpec((1, 1, D_MODEL), per_b),
            pl.BlockSpec((D_MODEL, D_MODEL), full),
            pl.BlockSpec((1, D_MODEL), full),
            pl.BlockSpec((1, D_MODEL), full),
            pl.BlockSpec((D_MODEL, 2 * LANES), full),
            pl.BlockSpec((1, LANES), full),
        ],
        out_specs=[
            pl.BlockSpec((tm, D_MODEL), row),
            pl.BlockSpec((tm,) + ROW_TILE, lambda i: (i,) + ROW_ZEROS),
            pl.BlockSpec((8, tm), lambda i: (0, i)),
            pl.BlockSpec((tm, LANES), row),
            pl.BlockSpec((1, 1, LANES), lambda i: (i, 0, 0)),
        ],
        out_shape=[
            jax.ShapeDtypeStruct((n, D_MODEL), F32),
            jax.ShapeDtypeStruct((n,) + ROW_TILE, F32),
            jax.ShapeDtypeStruct((8, n), F32),
            jax.ShapeDtypeStruct((n, LANES), F32),
            jax.ShapeDtypeStruct((nt, 1, LANES), F32),
        ],
        compiler_params=_cparams(("arbitrary",)),
        name="out_proj",
    )(da, sg, x2d, g1, sc2, sh2, w_out, lg, lb, wr, br)


ISSUE_UNROLL = 8


def _dispatch_kernel(zs_ref, ze_ref, dest_ref, t_ref, xb_ref, zero_ref, sem):
    i = pl.program_id(0)
    tm = t_ref.shape[0]

    @pl.when(i == 0)
    def _():
        zero_ref[...] = jnp.zeros_like(zero_ref)

        def per_expert(e, carry):
            def start(r, c):
                pltpu.make_async_copy(_row(zero_ref, 0), _row(xb_ref, r), sem).start()
                return c

            def wait(r, c):
                pltpu.make_async_copy(_row(zero_ref, 0), _row(xb_ref, r), sem).wait()
                return c

            lax.fori_loop(zs_ref[e], ze_ref[e], start, 0)
            lax.fori_loop(zs_ref[e], ze_ref[e], wait, 0)
            return carry

        lax.fori_loop(0, N_EXPERTS, per_expert, 0)

        def blk_copy(j):
            return pltpu.make_async_copy(zero_ref, xb_ref.at[pl.ds(pl.multiple_of(j * MOE_BLOCK, MOE_BLOCK), MOE_BLOCK)], sem)

        def blk_start(j, c):
            blk_copy(j).start()
            return c

        def blk_wait(j, c):
            blk_copy(j).wait()
            return c

        first_unused = ze_ref[N_EXPERTS - 1] // MOE_BLOCK
        n_blocks = xb_ref.shape[0] // MOE_BLOCK
        lax.fori_loop(first_unused, n_blocks, blk_start, 0)
        lax.fori_loop(first_unused, n_blocks, blk_wait, 0)

    def start(n, c):
        pltpu.make_async_copy(_row(t_ref, n), _row(xb_ref, dest_ref[0, 0, n]), sem).start()
        pltpu.make_async_copy(_row(t_ref, n), _row(xb_ref, dest_ref[0, 0, tm + n]), sem).start(priority=1)
        return c

    lax.fori_loop(0, tm, start, 0, unroll=ISSUE_UNROLL)
    for _ in range(2):
        pltpu.make_async_copy(t_ref, xb_ref.at[pl.ds(0, tm)], sem).wait()


def _dispatch_call(zs, ze, dest3, t, p_rows):
    n = t.shape[0]
    tm = TOK_TILE
    grid_spec = pltpu.PrefetchScalarGridSpec(
        num_scalar_prefetch=2,
        grid=(n // tm,),
        in_specs=[
            pl.BlockSpec((1, 1, 2 * tm), lambda i, zs, ze: (i, 0, 0), memory_space=pltpu.SMEM),
            pl.BlockSpec((tm,) + ROW_TILE, lambda i, zs, ze: (i,) + ROW_ZEROS),
        ],
        out_specs=pl.BlockSpec(memory_space=pl.ANY),
        scratch_shapes=[pltpu.VMEM((MOE_BLOCK,) + ROW_TILE, F32), pltpu.SemaphoreType.DMA(())],
    )
    return pl.pallas_call(
        _dispatch_kernel,
        grid_spec=grid_spec,
        out_shape=jax.ShapeDtypeStruct((p_rows,) + ROW_TILE, F32),
        compiler_params=_cparams(("arbitrary",)),
        name="dispatch",
    )(zs, ze, dest3, t)


ITEM_SUB = 4
ITEM_ROWS = ITEM_SUB * MOE_BLOCK


def _expert_kernel(ie_ref, ib_ref, ins_ref, ni_ref, xb_ref, wg_ref, wu_ref, wd_ref, yb_ref,
                   xbuf, ybuf, wgb, wub, wdb, insem, outsem):
    i = pl.program_id(0)
    n_steps = pl.num_programs(0)
    ni = ni_ref[0]
    slot = i % 2

    def rows_of(item, j):
        return pl.ds(pl.multiple_of((ib_ref[item] + j) * MOE_BLOCK, MOE_BLOCK), MOE_BLOCK)

    def in_copy(item, s, j):
        return pltpu.make_async_copy(xb_ref.at[rows_of(item, j)], xbuf.at[s, pl.ds(j * MOE_BLOCK, MOE_BLOCK)], insem.at[s])

    def out_copy(item, s, j):
        return pltpu.make_async_copy(ybuf.at[s, pl.ds(j * MOE_BLOCK, MOE_BLOCK)], yb_ref.at[rows_of(item, j)], outsem.at[s])

    def for_blocks(item, fn):
        for j in range(ITEM_SUB):
            @pl.when(j < ins_ref[item])
            def _():
                fn(j)

    @pl.when(i == 0)
    def _():
        for_blocks(0, lambda j: in_copy(0, 0, j).start())

    @pl.when(i < ni)
    def _():
        @pl.when(i >= 2)
        def _():
            for_blocks(i - 2, lambda j: out_copy(i - 2, slot, j).wait())

        @pl.when(i + 1 < ni)
        def _():
            for_blocks(i + 1, lambda j: in_copy(i + 1, 1 - slot, j).start())

        for_blocks(i, lambda j: in_copy(i, slot, j).wait())

        changed = (i == 0) | (ie_ref[i] != ie_ref[jnp.maximum(i - 1, 0)])

        @pl.when(changed)
        def _():
            wgb[...] = wg_ref[0].astype(BF16)
            wub[...] = wu_ref[0].astype(BF16)
            wdb[...] = wd_ref[0].astype(BF16)

        for ns in range(1, ITEM_SUB + 1):
            @pl.when(ins_ref[i] == ns)
            def _():
                rows = pl.ds(0, ns * MOE_BLOCK)
                x = _load_token_major(xbuf.at[slot, rows]).astype(BF16)
                g = _dot(x, wgb[...])
                u = _dot(x, wub[...])
                hid = (g * (1.0 / (1.0 + jnp.exp(-g))) * u).astype(BF16)
                _store_token_major(ybuf.at[slot, rows], _dot(hid, wdb[...]))

        for_blocks(i, lambda j: out_copy(i, slot, j).start())

    @pl.when(i == n_steps - 1)
    def _():
        for back in (2, 1):
            k = ni - back

            @pl.when(k >= 0)
            def _():
                for_blocks(k, lambda j: out_copy(k, k % 2, j).wait())

        ybuf[0, pl.ds(0, MOE_BLOCK)] = jnp.zeros((MOE_BLOCK,) + ROW_TILE, F32)
        last = jnp.maximum(ni - 1, 0)
        first_unused = ib_ref[last] + ins_ref[last]
        n_blocks = yb_ref.shape[0] // MOE_BLOCK

        def zero_copy(b):
            dst = yb_ref.at[pl.ds(pl.multiple_of(b * MOE_BLOCK, MOE_BLOCK), MOE_BLOCK)]
            return pltpu.make_async_copy(ybuf.at[0, pl.ds(0, MOE_BLOCK)], dst, outsem.at[0])

        def z_start(b, c):
            zero_copy(b).start()
            return c

        def z_wait(b, c):
            zero_copy(b).wait()
            return c

        lax.fori_loop(first_unused, n_blocks, z_start, 0)
        lax.fori_loop(first_unused, n_blocks, z_wait, 0)


def _expert_call(item_e, item_b, item_ns, n_items, xb, wg, wu, wd):
    p_rows = xb.shape[0]
    max_items = item_e.shape[0]

    def wmap(i, ie, ib, ins, ni):
        return (ie[jnp.minimum(i, ni[0] - 1)], 0, 0)

    grid_spec = pltpu.PrefetchScalarGridSpec(
        num_scalar_prefetch=4,
        grid=(max_items,),
        in_specs=[
            pl.BlockSpec(memory_space=pl.ANY),
            pl.BlockSpec((1, D_MODEL, D_MODEL), wmap),
            pl.BlockSpec((1, D_MODEL, D_MODEL), wmap),
            pl.BlockSpec((1, D_MODEL, D_MODEL), wmap),
        ],
        out_specs=pl.BlockSpec(memory_space=pl.ANY),
        scratch_shapes=[
            pltpu.VMEM((2, ITEM_ROWS) + ROW_TILE, F32),
            pltpu.VMEM((2, ITEM_ROWS) + ROW_TILE, F32),
            pltpu.VMEM((D_MODEL, D_MODEL), BF16),
            pltpu.VMEM((D_MODEL, D_MODEL), BF16),
            pltpu.VMEM((D_MODEL, D_MODEL), BF16),
            pltpu.SemaphoreType.DMA((2,)),
            pltpu.SemaphoreType.DMA((2,)),
        ],
    )
    return pl.pallas_call(
        _expert_kernel,
        grid_spec=grid_spec,
        out_shape=jax.ShapeDtypeStruct((p_rows,) + ROW_TILE, F32),
        compiler_params=_cparams(("arbitrary",)),
        name="experts",
    )(item_e, item_b, item_ns, n_items, xb, wg, wu, wd)


def _combine_kernel(dest_ref, yb_ref, x1_ref, wts_ref, g2_ref, lg_ref, lb_ref, o_ref, buf, sem):
    tm = x1_ref.shape[0]

    def start(n, c):
        pltpu.make_async_copy(_row(yb_ref, dest_ref[0, 0, n]), _row(buf.at[0], n), sem).start()
        pltpu.make_async_copy(_row(yb_ref, dest_ref[0, 0, tm + n]), _row(buf.at[1], n), sem).start(priority=1)
        return c

    lax.fori_loop(0, tm, start, 0, unroll=ISSUE_UNROLL)
    for k in range(2):
        pltpu.make_async_copy(yb_ref.at[pl.ds(0, tm)], buf.at[k], sem).wait()
    w = wts_ref[...]
    f = w[:, 0:1] * _load_token_major(buf.at[0]) + w[:, 64:65] * _load_token_major(buf.at[1])
    o_ref[...] = _layer_norm(DEEPNORM_ALPHA * x1_ref[...] + g2_ref[0] * f, lg_ref[...], lb_ref[...])


def _combine_call(dest3, yb, x1, wts, g2, lg, lb, tiles_per_batch):
    n = x1.shape[0]
    tm = TOK_TILE
    tpb = tiles_per_batch
    row = lambda i: (i, 0)
    full = lambda i: (0, 0)
    return pl.pallas_call(
        _combine_kernel,
        grid=(n // tm,),
        in_specs=[
            pl.BlockSpec((1, 1, 2 * tm), lambda i: (i, 0, 0), memory_space=pltpu.SMEM),
            pl.BlockSpec(memory_space=pl.ANY),
            pl.BlockSpec((tm, D_MODEL), row),
            pl.BlockSpec((tm, LANES), row),
            pl.BlockSpec((1, 1, D_MODEL), lambda i: (i // tpb, 0, 0)),
            pl.BlockSpec((1, D_MODEL), full),
            pl.BlockSpec((1, D_MODEL), full),
        ],
        out_specs=pl.BlockSpec((tm, D_MODEL), row),
        out_shape=jax.ShapeDtypeStruct((n, D_MODEL), F32),
        scratch_shapes=[pltpu.VMEM((2, tm) + ROW_TILE, F32), pltpu.SemaphoreType.DMA(())],
        compiler_params=_cparams(("arbitrary",)),
        name="combine",
    )(dest3, yb, x1, wts, g2, lg, lb)


def _head_interleave(w, lo):
    blk = w[:, lo:lo + 2 * QK_COLS].reshape(D_MODEL, 2, DA_HEADS, DA_HEAD_DIM)
    return blk.transpose(0, 2, 1, 3).reshape(D_MODEL, 2 * QK_COLS)


def _rope_tables(seq):
    rows_n = seq // GRID_W
    rows = jnp.repeat(jnp.arange(rows_n, dtype=F32), GRID_W)
    cols = jnp.tile(jnp.arange(GRID_W, dtype=F32), rows_n)
    half = DA_HEAD_DIM // 4
    inv = ROPE_THETA ** (-jnp.arange(half, dtype=F32) / half)
    ang_r = rows[:, None] * inv[None, :]
    ang_c = cols[:, None] * inv[None, :]
    cos64 = jnp.concatenate([jnp.cos(ang_r), jnp.cos(ang_r), jnp.cos(ang_c), jnp.cos(ang_c)], -1)
    sin64 = jnp.concatenate([-jnp.sin(ang_r), jnp.sin(ang_r), -jnp.sin(ang_c), jnp.sin(ang_c)], -1)
    return jnp.tile(cos64, (1, 2)), jnp.tile(sin64, (1, 2))


def kernel(x, c, ctx, c_ctx, w_mod, b_mod, w_in, lam_q1, lam_k1, lam_q2, lam_k2, subln_g, sg_ln_g, sg_ln_b, sg_w, sg_b, w_out, ln1_g, ln1_b, router_group_w, router_group_b, router_expert_w, router_expert_b, exp_w_gate, exp_w_up, exp_w_down, ln2_g, ln2_b):
    b, l, d = x.shape
    n = b * l
    tm = TOK_TILE
    nt = n // tm

    cc = jnp.zeros((b + 8, d), F32).at[:b].set(c).at[b].set(c_ctx)
    mod = _mod_call(cc, w_mod[0], b_mod[0][None, :])
    sh1, sc1, g1, sh2, sc2, g2 = [mod[:b, j * d:(j + 1) * d].reshape(b, 1, d) for j in range(6)]
    csh1 = mod[b:b + 1, 0:d]
    csc1 = mod[b:b + 1, d:2 * d]

    wi = w_in[0]
    w_all = jnp.concatenate([_head_interleave(wi, 0), _head_interleave(wi, KV_LO), wi[:, 2 * KV_LO:]], -1).astype(BF16)
    kc, vc = _ctx_call(ctx, csc1, csh1, w_all[:, KV_LO:KV_HI])

    cos_t, sin_t = _rope_tables(l)
    sbias = jnp.repeat(sg_b[0].T, LANES, axis=1)
    q, k, v, sg = _in_call(x, sc1, sh1, w_all, cos_t, sin_t, sg_ln_g[0][None, :], sg_ln_b[0][None, :],
                           sg_w[0].astype(BF16), sbias)

    lamv = jnp.stack([lam_q1[0], lam_k1[0], lam_q2[0], lam_k2[0]]).astype(F32)
    da = _attn_call(lamv, q, k, kc, v, vc, subln_g[0][None, :])

    wr = jnp.zeros((d, LANES), F32).at[:, :MOE_GROUPS].set(router_group_w[0]).at[:, MOE_GROUPS:MOE_GROUPS + N_EXPERTS].set(router_expert_w[0])
    br = jnp.zeros((1, LANES), F32).at[0, :MOE_GROUPS].set(router_group_b[0]).at[0, MOE_GROUPS:MOE_GROUPS + N_EXPERTS].set(router_expert_b[0])
    wr_hi = wr.astype(BF16)
    wr_split = jnp.concatenate([wr_hi, (wr - wr_hi.astype(F32)).astype(BF16)], -1)
    x1, t, route, wts, tcnt = _out_call(da.reshape(n, DA_WIDTH), sg.reshape(n, SG_WIDTH), x.reshape(n, d), g1, sc2, sh2,
                                        w_out[0].astype(BF16), ln1_g[0][None, :], ln1_b[0][None, :], wr_split, br, l // tm)

    cnt_te = tcnt[:, 0, :N_EXPERTS].astype(jnp.int32)
    counts = jnp.sum(cnt_te, 0)
    padded = (counts + MOE_BLOCK - 1) // MOE_BLOCK * MOE_BLOCK
    pad_end = jnp.cumsum(padded)
    pad_start = pad_end - padded
    base = pad_start[None, :] + jnp.cumsum(cnt_te, 0) - cnt_te
    ridx = route[:4].astype(jnp.int32).reshape(4, nt, tm)
    ex = jnp.arange(N_EXPERTS, dtype=jnp.int32)

    def slot_dest(eid, rank):
        return jnp.sum(jnp.where(eid[..., None] == ex, base[:, None, :], 0), -1) + rank

    dest3 = jnp.concatenate([slot_dest(ridx[0], ridx[2]), slot_dest(ridx[1], ridx[3])], -1).reshape(nt, 1, 2 * tm)
    n_blocks = (n * 2) // MOE_BLOCK + N_EXPERTS
    p_rows = n_blocks * MOE_BLOCK
    nb_e = padded // MOE_BLOCK
    items_e = (nb_e + ITEM_SUB - 1) // ITEM_SUB
    item_end = jnp.cumsum(items_e)
    max_items = (n_blocks + (ITEM_SUB - 1) * N_EXPERTS) // ITEM_SUB
    it = jnp.arange(max_items, dtype=jnp.int32)
    item_e = jnp.minimum(jnp.sum((it[:, None] >= item_end[None, :]).astype(jnp.int32), -1), N_EXPERTS - 1)
    item_j = it - (item_end - items_e)[item_e]
    item_b = (pad_start // MOE_BLOCK)[item_e] + ITEM_SUB * item_j
    item_ns = jnp.clip(nb_e[item_e] - ITEM_SUB * item_j, 0, ITEM_SUB)

    xb = _dispatch_call((pad_start + counts).astype(jnp.int32), pad_end.astype(jnp.int32), dest3, t, p_rows)
    yb = _expert_call(item_e, item_b.astype(jnp.int32), item_ns.astype(jnp.int32), item_end[-1:].astype(jnp.int32),
                      xb, exp_w_gate[0], exp_w_up[0], exp_w_down[0])
    out = _combine_call(dest3, yb, x1, wts, g2, ln2_g[0][None, :], ln2_b[0][None, :], l // tm)
    return out.reshape(b, l, d)
```

```python
import math

import jax
import jax.numpy as jnp
from jax import lax
from jax.experimental import pallas as pl
from jax.experimental.pallas import tpu as pltpu

F32 = jnp.float32
BF16 = jnp.bfloat16

D_MODEL = 1024
GRID_W = 64
DA_HEAD_DIM = 64
DA_V_DIM = 128
DA_WIDTH = 512
DA_HEADS = 4
QK_COLS = 256
SG_CHUNK = 128
SG_WIDTH = 512
SG_GROUPS = 4
KV_LO = 512
KV_HI = 1536
IN_COLS = 2560
ROPE_THETA = 10000.0
MOE_GROUPS = 4
MOE_EXPERTS_PER_GROUP = 8
N_EXPERTS = 32
MOE_BLOCK = 128
EPS = 1e-5
DEPTH = 1
DEEPNORM_ALPHA = (2.0 * DEPTH) ** 0.25
LAM_INIT = 0.8 - 0.6 * math.exp(-0.3 * 0)
Q_SCALE = DA_HEAD_DIM ** -0.5
LOG2E = math.log2(math.e)
SQRT_HALF = math.sqrt(0.5)

LANES = 128
TOK_TILE = 512
ATTN_TQ = 2048
ATTN_SUB = 256
VMEM_LIMIT = 56 * 1024 * 1024


def _cparams(sem):
    return pltpu.CompilerParams(dimension_semantics=sem, vmem_limit_bytes=VMEM_LIMIT)


def _dot(a, b):
    return jnp.dot(a, b, preferred_element_type=F32)


def _dot_hi(a, b):
    return jnp.dot(a, b, preferred_element_type=F32, precision=lax.Precision.HIGHEST)


def _layer_norm(y, g, b):
    mu = jnp.mean(y, -1, keepdims=True)
    yc = y - mu
    var = jnp.mean(yc * yc, -1, keepdims=True)
    return yc * lax.rsqrt(var + EPS) * g + b


TOKEN_MAJOR_ROWS = False
ROW_TILE = (8, LANES) if TOKEN_MAJOR_ROWS else (D_MODEL,)
ROW_ZEROS = (0,) * len(ROW_TILE)


def _row(ref, n):
    return ref.at[n] if TOKEN_MAJOR_ROWS else ref.at[pl.ds(n, 1)]


def _store_token_major(ref, val):
    if not TOKEN_MAJOR_ROWS:
        ref[...] = val
        return
    for s in range(ROW_TILE[0]):
        ref[:, s, :] = val[:, s * LANES:(s + 1) * LANES]


def _load_token_major(ref):
    if not TOKEN_MAJOR_ROWS:
        return ref[...]
    return jnp.concatenate([ref[:, s, :] for s in range(ROW_TILE[0])], axis=1)


def _mod_kernel(c_ref, w_ref, b_ref, o_ref):
    c = c_ref[...]
    s = c * (1.0 / (1.0 + jnp.exp(-c)))
    o_ref[...] = _dot_hi(s, w_ref[...]) + b_ref[...]


def _mod_call(cc, w_mod, b_mod):
    rows = cc.shape[0]
    n_out = w_mod.shape[1]
    bn = 1024
    return pl.pallas_call(
        _mod_kernel,
        grid=(n_out // bn,),
        in_specs=[
            pl.BlockSpec((rows, D_MODEL), lambda j: (0, 0)),
            pl.BlockSpec((D_MODEL, bn), lambda j: (0, j)),
            pl.BlockSpec((1, bn), lambda j: (0, j)),
        ],
        out_specs=pl.BlockSpec((rows, bn), lambda j: (0, j)),
        out_shape=jax.ShapeDtypeStruct((rows, n_out), F32),
        compiler_params=_cparams(("arbitrary",)),
        name="mod",
    )(cc, w_mod, b_mod)


def _ctx_kernel(c_ref, sc_ref, sh_ref, w_ref, kc_ref, vc_ref):
    h = (c_ref[0] * (1.0 + sc_ref[...]) + sh_ref[...]).astype(BF16)
    p = _dot(h, w_ref[...])
    kc_ref[0] = p[:, :2 * QK_COLS].astype(BF16)
    vc_ref[0] = p[:, 2 * QK_COLS:].astype(BF16)


def _ctx_call(ctx, csc1, csh1, w_kv):
    b, cl, _ = ctx.shape
    return pl.pallas_call(
        _ctx_kernel,
        grid=(b,),
        in_specs=[
            pl.BlockSpec((1, cl, D_MODEL), lambda i: (i, 0, 0)),
            pl.BlockSpec((1, D_MODEL), lambda i: (0, 0)),
            pl.BlockSpec((1, D_MODEL), lambda i: (0, 0)),
            pl.BlockSpec((D_MODEL, 2 * QK_COLS + DA_WIDTH), lambda i: (0, 0)),
        ],
        out_specs=[
            pl.BlockSpec((1, cl, 2 * QK_COLS), lambda i: (i, 0, 0)),
            pl.BlockSpec((1, cl, DA_WIDTH), lambda i: (i, 0, 0)),
        ],
        out_shape=[
            jax.ShapeDtypeStruct((b, cl, 2 * QK_COLS), BF16),
            jax.ShapeDtypeStruct((b, cl, DA_WIDTH), BF16),
        ],
        compiler_params=_cparams(("arbitrary",)),
        name="ctx_kv",
    )(ctx, csc1, csh1, w_kv)


def _in_kernel(x_ref, sc_ref, sh_ref, w_ref, cos_ref, sin_ref, lng_ref, lnb_ref,
               sw_ref, sb_ref, q_ref, k_ref, v_ref, sg_ref):
    tm = x_ref.shape[1]
    h = (x_ref[0] * (1.0 + sc_ref[0]) + sh_ref[0]).astype(BF16)
    p = _dot(h, w_ref[...])

    cos = cos_ref[...]
    sin = sin_ref[...]
    lane = lax.broadcasted_iota(jnp.int32, (tm, LANES), 1)
    first = (lane % 32) < 16

    def rope(t):
        partner = jnp.where(first, pltpu.roll(t, LANES - 16, 1), pltpu.roll(t, 16, 1))
        return t * cos + partner * sin

    for c in range(4):
        cs = slice(c * LANES, (c + 1) * LANES)
        q_ref[0, :, cs] = rope(p[:, cs] * (Q_SCALE * LOG2E)).astype(BF16)
        k_ref[0, :, cs] = rope(p[:, KV_LO + c * LANES:KV_LO + (c + 1) * LANES]).astype(BF16)
    v_ref[0] = p[:, 2 * KV_LO:KV_HI].astype(BF16)

    z = p[:, KV_HI:]
    gz = 0.5 * z * (1.0 + lax.erf(z * SQRT_HALF))
    u = gz[:, :SG_WIDTH]
    vn = _layer_norm(gz[:, SG_WIDTH:], lng_ref[...], lnb_ref[...]).astype(BF16)
    for c in range(tm // SG_CHUNK):
        rs = slice(c * SG_CHUNK, (c + 1) * SG_CHUNK)
        for g in range(SG_GROUPS):
            cs = slice(g * LANES, (g + 1) * LANES)
            s = _dot(sw_ref[g], vn[rs, cs]) + sb_ref[:, cs]
            sg_ref[0, rs, cs] = (u[rs, cs] * s).astype(BF16)


def _in_call(x, sc1, sh1, w_in, cos_t, sin_t, lng, lnb, sw, sbias):
    b, l, _ = x.shape
    tm = TOK_TILE
    full = lambda bi, i: (0, 0)
    return pl.pallas_call(
        _in_kernel,
        grid=(b, l // tm),
        in_specs=[
            pl.BlockSpec((1, tm, D_MODEL), lambda bi, i: (bi, i, 0)),
            pl.BlockSpec((1, 1, D_MODEL), lambda bi, i: (bi, 0, 0)),
            pl.BlockSpec((1, 1, D_MODEL), lambda bi, i: (bi, 0, 0)),
            pl.BlockSpec((D_MODEL, IN_COLS), full),
            pl.BlockSpec((tm, LANES), lambda bi, i: (i, 0)),
            pl.BlockSpec((tm, LANES), lambda bi, i: (i, 0)),
            pl.BlockSpec((1, SG_WIDTH), full),
            pl.BlockSpec((1, SG_WIDTH), full),
            pl.BlockSpec((SG_GROUPS, SG_CHUNK, SG_CHUNK), lambda bi, i: (0, 0, 0)),
            pl.BlockSpec((SG_CHUNK, SG_WIDTH), full),
        ],
        out_specs=[pl.BlockSpec((1, tm, 512), lambda bi, i: (bi, i, 0))] * 4,
        out_shape=[jax.ShapeDtypeStruct((b, l, 512), BF16)] * 4,
        compiler_params=_cparams(("arbitrary", "arbitrary")),
        name="in_proj",
    )(x, sc1, sh1, w_in, cos_t, sin_t, lng, lnb, sw, sbias)


def _attn_kernel(lam_ref, q_ref, k_ref, kc_ref, v_ref, vc_ref, g_ref, o_ref, kall, vall):
    lv = lam_ref[...]
    lam = (jnp.exp(jnp.sum(lv[0:1] * lv[1:2], -1, keepdims=True))
           - jnp.exp(jnp.sum(lv[2:3] * lv[3:4], -1, keepdims=True)) + LAM_INIT)
    l = k_ref.shape[1]

    @pl.when(pl.program_id(2) == 0)
    def _():
        kall[:l, :] = k_ref[0]
        kall[l:, :] = kc_ref[0]
        vall[:l, :DA_V_DIM] = v_ref[0]
        vall[l:, :DA_V_DIM] = vc_ref[0]
        vall[:, DA_V_DIM:] = jnp.ones((vall.shape[0], DA_V_DIM), BF16)

    nt = (((1,), (1,)), ((), ()))

    def branch(qm):
        s = lax.dot_general(qm, kall[...], nt, preferred_element_type=F32)
        m = jnp.max(s, -1, keepdims=True)
        e = jnp.exp2((s - m).astype(BF16))
        oe = _dot(e, vall[...])
        return oe[:, :DA_V_DIM] / oe[:, DA_V_DIM:DA_V_DIM + 1]

    for r in range(q_ref.shape[1] // ATTN_SUB):
        rs = pl.ds(r * ATTN_SUB, ATTN_SUB)
        q = q_ref[0, rs, :]
        lane = lax.broadcasted_iota(jnp.int32, q.shape, 1)
        zero = jnp.zeros_like(q)
        o = branch(jnp.where(lane < DA_HEAD_DIM, q, zero)) - lam * branch(jnp.where(lane >= DA_HEAD_DIM, q, zero))
        of = o * lax.rsqrt(jnp.mean(o * o, -1, keepdims=True) + EPS) * g_ref[...]
        o_ref[0, rs, :] = (of * (1.0 - LAM_INIT)).astype(BF16)


def _attn_call(lamv, q, k, kc, v, vc, subln_g):
    b, l, _ = q.shape
    cl = kc.shape[1]
    tq = ATTN_TQ
    return pl.pallas_call(
        _attn_kernel,
        grid=(b, DA_HEADS, l // tq),
        in_specs=[
            pl.BlockSpec((4, DA_HEAD_DIM), lambda bi, h, i: (0, 0)),
            pl.BlockSpec((1, tq, LANES), lambda bi, h, i: (bi, i, h)),
            pl.BlockSpec((1, l, LANES), lambda bi, h, i: (bi, 0, h)),
            pl.BlockSpec((1, cl, LANES), lambda bi, h, i: (bi, 0, h)),
            pl.BlockSpec((1, l, LANES), lambda bi, h, i: (bi, 0, h)),
            pl.BlockSpec((1, cl, LANES), lambda bi, h, i: (bi, 0, h)),
            pl.BlockSpec((1, DA_V_DIM), lambda bi, h, i: (0, 0)),
        ],
        out_specs=pl.BlockSpec((1, tq, LANES), lambda bi, h, i: (bi, i, h)),
        out_shape=jax.ShapeDtypeStruct((b, l, DA_WIDTH), BF16),
        scratch_shapes=[pltpu.VMEM((l + cl, LANES), BF16), pltpu.VMEM((l + cl, 2 * DA_V_DIM), BF16)],
        compiler_params=_cparams(("arbitrary", "arbitrary", "arbitrary")),
        name="attn",
    )(lamv, q, k, kc, v, vc, subln_g)


def _out_kernel(da_ref, sg_ref, x_ref, g1_ref, sc_ref, sh_ref, w_ref, lg_ref, lb_ref,
                wr_ref, br_ref, x1_ref, t_ref, route_ref, wts_ref, cnt_ref):
    tm = x_ref.shape[0]
    y = _dot(da_ref[...], w_ref[:DA_WIDTH, :]) + _dot(sg_ref[...], w_ref[DA_WIDTH:, :])
    x1 = _layer_norm(DEEPNORM_ALPHA * x_ref[...] + g1_ref[0] * y, lg_ref[...], lb_ref[...])
    x1_ref[...] = x1
    t = x1 * (1.0 + sc_ref[0]) + sh_ref[0]
    _store_token_major(t_ref, t)

    t_hi = t.astype(BF16)
    t_lo = (t - t_hi.astype(F32)).astype(BF16)
    hw = _dot(t_hi, wr_ref[...])
    logits = hw[:, :LANES] + hw[:, LANES:] + _dot(t_lo, wr_ref[:, :LANES]) + br_ref[...]
    lane = lax.broadcasted_iota(jnp.int32, (tm, LANES), 1).astype(F32)
    ninf = jnp.float32(-jnp.inf)
    big = jnp.float32(LANES)
    gmask = lane < MOE_GROUPS
    gl = jnp.where(gmask, logits, ninf)
    gmax = jnp.max(gl, -1, keepdims=True)
    gsel = jnp.min(jnp.where(gl == gmax, lane, big), -1, keepdims=True)
    gsum = jnp.sum(jnp.where(gmask, jnp.exp(gl - gmax), 0.0), -1, keepdims=True)
    gw = 1.0 / gsum
    lo = MOE_GROUPS + MOE_EXPERTS_PER_GROUP * gsel
    emask = (lane >= lo) & (lane < lo + MOE_EXPERTS_PER_GROUP)
    el = jnp.where(emask, logits, ninf)
    v1 = jnp.max(el, -1, keepdims=True)
    i1 = jnp.min(jnp.where(el == v1, lane, big), -1, keepdims=True)
    el2 = jnp.where(lane == i1, ninf, el)
    v2 = jnp.max(el2, -1, keepdims=True)
    i2 = jnp.min(jnp.where(el2 == v2, lane, big), -1, keepdims=True)
    e = jnp.exp(v2 - v1)
    w1 = gw / (1.0 + e)
    w2 = gw * e / (1.0 + e)
    e1 = i1 - MOE_GROUPS
    e2 = i2 - MOE_GROUPS

    oh1 = lane == e1
    oh2 = lane == e2
    oh = jnp.where(oh1 | oh2, 1.0, 0.0)
    r_i = lax.broadcasted_iota(jnp.int32, (tm, tm), 0)
    c_i = lax.broadcasted_iota(jnp.int32, (tm, tm), 1)
    lower = jnp.where(c_i < r_i, 1.0, 0.0).astype(BF16)
    pref = _dot(lower, oh.astype(BF16))
    r1 = jnp.sum(jnp.where(oh1, pref, 0.0), -1, keepdims=True)
    r2 = jnp.sum(jnp.where(oh2, pref, 0.0), -1, keepdims=True)
    cnt_ref[0] = jnp.sum(oh, 0, keepdims=True)

    wts_ref[...] = jnp.where(lane < 64, w1, w2)
    qm = jnp.where(lane == 0, e1, jnp.where(lane == 1, e2, jnp.where(lane == 2, r1, jnp.where(lane == 3, r2, 0.0))))
    route_ref[...] = qm.T[0:8, :]


def _out_call(da, sg, x2d, g1, sc2, sh2, w_out, lg, lb, wr, br, tiles_per_batch):
    n = x2d.shape[0]
    tm = TOK_TILE
    nt = n // tm
    tpb = tiles_per_batch
    row = lambda i: (i, 0)
    full = lambda i: (0, 0)
    per_b = lambda i: (i // tpb, 0, 0)
    return pl.pallas_call(
        _out_kernel,
        grid=(nt,),
        in_specs=[
            pl.BlockSpec((tm, DA_WIDTH), row),
            pl.BlockSpec((tm, SG_WIDTH), row),
            pl.BlockSpec((tm, D_MODEL), row),
            pl.BlockSpec((1, 1, D_MODEL), per_b),
            pl.BlockSpec((1, 1, D_MODEL), per_b),
            pl.BlockSpec((1, 1, D_MODEL), per_b),
            pl.BlockSpec((D_MODEL, D_MODEL), full),
            pl.BlockSpec((1, D_MODEL), full),
            pl.BlockSpec((1, D_MODEL), full),
            pl.BlockSpec((D_MODEL, 2 * LANES), full),
            pl.BlockSpec((1, LANES), full),
        ],
        out_specs=[
            pl.BlockSpec((tm, D_MODEL), row),
            pl.BlockSpec((tm,) + ROW_TILE, lambda i: (i,) + ROW_ZEROS),
            pl.BlockSpec((8, tm), lambda i: (0, i)),
            pl.BlockSpec((tm, LANES), row),
            pl.BlockSpec((1, 1, LANES), lambda i: (i, 0, 0)),
        ],
        out_shape=[
            jax.ShapeDtypeStruct((n, D_MODEL), F32),
            jax.ShapeDtypeStruct((n,) + ROW_TILE, F32),
            jax.ShapeDtypeStruct((8, n), F32),
            jax.ShapeDtypeStruct((n, LANES), F32),
            jax.ShapeDtypeStruct((nt, 1, LANES), F32),
        ],
        compiler_params=_cparams(("arbitrary",)),
        name="out_proj",
    )(da, sg, x2d, g1, sc2, sh2, w_out, lg, lb, wr, br)


ISSUE_UNROLL = 8


def _dispatch_kernel(zs_ref, ze_ref, dest_ref, t_ref, xb_ref, zero_ref, sem):
    i = pl.program_id(0)
    tm = t_ref.shape[0]

    @pl.when(i == 0)
    def _():
        zero_ref[...] = jnp.zeros_like(zero_ref)

        def per_expert(e, carry):
            def start(r, c):
                pltpu.make_async_copy(_row(zero_ref, 0), _row(xb_ref, r), sem).start()
                return c

            def wait(r, c):
                pltpu.make_async_copy(_row(zero_ref, 0), _row(xb_ref, r), sem).wait()
                return c

            lax.fori_loop(zs_ref[e], ze_ref[e], start, 0)
            lax.fori_loop(zs_ref[e], ze_ref[e], wait, 0)
            return carry

        lax.fori_loop(0, N_EXPERTS, per_expert, 0)

        def blk_copy(j):
            return pltpu.make_async_copy(zero_ref, xb_ref.at[pl.ds(pl.multiple_of(j * MOE_BLOCK, MOE_BLOCK), MOE_BLOCK)], sem)

        def blk_start(j, c):
            blk_copy(j).start()
            return c

        def blk_wait(j, c):
            blk_copy(j).wait()
            return c

        first_unused = ze_ref[N_EXPERTS - 1] // MOE_BLOCK
        n_blocks = xb_ref.shape[0] // MOE_BLOCK
        lax.fori_loop(first_unused, n_blocks, blk_start, 0)
        lax.fori_loop(first_unused, n_blocks, blk_wait, 0)

    def start(n, c):
        pltpu.make_async_copy(_row(t_ref, n), _row(xb_ref, dest_ref[0, 0, n]), sem).start()
        pltpu.make_async_copy(_row(t_ref, n), _row(xb_ref, dest_ref[0, 0, tm + n]), sem).start(priority=1)
        return c

    lax.fori_loop(0, tm, start, 0, unroll=ISSUE_UNROLL)
    for _ in range(2):
        pltpu.make_async_copy(t_ref, xb_ref.at[pl.ds(0, tm)], sem).wait()


def _dispatch_call(zs, ze, dest3, t, p_rows):
    n = t.shape[0]
    tm = TOK_TILE
    grid_spec = pltpu.PrefetchScalarGridSpec(
        num_scalar_prefetch=2,
        grid=(n // tm,),
        in_specs=[
            pl.BlockSpec((1, 1, 2 * tm), lambda i, zs, ze: (i, 0, 0), memory_space=pltpu.SMEM),
            pl.BlockSpec((tm,) + ROW_TILE, lambda i, zs, ze: (i,) + ROW_ZEROS),
        ],
        out_specs=pl.BlockSpec(memory_space=pl.ANY),
        scratch_shapes=[pltpu.VMEM((MOE_BLOCK,) + ROW_TILE, F32), pltpu.SemaphoreType.DMA(())],
    )
    return pl.pallas_call(
        _dispatch_kernel,
        grid_spec=grid_spec,
        out_shape=jax.ShapeDtypeStruct((p_rows,) + ROW_TILE, F32),
        compiler_params=_cparams(("arbitrary",)),
        name="dispatch",
    )(zs, ze, dest3, t)


ITEM_SUB = 4
ITEM_ROWS = ITEM_SUB * MOE_BLOCK


def _expert_kernel(ie_ref, ib_ref, ins_ref, ni_ref, xb_ref, wg_ref, wu_ref, wd_ref, yb_ref,
                   xbuf, ybuf, wgb, wub, wdb, insem, outsem):
    i = pl.program_id(0)
    n_steps = pl.num_programs(0)
    ni = ni_ref[0]
    slot = i % 2

    def rows_of(item, j):
        return pl.ds(pl.multiple_of((ib_ref[item] + j) * MOE_BLOCK, MOE_BLOCK), MOE_BLOCK)

    def in_copy(item, s, j):
        return pltpu.make_async_copy(xb_ref.at[rows_of(item, j)], xbuf.at[s, pl.ds(j * MOE_BLOCK, MOE_BLOCK)], insem.at[s])

    def out_copy(item, s, j):
        return pltpu.make_async_copy(ybuf.at[s, pl.ds(j * MOE_BLOCK, MOE_BLOCK)], yb_ref.at[rows_of(item, j)], outsem.at[s])

    def for_blocks(item, fn):
        for j in range(ITEM_SUB):
            @pl.when(j < ins_ref[item])
            def _():
                fn(j)

    @pl.when(i == 0)
    def _():
        for_blocks(0, lambda j: in_copy(0, 0, j).start())

    @pl.when(i < ni)
    def _():
        @pl.when(i >= 2)
        def _():
            for_blocks(i - 2, lambda j: out_copy(i - 2, slot, j).wait())

        @pl.when(i + 1 < ni)
        def _():
            for_blocks(i + 1, lambda j: in_copy(i + 1, 1 - slot, j).start())

        for_blocks(i, lambda j: in_copy(i, slot, j).wait())

        changed = (i == 0) | (ie_ref[i] != ie_ref[jnp.maximum(i - 1, 0)])

        @pl.when(changed)
        def _():
            wgb[...] = wg_ref[0].astype(BF16)
            wub[...] = wu_ref[0].astype(BF16)
            wdb[...] = wd_ref[0].astype(BF16)

        for ns in range(1, ITEM_SUB + 1):
            @pl.when(ins_ref[i] == ns)
            def _():
                rows = pl.ds(0, ns * MOE_BLOCK)
                x = _load_token_major(xbuf.at[slot, rows]).astype(BF16)
                g = _dot(x, wgb[...])
                u = _dot(x, wub[...])
                hid = (g * (1.0 / (1.0 + jnp.exp(-g))) * u).astype(BF16)
                _store_token_major(ybuf.at[slot, rows], _dot(hid, wdb[...]))

        for_blocks(i, lambda j: out_copy(i, slot, j).start())

    @pl.when(i == n_steps - 1)
    def _():
        for back in (2, 1):
            k = ni - back

            @pl.when(k >= 0)
            def _():
                for_blocks(k, lambda j: out_copy(k, k % 2, j).wait())

        ybuf[0, pl.ds(0, MOE_BLOCK)] = jnp.zeros((MOE_BLOCK,) + ROW_TILE, F32)
        last = jnp.maximum(ni - 1, 0)
        first_unused = ib_ref[last] + ins_ref[last]
        n_blocks = yb_ref.shape[0] // MOE_BLOCK

        def zero_copy(b):
            dst = yb_ref.at[pl.ds(pl.multiple_of(b * MOE_BLOCK, MOE_BLOCK), MOE_BLOCK)]
            return pltpu.make_async_copy(ybuf.at[0, pl.ds(0, MOE_BLOCK)], dst, outsem.at[0])

        def z_start(b, c):
            zero_copy(b).start()
            return c

        def z_wait(b, c):
            zero_copy(b).wait()
            return c

        lax.fori_loop(first_unused, n_blocks, z_start, 0)
        lax.fori_loop(first_unused, n_blocks, z_wait, 0)


def _expert_call(item_e, item_b, item_ns, n_items, xb, wg, wu, wd):
    p_rows = xb.shape[0]
    max_items = item_e.shape[0]

    def wmap(i, ie, ib, ins, ni):
        return (ie[jnp.minimum(i, ni[0] - 1)], 0, 0)

    grid_spec = pltpu.PrefetchScalarGridSpec(
        num_scalar_prefetch=4,
        grid=(max_items,),
        in_specs=[
            pl.BlockSpec(memory_space=pl.ANY),
            pl.BlockSpec((1, D_MODEL, D_MODEL), wmap),
            pl.BlockSpec((1, D_MODEL, D_MODEL), wmap),
            pl.BlockSpec((1, D_MODEL, D_MODEL), wmap),
        ],
        out_specs=pl.BlockSpec(memory_space=pl.ANY),
        scratch_shapes=[
            pltpu.VMEM((2, ITEM_ROWS) + ROW_TILE, F32),
            pltpu.VMEM((2, ITEM_ROWS) + ROW_TILE, F32),
            pltpu.VMEM((D_MODEL, D_MODEL), BF16),
            pltpu.VMEM((D_MODEL, D_MODEL), BF16),
            pltpu.VMEM((D_MODEL, D_MODEL), BF16),
            pltpu.SemaphoreType.DMA((2,)),
            pltpu.SemaphoreType.DMA((2,)),
        ],
    )
    return pl.pallas_call(
        _expert_kernel,
        grid_spec=grid_spec,
        out_shape=jax.ShapeDtypeStruct((p_rows,) + ROW_TILE, F32),
        compiler_params=_cparams(("arbitrary",)),
        name="experts",
    )(item_e, item_b, item_ns, n_items, xb, wg, wu, wd)


def _combine_kernel(dest_ref, yb_ref, x1_ref, wts_ref, g2_ref, lg_ref, lb_ref, o_ref, buf, sem):
    tm = x1_ref.shape[0]

    def start(n, c):
        pltpu.make_async_copy(_row(yb_ref, dest_ref[0, 0, n]), _row(buf.at[0], n), sem).start()
        pltpu.make_async_copy(_row(yb_ref, dest_ref[0, 0, tm + n]), _row(buf.at[1], n), sem).start(priority=1)
        return c

    lax.fori_loop(0, tm, start, 0, unroll=ISSUE_UNROLL)
    for k in range(2):
        pltpu.make_async_copy(yb_ref.at[pl.ds(0, tm)], buf.at[k], sem).wait()
    w = wts_ref[...]
    f = w[:, 0:1] * _load_token_major(buf.at[0]) + w[:, 64:65] * _load_token_major(buf.at[1])
    o_ref[...] = _layer_norm(DEEPNORM_ALPHA * x1_ref[...] + g2_ref[0] * f, lg_ref[...], lb_ref[...])


def _combine_call(dest3, yb, x1, wts, g2, lg, lb, tiles_per_batch):
    n = x1.shape[0]
    tm = TOK_TILE
    tpb = tiles_per_batch
    row = lambda i: (i, 0)
    full = lambda i: (0, 0)
    return pl.pallas_call(
        _combine_kernel,
        grid=(n // tm,),
        in_specs=[
            pl.BlockSpec((1, 1, 2 * tm), lambda i: (i, 0, 0), memory_space=pltpu.SMEM),
            pl.BlockSpec(memory_space=pl.ANY),
            pl.BlockSpec((tm, D_MODEL), row),
            pl.BlockSpec((tm, LANES), row),
            pl.BlockSpec((1, 1, D_MODEL), lambda i: (i // tpb, 0, 0)),
            pl.BlockSpec((1, D_MODEL), full),
            pl.BlockSpec((1, D_MODEL), full),
        ],
        out_specs=pl.BlockSpec((tm, D_MODEL), row),
        out_shape=jax.ShapeDtypeStruct((n, D_MODEL), F32),
        scratch_shapes=[pltpu.VMEM((2, tm) + ROW_TILE, F32), pltpu.SemaphoreType.DMA(())],
        compiler_params=_cparams(("arbitrary",)),
        name="combine",
    )(dest3, yb, x1, wts, g2, lg, lb)


def _head_interleave(w, lo):
    blk = w[:, lo:lo + 2 * QK_COLS].reshape(D_MODEL, 2, DA_HEADS, DA_HEAD_DIM)
    return blk.transpose(0, 2, 1, 3).reshape(D_MODEL, 2 * QK_COLS)


def _rope_tables(seq):
    rows_n = seq // GRID_W
    rows = jnp.repeat(jnp.arange(rows_n, dtype=F32), GRID_W)
    cols = jnp.tile(jnp.arange(GRID_W, dtype=F32), rows_n)
    half = DA_HEAD_DIM // 4
    inv = ROPE_THETA ** (-jnp.arange(half, dtype=F32) / half)
    ang_r = rows[:, None] * inv[None, :]
    ang_c = cols[:, None] * inv[None, :]
    cos64 = jnp.concatenate([jnp.cos(ang_r), jnp.cos(ang_r), jnp.cos(ang_c), jnp.cos(ang_c)], -1)
    sin64 = jnp.concatenate([-jnp.sin(ang_r), jnp.sin(ang_r), -jnp.sin(ang_c), jnp.sin(ang_c)], -1)
    return jnp.tile(cos64, (1, 2)), jnp.tile(sin64, (1, 2))


def kernel(x, c, ctx, c_ctx, w_mod, b_mod, w_in, lam_q1, lam_k1, lam_q2, lam_k2, subln_g, sg_ln_g, sg_ln_b, sg_w, sg_b, w_out, ln1_g, ln1_b, router_group_w, router_group_b, router_expert_w, router_expert_b, exp_w_gate, exp_w_up, exp_w_down, ln2_g, ln2_b):
    b, l, d = x.shape
    n = b * l
    tm = TOK_TILE
    nt = n // tm

    cc = jnp.zeros((b + 8, d), F32).at[:b].set(c).at[b].set(c_ctx)
    mod = _mod_call(cc, w_mod[0], b_mod[0][None, :])
    sh1, sc1, g1, sh2, sc2, g2 = [mod[:b, j * d:(j + 1) * d].reshape(b, 1, d) for j in range(6)]
    csh1 = mod[b:b + 1, 0:d]
    csc1 = mod[b:b + 1, d:2 * d]

    wi = w_in[0]
    w_all = jnp.concatenate([_head_interleave(wi, 0), _head_interleave(wi, KV_LO), wi[:, 2 * KV_LO:]], -1).astype(BF16)
    kc, vc = _ctx_call(ctx, csc1, csh1, w_all[:, KV_LO:KV_HI])

    cos_t, sin_t = _rope_tables(l)
    sbias = jnp.repeat(sg_b[0].T, LANES, axis=1)
    q, k, v, sg = _in_call(x, sc1, sh1, w_all, cos_t, sin_t, sg_ln_g[0][None, :], sg_ln_b[0][None, :],
                           sg_w[0].astype(BF16), sbias)

    lamv = jnp.stack([lam_q1[0], lam_k1[0], lam_q2[0], lam_k2[0]]).astype(F32)
    da = _attn_call(lamv, q, k, kc, v, vc, subln_g[0][None, :])

    wr = jnp.zeros((d, LANES), F32).at[:, :MOE_GROUPS].set(router_group_w[0]).at[:, MOE_GROUPS:MOE_GROUPS + N_EXPERTS].set(router_expert_w[0])
    br = jnp.zeros((1, LANES), F32).at[0, :MOE_GROUPS].set(router_group_b[0]).at[0, MOE_GROUPS:MOE_GROUPS + N_EXPERTS].set(router_expert_b[0])
    wr_hi = wr.astype(BF16)
    wr_split = jnp.concatenate([wr_hi, (wr - wr_hi.astype(F32)).astype(BF16)], -1)
    x1, t, route, wts, tcnt = _out_call(da.reshape(n, DA_WIDTH), sg.reshape(n, SG_WIDTH), x.reshape(n, d), g1, sc2, sh2,
                                        w_out[0].astype(BF16), ln1_g[0][None, :], ln1_b[0][None, :], wr_split, br, l // tm)

    cnt_te = tcnt[:, 0, :N_EXPERTS].astype(jnp.int32)
    counts = jnp.sum(cnt_te, 0)
    padded = (counts + MOE_BLOCK - 1) // MOE_BLOCK * MOE_BLOCK
    pad_end = jnp.cumsum(padded)
    pad_start = pad_end - padded
    base = pad_start[None, :] + jnp.cumsum(cnt_te, 0) - cnt_te
    ridx = route[:4].astype(jnp.int32).reshape(4, nt, tm)
    ex = jnp.arange(N_EXPERTS, dtype=jnp.int32)

    def slot_dest(eid, rank):
        return jnp.sum(jnp.where(eid[None] == ex[:, None, None], base.T[:, :, None], 0), 0) + rank

    dest3 = jnp.concatenate([slot_dest(ridx[0], ridx[2]), slot_dest(ridx[1], ridx[3])], -1).reshape(nt, 1, 2 * tm)
    n_blocks = (n * 2) // MOE_BLOCK + N_EXPERTS
    p_rows = n_blocks * MOE_BLOCK
    nb_e = padded // MOE_BLOCK
    items_e = (nb_e + ITEM_SUB - 1) // ITEM_SUB
    item_end = jnp.cumsum(items_e)
    max_items = (n_blocks + (ITEM_SUB - 1) * N_EXPERTS) // ITEM_SUB
    it = jnp.arange(max_items, dtype=jnp.int32)
    item_e = jnp.minimum(jnp.sum((it[:, None] >= item_end[None, :]).astype(jnp.int32), -1), N_EXPERTS - 1)
    item_j = it - (item_end - items_e)[item_e]
    item_b = (pad_start // MOE_BLOCK)[item_e] + ITEM_SUB * item_j
    item_ns = jnp.clip(nb_e[item_e] - ITEM_SUB * item_j, 0, ITEM_SUB)

    xb = _dispatch_call((pad_start + counts).astype(jnp.int32), pad_end.astype(jnp.int32), dest3, t, p_rows)
    yb = _expert_call(item_e, item_b.astype(jnp.int32), item_ns.astype(jnp.int32), item_end[-1:].astype(jnp.int32),
                      xb, exp_w_gate[0], exp_w_up[0], exp_w_down[0])
    out = _combine_call(dest3, yb, x1, wts, g2, ln2_g[0][None, :], ln2_b[0][None, :], l // tm)
    return out.reshape(b, l, d)
```

```python
import math

import jax
import jax.numpy as jnp
from jax import lax
from jax.experimental import pallas as pl
from jax.experimental.pallas import tpu as pltpu

F32 = jnp.float32
BF16 = jnp.bfloat16

D_MODEL = 1024
GRID_W = 64
DA_HEAD_DIM = 64
DA_V_DIM = 128
DA_WIDTH = 512
DA_HEADS = 4
QK_COLS = 256
SG_CHUNK = 128
SG_WIDTH = 512
SG_GROUPS = 4
KV_LO = 512
KV_HI = 1536
IN_COLS = 2560
ROPE_THETA = 10000.0
MOE_GROUPS = 4
MOE_EXPERTS_PER_GROUP = 8
N_EXPERTS = 32
MOE_BLOCK = 128
EPS = 1e-5
DEPTH = 1
DEEPNORM_ALPHA = (2.0 * DEPTH) ** 0.25
LAM_INIT = 0.8 - 0.6 * math.exp(-0.3 * 0)
Q_SCALE = DA_HEAD_DIM ** -0.5
LOG2E = math.log2(math.e)
SQRT_HALF = math.sqrt(0.5)

LANES = 128
TOK_TILE = 512
ATTN_TQ = 2048
ATTN_SUB = 256
VMEM_LIMIT = 56 * 1024 * 1024


def _cparams(sem):
    return pltpu.CompilerParams(dimension_semantics=sem, vmem_limit_bytes=VMEM_LIMIT)


def _dot(a, b):
    return jnp.dot(a, b, preferred_element_type=F32)


def _dot_hi(a, b):
    return jnp.dot(a, b, preferred_element_type=F32, precision=lax.Precision.HIGHEST)


def _layer_norm(y, g, b):
    mu = jnp.mean(y, -1, keepdims=True)
    yc = y - mu
    var = jnp.mean(yc * yc, -1, keepdims=True)
    return yc * lax.rsqrt(var + EPS) * g + b


def _row(ref, n):
    return ref.at[pl.ds(n, 1)]


def _mod_kernel(c_ref, w_ref, b_ref, o_ref):
    c = c_ref[...]
    s = c * (1.0 / (1.0 + jnp.exp(-c)))
    o_ref[...] = _dot_hi(s, w_ref[...]) + b_ref[...]


def _mod_call(cc, w_mod, b_mod):
    rows = cc.shape[0]
    n_out = w_mod.shape[1]
    bn = 1024
    return pl.pallas_call(
        _mod_kernel,
        grid=(n_out // bn,),
        in_specs=[
            pl.BlockSpec((rows, D_MODEL), lambda j: (0, 0)),
            pl.BlockSpec((D_MODEL, bn), lambda j: (0, j)),
            pl.BlockSpec((1, bn), lambda j: (0, j)),
        ],
        out_specs=pl.BlockSpec((rows, bn), lambda j: (0, j)),
        out_shape=jax.ShapeDtypeStruct((rows, n_out), F32),
        compiler_params=_cparams(("arbitrary",)),
        name="mod",
    )(cc, w_mod, b_mod)


def _ctx_kernel(c_ref, sc_ref, sh_ref, w_ref, kc_ref, vc_ref):
    h = (c_ref[0] * (1.0 + sc_ref[...]) + sh_ref[...]).astype(BF16)
    p = _dot(h, w_ref[...])
    kc_ref[0] = p[:, :2 * QK_COLS].astype(BF16)
    vc_ref[0] = p[:, 2 * QK_COLS:].astype(BF16)


def _ctx_call(ctx, csc1, csh1, w_kv):
    b, cl, _ = ctx.shape
    return pl.pallas_call(
        _ctx_kernel,
        grid=(b,),
        in_specs=[
            pl.BlockSpec((1, cl, D_MODEL), lambda i: (i, 0, 0)),
            pl.BlockSpec((1, D_MODEL), lambda i: (0, 0)),
            pl.BlockSpec((1, D_MODEL), lambda i: (0, 0)),
            pl.BlockSpec((D_MODEL, 2 * QK_COLS + DA_WIDTH), lambda i: (0, 0)),
        ],
        out_specs=[
            pl.BlockSpec((1, cl, 2 * QK_COLS), lambda i: (i, 0, 0)),
            pl.BlockSpec((1, cl, DA_WIDTH), lambda i: (i, 0, 0)),
        ],
        out_shape=[
            jax.ShapeDtypeStruct((b, cl, 2 * QK_COLS), BF16),
            jax.ShapeDtypeStruct((b, cl, DA_WIDTH), BF16),
        ],
        compiler_params=_cparams(("arbitrary",)),
        name="ctx_kv",
    )(ctx, csc1, csh1, w_kv)


def _in_kernel(x_ref, sc_ref, sh_ref, w_ref, cos_ref, sin_ref, lng_ref, lnb_ref,
               sw_ref, sb_ref, q_ref, k_ref, v_ref, sg_ref):
    tm = x_ref.shape[1]
    h = (x_ref[0] * (1.0 + sc_ref[0]) + sh_ref[0]).astype(BF16)
    p = _dot(h, w_ref[...])

    cos = cos_ref[...]
    sin = sin_ref[...]
    lane = lax.broadcasted_iota(jnp.int32, (tm, LANES), 1)
    first = (lane % 32) < 16

    def rope(t):
        partner = jnp.where(first, pltpu.roll(t, LANES - 16, 1), pltpu.roll(t, 16, 1))
        return t * cos + partner * sin

    for c in range(4):
        cs = slice(c * LANES, (c + 1) * LANES)
        q_ref[0, :, cs] = rope(p[:, cs] * (Q_SCALE * LOG2E)).astype(BF16)
        k_ref[0, :, cs] = rope(p[:, KV_LO + c * LANES:KV_LO + (c + 1) * LANES]).astype(BF16)
    v_ref[0] = p[:, 2 * KV_LO:KV_HI].astype(BF16)

    z = p[:, KV_HI:]
    gz = 0.5 * z * (1.0 + lax.erf(z * SQRT_HALF))
    u = gz[:, :SG_WIDTH]
    vn = _layer_norm(gz[:, SG_WIDTH:], lng_ref[...], lnb_ref[...]).astype(BF16)
    for c in range(tm // SG_CHUNK):
        rs = slice(c * SG_CHUNK, (c + 1) * SG_CHUNK)
        for g in range(SG_GROUPS):
            cs = slice(g * LANES, (g + 1) * LANES)
            s = _dot(sw_ref[g], vn[rs, cs]) + sb_ref[:, cs]
            sg_ref[0, rs, cs] = (u[rs, cs] * s).astype(BF16)


def _in_call(x, sc1, sh1, w_in, cos_t, sin_t, lng, lnb, sw, sbias):
    b, l, _ = x.shape
    tm = TOK_TILE
    full = lambda bi, i: (0, 0)
    return pl.pallas_call(
        _in_kernel,
        grid=(b, l // tm),
        in_specs=[
            pl.BlockSpec((1, tm, D_MODEL), lambda bi, i: (bi, i, 0)),
            pl.BlockSpec((1, 1, D_MODEL), lambda bi, i: (bi, 0, 0)),
            pl.BlockSpec((1, 1, D_MODEL), lambda bi, i: (bi, 0, 0)),
            pl.BlockSpec((D_MODEL, IN_COLS), full),
            pl.BlockSpec((tm, LANES), lambda bi, i: (i, 0)),
            pl.BlockSpec((tm, LANES), lambda bi, i: (i, 0)),
            pl.BlockSpec((1, SG_WIDTH), full),
            pl.BlockSpec((1, SG_WIDTH), full),
            pl.BlockSpec((SG_GROUPS, SG_CHUNK, SG_CHUNK), lambda bi, i: (0, 0, 0)),
            pl.BlockSpec((SG_CHUNK, SG_WIDTH), full),
        ],
        out_specs=[pl.BlockSpec((1, tm, 512), lambda bi, i: (bi, i, 0))] * 4,
        out_shape=[jax.ShapeDtypeStruct((b, l, 512), BF16)] * 4,
        compiler_params=_cparams(("arbitrary", "arbitrary")),
        name="in_proj",
    )(x, sc1, sh1, w_in, cos_t, sin_t, lng, lnb, sw, sbias)


def _attn_kernel(lam_ref, q_ref, k_ref, kc_ref, v_ref, vc_ref, g_ref, o_ref, kall, vall):
    lv = lam_ref[...]
    lam = (jnp.exp(jnp.sum(lv[0:1] * lv[1:2], -1, keepdims=True))
           - jnp.exp(jnp.sum(lv[2:3] * lv[3:4], -1, keepdims=True)) + LAM_INIT)
    l = k_ref.shape[1]

    @pl.when(pl.program_id(2) == 0)
    def _():
        kall[:l, :] = k_ref[0]
        kall[l:, :] = kc_ref[0]
        vall[:l, :DA_V_DIM] = v_ref[0]
        vall[l:, :DA_V_DIM] = vc_ref[0]
        vall[:, DA_V_DIM:] = jnp.ones((vall.shape[0], DA_V_DIM), BF16)

    nt = (((1,), (1,)), ((), ()))

    def branch(qm):
        s = lax.dot_general(qm, kall[...], nt, preferred_element_type=F32)
        m = jnp.max(s, -1, keepdims=True)
        e = jnp.exp2((s - m).astype(BF16))
        oe = _dot(e, vall[...])
        return oe[:, :DA_V_DIM] / oe[:, DA_V_DIM:DA_V_DIM + 1]

    for r in range(q_ref.shape[1] // ATTN_SUB):
        rs = pl.ds(r * ATTN_SUB, ATTN_SUB)
        q = q_ref[0, rs, :]
        lane = lax.broadcasted_iota(jnp.int32, q.shape, 1)
        zero = jnp.zeros_like(q)
        o = branch(jnp.where(lane < DA_HEAD_DIM, q, zero)) - lam * branch(jnp.where(lane >= DA_HEAD_DIM, q, zero))
        of = o * lax.rsqrt(jnp.mean(o * o, -1, keepdims=True) + EPS) * g_ref[...]
        o_ref[0, rs, :] = (of * (1.0 - LAM_INIT)).astype(BF16)


def _attn_call(lamv, q, k, kc, v, vc, subln_g):
    b, l, _ = q.shape
    cl = kc.shape[1]
    tq = ATTN_TQ
    return pl.pallas_call(
        _attn_kernel,
        grid=(b, DA_HEADS, l // tq),
        in_specs=[
            pl.BlockSpec((4, DA_HEAD_DIM), lambda bi, h, i: (0, 0)),
            pl.BlockSpec((1, tq, LANES), lambda bi, h, i: (bi, i, h)),
            pl.BlockSpec((1, l, LANES), lambda bi, h, i: (bi, 0, h)),
            pl.BlockSpec((1, cl, LANES), lambda bi, h, i: (bi, 0, h)),
            pl.BlockSpec((1, l, LANES), lambda bi, h, i: (bi, 0, h)),
            pl.BlockSpec((1, cl, LANES), lambda bi, h, i: (bi, 0, h)),
            pl.BlockSpec((1, DA_V_DIM), lambda bi, h, i: (0, 0)),
        ],
        out_specs=pl.BlockSpec((1, tq, LANES), lambda bi, h, i: (bi, i, h)),
        out_shape=jax.ShapeDtypeStruct((b, l, DA_WIDTH), BF16),
        scratch_shapes=[pltpu.VMEM((l + cl, LANES), BF16), pltpu.VMEM((l + cl, 2 * DA_V_DIM), BF16)],
        compiler_params=_cparams(("arbitrary", "arbitrary", "arbitrary")),
        name="attn",
    )(lamv, q, k, kc, v, vc, subln_g)


def _out_kernel(da_ref, sg_ref, x_ref, g1_ref, sc_ref, sh_ref, w_ref, lg_ref, lb_ref,
                wr_ref, br_ref, x1_ref, t_ref, route_ref, wts_ref, cnt_ref):
    tm = x_ref.shape[0]
    y = _dot(da_ref[...], w_ref[:DA_WIDTH, :]) + _dot(sg_ref[...], w_ref[DA_WIDTH:, :])
    x1 = _layer_norm(DEEPNORM_ALPHA * x_ref[...] + g1_ref[0] * y, lg_ref[...], lb_ref[...])
    x1_ref[...] = x1
    t = x1 * (1.0 + sc_ref[0]) + sh_ref[0]
    t_ref[...] = t

    t_hi = t.astype(BF16)
    t_lo = (t - t_hi.astype(F32)).astype(BF16)
    hw = _dot(t_hi, wr_ref[...])
    logits = hw[:, :LANES] + hw[:, LANES:] + _dot(t_lo, wr_ref[:, :LANES]) + br_ref[...]
    lane = lax.broadcasted_iota(jnp.int32, (tm, LANES), 1).astype(F32)
    ninf = jnp.float32(-jnp.inf)
    big = jnp.float32(LANES)
    gmask = lane < MOE_GROUPS
    gl = jnp.where(gmask, logits, ninf)
    gmax = jnp.max(gl, -1, keepdims=True)
    gsel = jnp.min(jnp.where(gl == gmax, lane, big), -1, keepdims=True)
    gsum = jnp.sum(jnp.where(gmask, jnp.exp(gl - gmax), 0.0), -1, keepdims=True)
    gw = 1.0 / gsum
    lo = MOE_GROUPS + MOE_EXPERTS_PER_GROUP * gsel
    emask = (lane >= lo) & (lane < lo + MOE_EXPERTS_PER_GROUP)
    el = jnp.where(emask, logits, ninf)
    v1 = jnp.max(el, -1, keepdims=True)
    i1 = jnp.min(jnp.where(el == v1, lane, big), -1, keepdims=True)
    el2 = jnp.where(lane == i1, ninf, el)
    v2 = jnp.max(el2, -1, keepdims=True)
    i2 = jnp.min(jnp.where(el2 == v2, lane, big), -1, keepdims=True)
    e = jnp.exp(v2 - v1)
    w1 = gw / (1.0 + e)
    w2 = gw * e / (1.0 + e)
    e1 = i1 - MOE_GROUPS
    e2 = i2 - MOE_GROUPS

    oh1 = lane == e1
    oh2 = lane == e2
    oh = jnp.where(oh1 | oh2, 1.0, 0.0)
    r_i = lax.broadcasted_iota(jnp.int32, (tm, tm), 0)
    c_i = lax.broadcasted_iota(jnp.int32, (tm, tm), 1)
    lower = jnp.where(c_i < r_i, 1.0, 0.0).astype(BF16)
    pref = _dot(lower, oh.astype(BF16))
    r1 = jnp.sum(jnp.where(oh1, pref, 0.0), -1, keepdims=True)
    r2 = jnp.sum(jnp.where(oh2, pref, 0.0), -1, keepdims=True)
    cnt_ref[0] = jnp.sum(oh, 0, keepdims=True)

    wts_ref[...] = jnp.where(lane < 64, w1, w2)
    qm = jnp.where(lane == 0, e1, jnp.where(lane == 1, e2, jnp.where(lane == 2, r1, jnp.where(lane == 3, r2, 0.0))))
    route_ref[...] = qm.T[0:8, :]


def _out_call(da, sg, x2d, g1, sc2, sh2, w_out, lg, lb, wr, br, tiles_per_batch):
    n = x2d.shape[0]
    tm = TOK_TILE
    nt = n // tm
    tpb = tiles_per_batch
    row = lambda i: (i, 0)
    full = lambda i: (0, 0)
    per_b = lambda i: (i // tpb, 0, 0)
    return pl.pallas_call(
        _out_kernel,
        grid=(nt,),
        in_specs=[
            pl.BlockSpec((tm, DA_WIDTH), row),
            pl.BlockSpec((tm, SG_WIDTH), row),
            pl.BlockSpec((tm, D_MODEL), row),
            pl.BlockSpec((1, 1, D_MODEL), per_b),
            pl.BlockSpec((1, 1, D_MODEL), per_b),
            pl.BlockSpec((1, 1, D_MODEL), per_b),
            pl.BlockSpec((D_MODEL, D_MODEL), full),
            pl.BlockSpec((1, D_MODEL), full),
            pl.BlockSpec((1, D_MODEL), full),
            pl.BlockSpec((D_MODEL, 2 * LANES), full),
            pl.BlockSpec((1, LANES), full),
        ],
        out_specs=[
            pl.BlockSpec((tm, D_MODEL), row),
            pl.BlockSpec((tm, D_MODEL), row),
            pl.BlockSpec((8, tm), lambda i: (0, i)),
            pl.BlockSpec((tm, LANES), row),
            pl.BlockSpec((1, 1, LANES), lambda i: (i, 0, 0)),
        ],
        out_shape=[
            jax.ShapeDtypeStruct((n, D_MODEL), F32),
            jax.ShapeDtypeStruct((n, D_MODEL), F32),
            jax.ShapeDtypeStruct((8, n), F32),
            jax.ShapeDtypeStruct((n, LANES), F32),
            jax.ShapeDtypeStruct((nt, 1, LANES), F32),
        ],
        compiler_params=_cparams(("arbitrary",)),
        name="out_proj",
    )(da, sg, x2d, g1, sc2, sh2, w_out, lg, lb, wr, br)


ISSUE_UNROLL = 8


def _dispatch_kernel(zs_ref, ze_ref, dest_ref, t_ref, xb_ref, zero_ref, sem):
    i = pl.program_id(0)
    tm = t_ref.shape[0]

    @pl.when(i == 0)
    def _():
        zero_ref[...] = jnp.zeros_like(zero_ref)

        def per_expert(e, carry):
            def start(r, c):
                pltpu.make_async_copy(_row(zero_ref, 0), _row(xb_ref, r), sem).start()
                return c

            def wait(r, c):
                pltpu.make_async_copy(_row(zero_ref, 0), _row(xb_ref, r), sem).wait()
                return c

            lax.fori_loop(zs_ref[e], ze_ref[e], start, 0)
            lax.fori_loop(zs_ref[e], ze_ref[e], wait, 0)
            return carry

        lax.fori_loop(0, N_EXPERTS, per_expert, 0)

        def blk_copy(j):
            return pltpu.make_async_copy(zero_ref, xb_ref.at[pl.ds(pl.multiple_of(j * MOE_BLOCK, MOE_BLOCK), MOE_BLOCK)], sem)

        def blk_start(j, c):
            blk_copy(j).start()
            return c

        def blk_wait(j, c):
            blk_copy(j).wait()
            return c

        first_unused = ze_ref[N_EXPERTS - 1] // MOE_BLOCK
        n_blocks = xb_ref.shape[0] // MOE_BLOCK
        lax.fori_loop(first_unused, n_blocks, blk_start, 0)
        lax.fori_loop(first_unused, n_blocks, blk_wait, 0)

    def start(n, c):
        pltpu.make_async_copy(_row(t_ref, n), _row(xb_ref, dest_ref[0, 0, n]), sem).start()
        pltpu.make_async_copy(_row(t_ref, n), _row(xb_ref, dest_ref[0, 0, tm + n]), sem).start(priority=1)
        return c

    lax.fori_loop(0, tm, start, 0, unroll=ISSUE_UNROLL)
    for _ in range(2):
        pltpu.make_async_copy(t_ref, xb_ref.at[pl.ds(0, tm)], sem).wait()


def _dispatch_call(zs, ze, dest3, t, p_rows):
    n = t.shape[0]
    tm = TOK_TILE
    grid_spec = pltpu.PrefetchScalarGridSpec(
        num_scalar_prefetch=2,
        grid=(n // tm,),
        in_specs=[
            pl.BlockSpec((1, 1, 2 * tm), lambda i, zs, ze: (i, 0, 0), memory_space=pltpu.SMEM),
            pl.BlockSpec((tm, D_MODEL), lambda i, zs, ze: (i, 0)),
        ],
        out_specs=pl.BlockSpec(memory_space=pl.ANY),
        scratch_shapes=[pltpu.VMEM((MOE_BLOCK, D_MODEL), F32), pltpu.SemaphoreType.DMA(())],
    )
    return pl.pallas_call(
        _dispatch_kernel,
        grid_spec=grid_spec,
        out_shape=jax.ShapeDtypeStruct((p_rows, D_MODEL), F32),
        compiler_params=_cparams(("arbitrary",)),
        name="dispatch",
    )(zs, ze, dest3, t)


ITEM_SUB = 4
ITEM_ROWS = ITEM_SUB * MOE_BLOCK


def _expert_kernel(ie_ref, ib_ref, ins_ref, ni_ref, xb_ref, wg_ref, wu_ref, wd_ref, yb_ref,
                   xbuf, ybuf, wgb, wub, wdb, insem, outsem):
    i = pl.program_id(0)
    n_steps = pl.num_programs(0)
    ni = ni_ref[0]
    slot = i % 2

    def rows_of(item, j):
        return pl.ds(pl.multiple_of((ib_ref[item] + j) * MOE_BLOCK, MOE_BLOCK), MOE_BLOCK)

    def in_copy(item, s, j):
        return pltpu.make_async_copy(xb_ref.at[rows_of(item, j)], xbuf.at[s, pl.ds(j * MOE_BLOCK, MOE_BLOCK)], insem.at[s])

    def out_copy(item, s, j):
        return pltpu.make_async_copy(ybuf.at[s, pl.ds(j * MOE_BLOCK, MOE_BLOCK)], yb_ref.at[rows_of(item, j)], outsem.at[s])

    def for_blocks(item, fn):
        for j in range(ITEM_SUB):
            @pl.when(j < ins_ref[item])
            def _():
                fn(j)

    @pl.when(i == 0)
    def _():
        for_blocks(0, lambda j: in_copy(0, 0, j).start())

    @pl.when(i < ni)
    def _():
        @pl.when(i >= 2)
        def _():
            for_blocks(i - 2, lambda j: out_copy(i - 2, slot, j).wait())

        @pl.when(i + 1 < ni)
        def _():
            for_blocks(i + 1, lambda j: in_copy(i + 1, 1 - slot, j).start())

        for_blocks(i, lambda j: in_copy(i, slot, j).wait())

        changed = (i == 0) | (ie_ref[i] != ie_ref[jnp.maximum(i - 1, 0)])

        @pl.when(changed)
        def _():
            wgb[...] = wg_ref[0].astype(BF16)
            wub[...] = wu_ref[0].astype(BF16)
            wdb[...] = wd_ref[0].astype(BF16)

        for ns in range(1, ITEM_SUB + 1):
            @pl.when(ins_ref[i] == ns)
            def _():
                rows = pl.ds(0, ns * MOE_BLOCK)
                x = xbuf[slot, rows].astype(BF16)
                g = _dot(x, wgb[...])
                u = _dot(x, wub[...])
                hid = (g * (1.0 / (1.0 + jnp.exp(-g))) * u).astype(BF16)
                ybuf[slot, rows] = _dot(hid, wdb[...])

        for_blocks(i, lambda j: out_copy(i, slot, j).start())

    @pl.when(i == n_steps - 1)
    def _():
        for back in (2, 1):
            k = ni - back

            @pl.when(k >= 0)
            def _():
                for_blocks(k, lambda j: out_copy(k, k % 2, j).wait())

        ybuf[0, pl.ds(0, MOE_BLOCK)] = jnp.zeros((MOE_BLOCK, D_MODEL), F32)
        last = jnp.maximum(ni - 1, 0)
        first_unused = ib_ref[last] + ins_ref[last]
        n_blocks = yb_ref.shape[0] // MOE_BLOCK

        def zero_copy(b):
            dst = yb_ref.at[pl.ds(pl.multiple_of(b * MOE_BLOCK, MOE_BLOCK), MOE_BLOCK)]
            return pltpu.make_async_copy(ybuf.at[0, pl.ds(0, MOE_BLOCK)], dst, outsem.at[0])

        def z_start(b, c):
            zero_copy(b).start()
            return c

        def z_wait(b, c):
            zero_copy(b).wait()
            return c

        lax.fori_loop(first_unused, n_blocks, z_start, 0)
        lax.fori_loop(first_unused, n_blocks, z_wait, 0)


def _expert_call(item_e, item_b, item_ns, n_items, xb, wg, wu, wd):
    p_rows = xb.shape[0]
    max_items = item_e.shape[0]

    def wmap(i, ie, ib, ins, ni):
        return (ie[jnp.minimum(i, ni[0] - 1)], 0, 0)

    grid_spec = pltpu.PrefetchScalarGridSpec(
        num_scalar_prefetch=4,
        grid=(max_items,),
        in_specs=[
            pl.BlockSpec(memory_space=pl.ANY),
            pl.BlockSpec((1, D_MODEL, D_MODEL), wmap),
            pl.BlockSpec((1, D_MODEL, D_MODEL), wmap),
            pl.BlockSpec((1, D_MODEL, D_MODEL), wmap),
        ],
        out_specs=pl.BlockSpec(memory_space=pl.ANY),
        scratch_shapes=[
            pltpu.VMEM((2, ITEM_ROWS, D_MODEL), F32),
            pltpu.VMEM((2, ITEM_ROWS, D_MODEL), F32),
            pltpu.VMEM((D_MODEL, D_MODEL), BF16),
            pltpu.VMEM((D_MODEL, D_MODEL), BF16),
            pltpu.VMEM((D_MODEL, D_MODEL), BF16),
            pltpu.SemaphoreType.DMA((2,)),
            pltpu.SemaphoreType.DMA((2,)),
        ],
    )
    return pl.pallas_call(
        _expert_kernel,
        grid_spec=grid_spec,
        out_shape=jax.ShapeDtypeStruct((p_rows, D_MODEL), F32),
        compiler_params=_cparams(("arbitrary",)),
        name="experts",
    )(item_e, item_b, item_ns, n_items, xb, wg, wu, wd)


def _combine_kernel(dcur_ref, dnxt_ref, yb_ref, x1_ref, wts_ref, g2_ref, lg_ref, lb_ref, o_ref, buf, sem):
    i = pl.program_id(0)
    tm = x1_ref.shape[0]
    slot = i % 2

    def issue(dref, s):
        def start(n, c):
            pltpu.make_async_copy(_row(yb_ref, dref[0, 0, n]), _row(buf.at[s, 0], n), sem.at[s]).start()
            pltpu.make_async_copy(_row(yb_ref, dref[0, 0, tm + n]), _row(buf.at[s, 1], n), sem.at[s]).start(priority=1)
            return c

        lax.fori_loop(0, tm, start, 0, unroll=ISSUE_UNROLL)

    @pl.when(i == 0)
    def _():
        issue(dcur_ref, 0)

    @pl.when(i + 1 < pl.num_programs(0))
    def _():
        issue(dnxt_ref, 1 - slot)

    for k in range(2):
        pltpu.make_async_copy(yb_ref.at[pl.ds(0, tm)], buf.at[slot, k], sem.at[slot]).wait()
    w = wts_ref[...]
    f = w[:, 0:1] * buf[slot, 0] + w[:, 64:65] * buf[slot, 1]
    o_ref[...] = _layer_norm(DEEPNORM_ALPHA * x1_ref[...] + g2_ref[0] * f, lg_ref[...], lb_ref[...])


def _combine_call(dest3, yb, x1, wts, g2, lg, lb, tiles_per_batch):
    n = x1.shape[0]
    tm = TOK_TILE
    tpb = tiles_per_batch
    row = lambda i: (i, 0)
    full = lambda i: (0, 0)
    nt = n // tm
    return pl.pallas_call(
        _combine_kernel,
        grid=(nt,),
        in_specs=[
            pl.BlockSpec((1, 1, 2 * tm), lambda i: (i, 0, 0), memory_space=pltpu.SMEM),
            pl.BlockSpec((1, 1, 2 * tm), lambda i: (jnp.minimum(i + 1, nt - 1), 0, 0), memory_space=pltpu.SMEM),
            pl.BlockSpec(memory_space=pl.ANY),
            pl.BlockSpec((tm, D_MODEL), row),
            pl.BlockSpec((tm, LANES), row),
            pl.BlockSpec((1, 1, D_MODEL), lambda i: (i // tpb, 0, 0)),
            pl.BlockSpec((1, D_MODEL), full),
            pl.BlockSpec((1, D_MODEL), full),
        ],
        out_specs=pl.BlockSpec((tm, D_MODEL), row),
        out_shape=jax.ShapeDtypeStruct((n, D_MODEL), F32),
        scratch_shapes=[pltpu.VMEM((2, 2, tm, D_MODEL), F32), pltpu.SemaphoreType.DMA((2,))],
        compiler_params=_cparams(("arbitrary",)),
        name="combine",
    )(dest3, dest3, yb, x1, wts, g2, lg, lb)


def _head_interleave(w, lo):
    blk = w[:, lo:lo + 2 * QK_COLS].reshape(D_MODEL, 2, DA_HEADS, DA_HEAD_DIM)
    return blk.transpose(0, 2, 1, 3).reshape(D_MODEL, 2 * QK_COLS)


def _cumsum_small(x):
    idx = jnp.arange(x.shape[0], dtype=jnp.int32)
    return jnp.sum(jnp.where(idx[None, :] <= idx[:, None], x[None, :], 0), -1)


def _lookup(table, idx):
    return jnp.sum(jnp.where(idx[:, None] == jnp.arange(table.shape[0], dtype=jnp.int32)[None, :], table[None, :], 0), -1)


def _rope_tables(seq):
    rows_n = seq // GRID_W
    rows = jnp.repeat(jnp.arange(rows_n, dtype=F32), GRID_W)
    cols = jnp.tile(jnp.arange(GRID_W, dtype=F32), rows_n)
    half = DA_HEAD_DIM // 4
    inv = ROPE_THETA ** (-jnp.arange(half, dtype=F32) / half)
    ang_r = rows[:, None] * inv[None, :]
    ang_c = cols[:, None] * inv[None, :]
    cos64 = jnp.concatenate([jnp.cos(ang_r), jnp.cos(ang_r), jnp.cos(ang_c), jnp.cos(ang_c)], -1)
    sin64 = jnp.concatenate([-jnp.sin(ang_r), jnp.sin(ang_r), -jnp.sin(ang_c), jnp.sin(ang_c)], -1)
    return jnp.tile(cos64, (1, 2)), jnp.tile(sin64, (1, 2))


def kernel(x, c, ctx, c_ctx, w_mod, b_mod, w_in, lam_q1, lam_k1, lam_q2, lam_k2, subln_g, sg_ln_g, sg_ln_b, sg_w, sg_b, w_out, ln1_g, ln1_b, router_group_w, router_group_b, router_expert_w, router_expert_b, exp_w_gate, exp_w_up, exp_w_down, ln2_g, ln2_b):
    b, l, d = x.shape
    n = b * l
    tm = TOK_TILE
    nt = n // tm

    cc = jnp.zeros((b + 8, d), F32).at[:b].set(c).at[b].set(c_ctx)
    mod = _mod_call(cc, w_mod[0], b_mod[0][None, :])
    sh1, sc1, g1, sh2, sc2, g2 = [mod[:b, j * d:(j + 1) * d].reshape(b, 1, d) for j in range(6)]
    csh1 = mod[b:b + 1, 0:d]
    csc1 = mod[b:b + 1, d:2 * d]

    wi = w_in[0]
    w_all = jnp.concatenate([_head_interleave(wi, 0), _head_interleave(wi, KV_LO), wi[:, 2 * KV_LO:]], -1).astype(BF16)
    kc, vc = _ctx_call(ctx, csc1, csh1, w_all[:, KV_LO:KV_HI])

    cos_t, sin_t = _rope_tables(l)
    sbias = jnp.repeat(sg_b[0].T, LANES, axis=1)
    q, k, v, sg = _in_call(x, sc1, sh1, w_all, cos_t, sin_t, sg_ln_g[0][None, :], sg_ln_b[0][None, :],
                           sg_w[0].astype(BF16), sbias)

    lamv = jnp.stack([lam_q1[0], lam_k1[0], lam_q2[0], lam_k2[0]]).astype(F32)
    da = _attn_call(lamv, q, k, kc, v, vc, subln_g[0][None, :])

    wr = jnp.zeros((d, LANES), F32).at[:, :MOE_GROUPS].set(router_group_w[0]).at[:, MOE_GROUPS:MOE_GROUPS + N_EXPERTS].set(router_expert_w[0])
    br = jnp.zeros((1, LANES), F32).at[0, :MOE_GROUPS].set(router_group_b[0]).at[0, MOE_GROUPS:MOE_GROUPS + N_EXPERTS].set(router_expert_b[0])
    wr_hi = wr.astype(BF16)
    wr_split = jnp.concatenate([wr_hi, (wr - wr_hi.astype(F32)).astype(BF16)], -1)
    x1, t, route, wts, tcnt = _out_call(da.reshape(n, DA_WIDTH), sg.reshape(n, SG_WIDTH), x.reshape(n, d), g1, sc2, sh2,
                                        w_out[0].astype(BF16), ln1_g[0][None, :], ln1_b[0][None, :], wr_split, br, l // tm)

    cnt_te = tcnt[:, 0, :N_EXPERTS].astype(jnp.int32)
    counts = jnp.sum(cnt_te, 0)
    padded = (counts + MOE_BLOCK - 1) // MOE_BLOCK * MOE_BLOCK
    pad_end = _cumsum_small(padded)
    pad_start = pad_end - padded
    tix = jnp.arange(nt, dtype=jnp.int32)
    tile_prefix = jnp.sum(jnp.where(tix[None, :, None] < tix[:, None, None], cnt_te[None], 0), 1)
    base = pad_start[None, :] + tile_prefix
    ridx = route[:4].astype(jnp.int32).reshape(4, nt, tm)
    ex = jnp.arange(N_EXPERTS, dtype=jnp.int32)

    def slot_dest(eid, rank):
        return jnp.sum(jnp.where(eid[None] == ex[:, None, None], base.T[:, :, None], 0), 0) + rank

    dest3 = jnp.concatenate([slot_dest(ridx[0], ridx[2]), slot_dest(ridx[1], ridx[3])], -1).reshape(nt, 1, 2 * tm)
    n_blocks = (n * 2) // MOE_BLOCK + N_EXPERTS
    p_rows = n_blocks * MOE_BLOCK
    nb_e = padded // MOE_BLOCK
    items_e = (nb_e + ITEM_SUB - 1) // ITEM_SUB
    item_end = _cumsum_small(items_e)
    max_items = (n_blocks + (ITEM_SUB - 1) * N_EXPERTS) // ITEM_SUB
    it = jnp.arange(max_items, dtype=jnp.int32)
    item_e = jnp.minimum(jnp.sum((it[:, None] >= item_end[None, :]).astype(jnp.int32), -1), N_EXPERTS - 1)
    item_j = it - _lookup(item_end - items_e, item_e)
    item_b = _lookup(pad_start // MOE_BLOCK, item_e) + ITEM_SUB * item_j
    item_ns = jnp.clip(_lookup(nb_e, item_e) - ITEM_SUB * item_j, 0, ITEM_SUB)

    xb = _dispatch_call((pad_start + counts).astype(jnp.int32), pad_end.astype(jnp.int32), dest3, t, p_rows)
    yb = _expert_call(item_e, item_b.astype(jnp.int32), item_ns.astype(jnp.int32), item_end[-1:].astype(jnp.int32),
                      xb, exp_w_gate[0], exp_w_up[0], exp_w_down[0])
    out = _combine_call(dest3, yb, x1, wts, g2, ln2_g[0][None, :], ln2_b[0][None, :], l // tm)
    return out.reshape(b, l, d)
```

```python
import math

import jax
import jax.numpy as jnp
from jax import lax
from jax.experimental import pallas as pl
from jax.experimental.pallas import tpu as pltpu

F32 = jnp.float32
BF16 = jnp.bfloat16

D_MODEL = 1024
GRID_W = 64
DA_HEAD_DIM = 64
DA_V_DIM = 128
DA_WIDTH = 512
DA_HEADS = 4
QK_COLS = 256
SG_CHUNK = 128
SG_WIDTH = 512
SG_GROUPS = 4
KV_LO = 512
KV_HI = 1536
IN_COLS = 2560
ROPE_THETA = 10000.0
MOE_GROUPS = 4
MOE_EXPERTS_PER_GROUP = 8
N_EXPERTS = 32
MOE_BLOCK = 128
EPS = 1e-5
DEPTH = 1
DEEPNORM_ALPHA = (2.0 * DEPTH) ** 0.25
LAM_INIT = 0.8 - 0.6 * math.exp(-0.3 * 0)
Q_SCALE = DA_HEAD_DIM ** -0.5
LOG2E = math.log2(math.e)
SQRT_HALF = math.sqrt(0.5)

LANES = 128
TOK_TILE = 512
ATTN_TQ = 2048
ATTN_SUB = 256
VMEM_LIMIT = 56 * 1024 * 1024


def _cparams(sem):
    return pltpu.CompilerParams(dimension_semantics=sem, vmem_limit_bytes=VMEM_LIMIT)


def _dot(a, b):
    return jnp.dot(a, b, preferred_element_type=F32)


def _dot_hi(a, b):
    return jnp.dot(a, b, preferred_element_type=F32, precision=lax.Precision.HIGHEST)


def _layer_norm(y, g, b):
    mu = jnp.mean(y, -1, keepdims=True)
    yc = y - mu
    var = jnp.mean(yc * yc, -1, keepdims=True)
    return yc * lax.rsqrt(var + EPS) * g + b


def _row(ref, n):
    return ref.at[pl.ds(n, 1)]


def _mod_kernel(c_ref, w_ref, b_ref, o_ref):
    c = c_ref[...]
    s = c * (1.0 / (1.0 + jnp.exp(-c)))
    o_ref[...] = _dot_hi(s, w_ref[...]) + b_ref[...]


def _mod_call(cc, w_mod, b_mod):
    rows = cc.shape[0]
    n_out = w_mod.shape[1]
    bn = 1024
    return pl.pallas_call(
        _mod_kernel,
        grid=(n_out // bn,),
        in_specs=[
            pl.BlockSpec((rows, D_MODEL), lambda j: (0, 0)),
            pl.BlockSpec((D_MODEL, bn), lambda j: (0, j)),
            pl.BlockSpec((1, bn), lambda j: (0, j)),
        ],
        out_specs=pl.BlockSpec((rows, bn), lambda j: (0, j)),
        out_shape=jax.ShapeDtypeStruct((rows, n_out), F32),
        compiler_params=_cparams(("arbitrary",)),
        name="mod",
    )(cc, w_mod, b_mod)


def _ctx_kernel(c_ref, sc_ref, sh_ref, w_ref, kc_ref, vc_ref):
    h = (c_ref[0] * (1.0 + sc_ref[...]) + sh_ref[...]).astype(BF16)
    p = _dot(h, w_ref[...])
    kc_ref[0] = p[:, :2 * QK_COLS].astype(BF16)
    vc_ref[0] = p[:, 2 * QK_COLS:].astype(BF16)


def _ctx_call(ctx, csc1, csh1, w_kv):
    b, cl, _ = ctx.shape
    return pl.pallas_call(
        _ctx_kernel,
        grid=(b,),
        in_specs=[
            pl.BlockSpec((1, cl, D_MODEL), lambda i: (i, 0, 0)),
            pl.BlockSpec((1, D_MODEL), lambda i: (0, 0)),
            pl.BlockSpec((1, D_MODEL), lambda i: (0, 0)),
            pl.BlockSpec((D_MODEL, 2 * QK_COLS + DA_WIDTH), lambda i: (0, 0)),
        ],
        out_specs=[
            pl.BlockSpec((1, cl, 2 * QK_COLS), lambda i: (i, 0, 0)),
            pl.BlockSpec((1, cl, DA_WIDTH), lambda i: (i, 0, 0)),
        ],
        out_shape=[
            jax.ShapeDtypeStruct((b, cl, 2 * QK_COLS), BF16),
            jax.ShapeDtypeStruct((b, cl, DA_WIDTH), BF16),
        ],
        compiler_params=_cparams(("arbitrary",)),
        name="ctx_kv",
    )(ctx, csc1, csh1, w_kv)


def _in_kernel(x_ref, sc_ref, sh_ref, w_ref, cos_ref, sin_ref, lng_ref, lnb_ref,
               sw_ref, sb_ref, q_ref, k_ref, v_ref, sg_ref):
    tm = x_ref.shape[1]
    h = (x_ref[0] * (1.0 + sc_ref[0]) + sh_ref[0]).astype(BF16)
    p = _dot(h, w_ref[...])

    cos = cos_ref[...]
    sin = sin_ref[...]
    lane = lax.broadcasted_iota(jnp.int32, (tm, LANES), 1)
    first = (lane % 32) < 16

    def rope(t):
        partner = jnp.where(first, pltpu.roll(t, LANES - 16, 1), pltpu.roll(t, 16, 1))
        return t * cos + partner * sin

    for c in range(4):
        cs = slice(c * LANES, (c + 1) * LANES)
        q_ref[0, :, cs] = rope(p[:, cs] * (Q_SCALE * LOG2E)).astype(BF16)
        k_ref[0, :, cs] = rope(p[:, KV_LO + c * LANES:KV_LO + (c + 1) * LANES]).astype(BF16)
    v_ref[0] = p[:, 2 * KV_LO:KV_HI].astype(BF16)

    z = p[:, KV_HI:]
    gz = 0.5 * z * (1.0 + lax.erf(z * SQRT_HALF))
    u = gz[:, :SG_WIDTH]
    vn = _layer_norm(gz[:, SG_WIDTH:], lng_ref[...], lnb_ref[...]).astype(BF16)
    for c in range(tm // SG_CHUNK):
        rs = slice(c * SG_CHUNK, (c + 1) * SG_CHUNK)
        for g in range(SG_GROUPS):
            cs = slice(g * LANES, (g + 1) * LANES)
            s = _dot(sw_ref[g], vn[rs, cs]) + sb_ref[:, cs]
            sg_ref[0, rs, cs] = (u[rs, cs] * s).astype(BF16)


def _in_call(x, sc1, sh1, w_in, cos_t, sin_t, lng, lnb, sw, sbias):
    b, l, _ = x.shape
    tm = TOK_TILE
    full = lambda bi, i: (0, 0)
    return pl.pallas_call(
        _in_kernel,
        grid=(b, l // tm),
        in_specs=[
            pl.BlockSpec((1, tm, D_MODEL), lambda bi, i: (bi, i, 0)),
            pl.BlockSpec((1, 1, D_MODEL), lambda bi, i: (bi, 0, 0)),
            pl.BlockSpec((1, 1, D_MODEL), lambda bi, i: (bi, 0, 0)),
            pl.BlockSpec((D_MODEL, IN_COLS), full),
            pl.BlockSpec((tm, LANES), lambda bi, i: (i, 0)),
            pl.BlockSpec((tm, LANES), lambda bi, i: (i, 0)),
            pl.BlockSpec((1, SG_WIDTH), full),
            pl.BlockSpec((1, SG_WIDTH), full),
            pl.BlockSpec((SG_GROUPS, SG_CHUNK, SG_CHUNK), lambda bi, i: (0, 0, 0)),
            pl.BlockSpec((SG_CHUNK, SG_WIDTH), full),
        ],
        out_specs=[pl.BlockSpec((1, tm, 512), lambda bi, i: (bi, i, 0))] * 4,
        out_shape=[jax.ShapeDtypeStruct((b, l, 512), BF16)] * 4,
        compiler_params=_cparams(("arbitrary", "arbitrary")),
        name="in_proj",
    )(x, sc1, sh1, w_in, cos_t, sin_t, lng, lnb, sw, sbias)


def _attn_kernel(lam_ref, q_ref, k_ref, kc_ref, v_ref, vc_ref, g_ref, o_ref, kall, vall):
    lv = lam_ref[...]
    lam = (jnp.exp(jnp.sum(lv[0:1] * lv[1:2], -1, keepdims=True))
           - jnp.exp(jnp.sum(lv[2:3] * lv[3:4], -1, keepdims=True)) + LAM_INIT)
    l = k_ref.shape[1]

    @pl.when(pl.program_id(2) == 0)
    def _():
        kall[:l, :] = k_ref[0]
        kall[l:, :] = kc_ref[0]
        vall[:l, :DA_V_DIM] = v_ref[0]
        vall[l:, :DA_V_DIM] = vc_ref[0]
        vall[:, DA_V_DIM:] = jnp.ones((vall.shape[0], DA_V_DIM), BF16)

    nt = (((1,), (1,)), ((), ()))

    def branch(qm):
        s = lax.dot_general(qm, kall[...], nt, preferred_element_type=F32)
        m = jnp.max(s, -1, keepdims=True)
        e = jnp.exp2((s - m).astype(BF16))
        oe = _dot(e, vall[...])
        return oe[:, :DA_V_DIM] / oe[:, DA_V_DIM:DA_V_DIM + 1]

    for r in range(q_ref.shape[1] // ATTN_SUB):
        rs = pl.ds(r * ATTN_SUB, ATTN_SUB)
        q = q_ref[0, rs, :]
        lane = lax.broadcasted_iota(jnp.int32, q.shape, 1)
        zero = jnp.zeros_like(q)
        o = branch(jnp.where(lane < DA_HEAD_DIM, q, zero)) - lam * branch(jnp.where(lane >= DA_HEAD_DIM, q, zero))
        of = o * lax.rsqrt(jnp.mean(o * o, -1, keepdims=True) + EPS) * g_ref[...]
        o_ref[0, rs, :] = (of * (1.0 - LAM_INIT)).astype(BF16)


def _attn_call(lamv, q, k, kc, v, vc, subln_g):
    b, l, _ = q.shape
    cl = kc.shape[1]
    tq = ATTN_TQ
    return pl.pallas_call(
        _attn_kernel,
        grid=(b, DA_HEADS, l // tq),
        in_specs=[
            pl.BlockSpec((4, DA_HEAD_DIM), lambda bi, h, i: (0, 0)),
            pl.BlockSpec((1, tq, LANES), lambda bi, h, i: (bi, i, h)),
            pl.BlockSpec((1, l, LANES), lambda bi, h, i: (bi, 0, h)),
            pl.BlockSpec((1, cl, LANES), lambda bi, h, i: (bi, 0, h)),
            pl.BlockSpec((1, l, LANES), lambda bi, h, i: (bi, 0, h)),
            pl.BlockSpec((1, cl, LANES), lambda bi, h, i: (bi, 0, h)),
            pl.BlockSpec((1, DA_V_DIM), lambda bi, h, i: (0, 0)),
        ],
        out_specs=pl.BlockSpec((1, tq, LANES), lambda bi, h, i: (bi, i, h)),
        out_shape=jax.ShapeDtypeStruct((b, l, DA_WIDTH), BF16),
        scratch_shapes=[pltpu.VMEM((l + cl, LANES), BF16), pltpu.VMEM((l + cl, 2 * DA_V_DIM), BF16)],
        compiler_params=_cparams(("arbitrary", "arbitrary", "arbitrary")),
        name="attn",
    )(lamv, q, k, kc, v, vc, subln_g)


def _out_kernel(da_ref, sg_ref, x_ref, g1_ref, sc_ref, sh_ref, w_ref, lg_ref, lb_ref,
                wr_ref, br_ref, x1_ref, t_ref, route_ref, wts_ref, cnt_ref):
    tm = x_ref.shape[0]
    y = _dot(da_ref[...], w_ref[:DA_WIDTH, :]) + _dot(sg_ref[...], w_ref[DA_WIDTH:, :])
    x1 = _layer_norm(DEEPNORM_ALPHA * x_ref[...] + g1_ref[0] * y, lg_ref[...], lb_ref[...])
    x1_ref[...] = x1
    t = x1 * (1.0 + sc_ref[0]) + sh_ref[0]
    t_ref[...] = t

    t_hi = t.astype(BF16)
    t_lo = (t - t_hi.astype(F32)).astype(BF16)
    nt_dims = (((1,), (1,)), ((), ()))
    hw = lax.dot_general(wr_ref[...], t_hi, nt_dims, preferred_element_type=F32)
    lt = (hw[:LANES] + hw[LANES:] + lax.dot_general(wr_ref[:LANES, :], t_lo, nt_dims, preferred_element_type=F32)
          + br_ref[...])
    epg = MOE_EXPERTS_PER_GROUP
    row8 = lax.broadcasted_iota(jnp.int32, (epg, tm), 0).astype(F32)
    ninf = jnp.float32(-jnp.inf)
    big = jnp.float32(epg)
    gmask = row8 < MOE_GROUPS
    gl = jnp.where(gmask, lt[0:epg], ninf)
    gmax = jnp.max(gl, 0, keepdims=True)
    gsel = jnp.min(jnp.where(gl == gmax, row8, big), 0, keepdims=True)
    gsum = jnp.sum(jnp.where(gmask, jnp.exp(gl - gmax), 0.0), 0, keepdims=True)
    gw = 1.0 / gsum
    el = lt[epg:2 * epg]
    for g in range(1, MOE_GROUPS):
        el = jnp.where(gsel == g, lt[epg * (g + 1):epg * (g + 2)], el)
    v1 = jnp.max(el, 0, keepdims=True)
    i1 = jnp.min(jnp.where(el == v1, row8, big), 0, keepdims=True)
    el2 = jnp.where(row8 == i1, ninf, el)
    v2 = jnp.max(el2, 0, keepdims=True)
    i2 = jnp.min(jnp.where(el2 == v2, row8, big), 0, keepdims=True)
    e = jnp.exp(v2 - v1)
    w1 = gw / (1.0 + e)
    w2 = gw * e / (1.0 + e)
    e1 = gsel * epg + i1
    e2 = gsel * epg + i2

    rowe = lax.broadcasted_iota(jnp.int32, (N_EXPERTS, tm), 0).astype(F32)
    oh1 = rowe == e1
    oh2 = rowe == e2
    oh = jnp.where(oh1 | oh2, 1.0, 0.0)
    r_i = lax.broadcasted_iota(jnp.int32, (tm, tm), 0)
    c_i = lax.broadcasted_iota(jnp.int32, (tm, tm), 1)
    upper = jnp.where(r_i < c_i, 1.0, 0.0).astype(BF16)
    pref = _dot(oh.astype(BF16), upper)
    r1 = jnp.sum(jnp.where(oh1, pref, 0.0), 0, keepdims=True)
    r2 = jnp.sum(jnp.where(oh2, pref, 0.0), 0, keepdims=True)
    cnt_ref[0] = jnp.broadcast_to(jnp.sum(oh, 1, keepdims=True), (N_EXPERTS, LANES))

    route_ref[...] = jnp.where(row8 == 0, e1, jnp.where(row8 == 1, e2, jnp.where(row8 == 2, r1, jnp.where(row8 == 3, r2, 0.0))))
    rowl = lax.broadcasted_iota(jnp.int32, (LANES, tm), 0)
    wts_ref[...] = jnp.where(rowl < 64, w1, w2).T


def _out_call(da, sg, x2d, g1, sc2, sh2, w_out, lg, lb, wr, br, tiles_per_batch):
    n = x2d.shape[0]
    tm = TOK_TILE
    nt = n // tm
    tpb = tiles_per_batch
    row = lambda i: (i, 0)
    full = lambda i: (0, 0)
    per_b = lambda i: (i // tpb, 0, 0)
    return pl.pallas_call(
        _out_kernel,
        grid=(nt,),
        in_specs=[
            pl.BlockSpec((tm, DA_WIDTH), row),
            pl.BlockSpec((tm, SG_WIDTH), row),
            pl.BlockSpec((tm, D_MODEL), row),
            pl.BlockSpec((1, 1, D_MODEL), per_b),
            pl.BlockSpec((1, 1, D_MODEL), per_b),
            pl.BlockSpec((1, 1, D_MODEL), per_b),
            pl.BlockSpec((D_MODEL, D_MODEL), full),
            pl.BlockSpec((1, D_MODEL), full),
            pl.BlockSpec((1, D_MODEL), full),
            pl.BlockSpec((2 * LANES, D_MODEL), full),
            pl.BlockSpec((LANES, tm), full),
        ],
        out_specs=[
            pl.BlockSpec((tm, D_MODEL), row),
            pl.BlockSpec((tm, D_MODEL), row),
            pl.BlockSpec((8, tm), lambda i: (0, i)),
            pl.BlockSpec((tm, LANES), row),
            pl.BlockSpec((1, N_EXPERTS, LANES), lambda i: (i, 0, 0)),
        ],
        out_shape=[
            jax.ShapeDtypeStruct((n, D_MODEL), F32),
            jax.ShapeDtypeStruct((n, D_MODEL), F32),
            jax.ShapeDtypeStruct((8, n), F32),
            jax.ShapeDtypeStruct((n, LANES), F32),
            jax.ShapeDtypeStruct((nt, N_EXPERTS, LANES), F32),
        ],
        compiler_params=_cparams(("arbitrary",)),
        name="out_proj",
    )(da, sg, x2d, g1, sc2, sh2, w_out, lg, lb, wr, br)


ISSUE_UNROLL = 8


def _dispatch_kernel(zs_ref, ze_ref, dest_ref, t_ref, xb_ref, zero_ref, sem):
    i = pl.program_id(0)
    tm = t_ref.shape[0]

    @pl.when(i == 0)
    def _():
        zero_ref[...] = jnp.zeros_like(zero_ref)

        def per_expert(e, carry):
            def start(r, c):
                pltpu.make_async_copy(_row(zero_ref, 0), _row(xb_ref, r), sem).start()
                return c

            def wait(r, c):
                pltpu.make_async_copy(_row(zero_ref, 0), _row(xb_ref, r), sem).wait()
                return c

            lax.fori_loop(zs_ref[e], ze_ref[e], start, 0)
            lax.fori_loop(zs_ref[e], ze_ref[e], wait, 0)
            return carry

        lax.fori_loop(0, N_EXPERTS, per_expert, 0)

        def blk_copy(j):
            return pltpu.make_async_copy(zero_ref, xb_ref.at[pl.ds(pl.multiple_of(j * MOE_BLOCK, MOE_BLOCK), MOE_BLOCK)], sem)

        def blk_start(j, c):
            blk_copy(j).start()
            return c

        def blk_wait(j, c):
            blk_copy(j).wait()
            return c

        first_unused = ze_ref[N_EXPERTS - 1] // MOE_BLOCK
        n_blocks = xb_ref.shape[0] // MOE_BLOCK
        lax.fori_loop(first_unused, n_blocks, blk_start, 0)
        lax.fori_loop(first_unused, n_blocks, blk_wait, 0)

    def start(n, c):
        pltpu.make_async_copy(_row(t_ref, n), _row(xb_ref, dest_ref[0, 0, n]), sem).start()
        pltpu.make_async_copy(_row(t_ref, n), _row(xb_ref, dest_ref[0, 0, tm + n]), sem).start(priority=1)
        return c

    lax.fori_loop(0, tm, start, 0, unroll=ISSUE_UNROLL)
    for _ in range(2):
        pltpu.make_async_copy(t_ref, xb_ref.at[pl.ds(0, tm)], sem).wait()


def _dispatch_call(zs, ze, dest3, t, p_rows):
    n = t.shape[0]
    tm = TOK_TILE
    grid_spec = pltpu.PrefetchScalarGridSpec(
        num_scalar_prefetch=2,
        grid=(n // tm,),
        in_specs=[
            pl.BlockSpec((1, 1, 2 * tm), lambda i, zs, ze: (i, 0, 0), memory_space=pltpu.SMEM),
            pl.BlockSpec((tm, D_MODEL), lambda i, zs, ze: (i, 0)),
        ],
        out_specs=pl.BlockSpec(memory_space=pl.ANY),
        scratch_shapes=[pltpu.VMEM((MOE_BLOCK, D_MODEL), F32), pltpu.SemaphoreType.DMA(())],
    )
    return pl.pallas_call(
        _dispatch_kernel,
        grid_spec=grid_spec,
        out_shape=jax.ShapeDtypeStruct((p_rows, D_MODEL), F32),
        compiler_params=_cparams(("arbitrary",)),
        name="dispatch",
    )(zs, ze, dest3, t)


ITEM_SUB = 4
ITEM_ROWS = ITEM_SUB * MOE_BLOCK


def _expert_kernel(ie_ref, ib_ref, ins_ref, iord_ref, inxt_ref, ni_ref, xb_ref, wg_ref, wu_ref, wd_ref, yb_ref,
                   xbuf, ybuf, wf32, wgb, wub, wdb, insem, outsem, wsem):
    i = pl.program_id(0)
    n_steps = pl.num_programs(0)
    ni = ni_ref[0]
    slot = i % 2

    def w_copies(e, s):
        return [pltpu.make_async_copy(w.at[e], wf32.at[s, k], wsem.at[s]) for k, w in enumerate((wg_ref, wu_ref, wd_ref))]

    def rows_of(item, j):
        return pl.ds(pl.multiple_of((ib_ref[item] + j) * MOE_BLOCK, MOE_BLOCK), MOE_BLOCK)

    def in_copy(item, s, j):
        return pltpu.make_async_copy(xb_ref.at[rows_of(item, j)], xbuf.at[s, pl.ds(j * MOE_BLOCK, MOE_BLOCK)], insem.at[s])

    def out_copy(item, s, j):
        return pltpu.make_async_copy(ybuf.at[s, pl.ds(j * MOE_BLOCK, MOE_BLOCK)], yb_ref.at[rows_of(item, j)], outsem.at[s])

    def for_blocks(item, fn):
        for j in range(ITEM_SUB):
            @pl.when(j < ins_ref[item])
            def _():
                fn(j)

    @pl.when(i == 0)
    def _():
        for c in w_copies(ie_ref[0], 0):
            c.start()
        for_blocks(0, lambda j: in_copy(0, 0, j).start())

    @pl.when(i < ni)
    def _():
        @pl.when(i >= 2)
        def _():
            for_blocks(i - 2, lambda j: out_copy(i - 2, slot, j).wait())

        @pl.when(i + 1 < ni)
        def _():
            for_blocks(i + 1, lambda j: in_copy(i + 1, 1 - slot, j).start())

        changed = (i == 0) | (ie_ref[i] != ie_ref[jnp.maximum(i - 1, 0)])

        @pl.when(changed)
        def _():
            ws = iord_ref[i] % 2
            for c in w_copies(ie_ref[i], ws):
                c.wait()

            @pl.when(inxt_ref[i] >= 0)
            def _():
                for c in w_copies(inxt_ref[i], 1 - ws):
                    c.start()

            wgb[...] = wf32[ws, 0].astype(BF16)
            wub[...] = wf32[ws, 1].astype(BF16)
            wdb[...] = wf32[ws, 2].astype(BF16)

        for_blocks(i, lambda j: in_copy(i, slot, j).wait())

        for ns in range(1, ITEM_SUB + 1):
            @pl.when(ins_ref[i] == ns)
            def _():
                rows = pl.ds(0, ns * MOE_BLOCK)
                x = xbuf[slot, rows].astype(BF16)
                g = _dot(x, wgb[...])
                u = _dot(x, wub[...])
                hid = (g * (1.0 / (1.0 + jnp.exp(-g))) * u).astype(BF16)
                ybuf[slot, rows] = _dot(hid, wdb[...])

        for_blocks(i, lambda j: out_copy(i, slot, j).start())

    @pl.when(i == n_steps - 1)
    def _():
        for back in (2, 1):
            k = ni - back

            @pl.when(k >= 0)
            def _():
                for_blocks(k, lambda j: out_copy(k, k % 2, j).wait())

        ybuf[0, pl.ds(0, MOE_BLOCK)] = jnp.zeros((MOE_BLOCK, D_MODEL), F32)
        last = jnp.maximum(ni - 1, 0)
        first_unused = ib_ref[last] + ins_ref[last]
        n_blocks = yb_ref.shape[0] // MOE_BLOCK

        def zero_copy(b):
            dst = yb_ref.at[pl.ds(pl.multiple_of(b * MOE_BLOCK, MOE_BLOCK), MOE_BLOCK)]
            return pltpu.make_async_copy(ybuf.at[0, pl.ds(0, MOE_BLOCK)], dst, outsem.at[0])

        def z_start(b, c):
            zero_copy(b).start()
            return c

        def z_wait(b, c):
            zero_copy(b).wait()
            return c

        lax.fori_loop(first_unused, n_blocks, z_start, 0)
        lax.fori_loop(first_unused, n_blocks, z_wait, 0)


def _expert_call(item_e, item_b, item_ns, item_ord, item_nxt, n_items, xb, wg, wu, wd):
    p_rows = xb.shape[0]
    max_items = item_e.shape[0]
    grid_spec = pltpu.PrefetchScalarGridSpec(
        num_scalar_prefetch=6,
        grid=(max_items,),
        in_specs=[pl.BlockSpec(memory_space=pl.ANY)] * 4,
        out_specs=pl.BlockSpec(memory_space=pl.ANY),
        scratch_shapes=[
            pltpu.VMEM((2, ITEM_ROWS, D_MODEL), F32),
            pltpu.VMEM((2, ITEM_ROWS, D_MODEL), F32),
            pltpu.VMEM((2, 3, D_MODEL, D_MODEL), F32),
            pltpu.VMEM((D_MODEL, D_MODEL), BF16),
            pltpu.VMEM((D_MODEL, D_MODEL), BF16),
            pltpu.VMEM((D_MODEL, D_MODEL), BF16),
            pltpu.SemaphoreType.DMA((2,)),
            pltpu.SemaphoreType.DMA((2,)),
            pltpu.SemaphoreType.DMA((2,)),
        ],
    )
    return pl.pallas_call(
        _expert_kernel,
        grid_spec=grid_spec,
        out_shape=jax.ShapeDtypeStruct((p_rows, D_MODEL), F32),
        compiler_params=_cparams(("arbitrary",)),
        name="experts",
    )(item_e, item_b, item_ns, item_ord, item_nxt, n_items, xb, wg, wu, wd)


def _combine_kernel(dcur_ref, dnxt_ref, yb_ref, x1_ref, wts_ref, g2_ref, lg_ref, lb_ref, o_ref, buf, sem):
    i = pl.program_id(0)
    tm = x1_ref.shape[0]
    slot = i % 2

    def issue(dref, s):
        def start(n, c):
            pltpu.make_async_copy(_row(yb_ref, dref[0, 0, n]), _row(buf.at[s, 0], n), sem.at[s]).start()
            pltpu.make_async_copy(_row(yb_ref, dref[0, 0, tm + n]), _row(buf.at[s, 1], n), sem.at[s]).start(priority=1)
            return c

        lax.fori_loop(0, tm, start, 0, unroll=ISSUE_UNROLL)

    @pl.when(i == 0)
    def _():
        issue(dcur_ref, 0)

    @pl.when(i + 1 < pl.num_programs(0))
    def _():
        issue(dnxt_ref, 1 - slot)

    for k in range(2):
        pltpu.make_async_copy(yb_ref.at[pl.ds(0, tm)], buf.at[slot, k], sem.at[slot]).wait()
    w = wts_ref[...]
    f = w[:, 0:1] * buf[slot, 0] + w[:, 64:65] * buf[slot, 1]
    o_ref[...] = _layer_norm(DEEPNORM_ALPHA * x1_ref[...] + g2_ref[0] * f, lg_ref[...], lb_ref[...])


def _combine_call(dest3, yb, x1, wts, g2, lg, lb, tiles_per_batch):
    n = x1.shape[0]
    tm = TOK_TILE
    tpb = tiles_per_batch
    row = lambda i: (i, 0)
    full = lambda i: (0, 0)
    nt = n // tm
    return pl.pallas_call(
        _combine_kernel,
        grid=(nt,),
        in_specs=[
            pl.BlockSpec((1, 1, 2 * tm), lambda i: (i, 0, 0), memory_space=pltpu.SMEM),
            pl.BlockSpec((1, 1, 2 * tm), lambda i: (jnp.minimum(i + 1, nt - 1), 0, 0), memory_space=pltpu.SMEM),
            pl.BlockSpec(memory_space=pl.ANY),
            pl.BlockSpec((tm, D_MODEL), row),
            pl.BlockSpec((tm, LANES), row),
            pl.BlockSpec((1, 1, D_MODEL), lambda i: (i // tpb, 0, 0)),
            pl.BlockSpec((1, D_MODEL), full),
            pl.BlockSpec((1, D_MODEL), full),
        ],
        out_specs=pl.BlockSpec((tm, D_MODEL), row),
        out_shape=jax.ShapeDtypeStruct((n, D_MODEL), F32),
        scratch_shapes=[pltpu.VMEM((2, 2, tm, D_MODEL), F32), pltpu.SemaphoreType.DMA((2,))],
        compiler_params=_cparams(("arbitrary",)),
        name="combine",
    )(dest3, dest3, yb, x1, wts, g2, lg, lb)


def _head_interleave(w, lo):
    blk = w[:, lo:lo + 2 * QK_COLS].reshape(D_MODEL, 2, DA_HEADS, DA_HEAD_DIM)
    return blk.transpose(0, 2, 1, 3).reshape(D_MODEL, 2 * QK_COLS)


def _cumsum_small(x):
    idx = jnp.arange(x.shape[0], dtype=jnp.int32)
    return jnp.sum(jnp.where(idx[None, :] <= idx[:, None], x[None, :], 0), -1)


def _lookup(table, idx):
    return jnp.sum(jnp.where(idx[:, None] == jnp.arange(table.shape[0], dtype=jnp.int32)[None, :], table[None, :], 0), -1)


def _rope_tables(seq):
    rows_n = seq // GRID_W
    rows = jnp.repeat(jnp.arange(rows_n, dtype=F32), GRID_W)
    cols = jnp.tile(jnp.arange(GRID_W, dtype=F32), rows_n)
    half = DA_HEAD_DIM // 4
    inv = ROPE_THETA ** (-jnp.arange(half, dtype=F32) / half)
    ang_r = rows[:, None] * inv[None, :]
    ang_c = cols[:, None] * inv[None, :]
    cos64 = jnp.concatenate([jnp.cos(ang_r), jnp.cos(ang_r), jnp.cos(ang_c), jnp.cos(ang_c)], -1)
    sin64 = jnp.concatenate([-jnp.sin(ang_r), jnp.sin(ang_r), -jnp.sin(ang_c), jnp.sin(ang_c)], -1)
    return jnp.tile(cos64, (1, 2)), jnp.tile(sin64, (1, 2))


def kernel(x, c, ctx, c_ctx, w_mod, b_mod, w_in, lam_q1, lam_k1, lam_q2, lam_k2, subln_g, sg_ln_g, sg_ln_b, sg_w, sg_b, w_out, ln1_g, ln1_b, router_group_w, router_group_b, router_expert_w, router_expert_b, exp_w_gate, exp_w_up, exp_w_down, ln2_g, ln2_b):
    b, l, d = x.shape
    n = b * l
    tm = TOK_TILE
    nt = n // tm

    cc = jnp.zeros((b + 8, d), F32).at[:b].set(c).at[b].set(c_ctx)
    mod = _mod_call(cc, w_mod[0], b_mod[0][None, :])
    sh1, sc1, g1, sh2, sc2, g2 = [mod[:b, j * d:(j + 1) * d].reshape(b, 1, d) for j in range(6)]
    csh1 = mod[b:b + 1, 0:d]
    csc1 = mod[b:b + 1, d:2 * d]

    wi = w_in[0]
    w_all = jnp.concatenate([_head_interleave(wi, 0), _head_interleave(wi, KV_LO), wi[:, 2 * KV_LO:]], -1).astype(BF16)
    kc, vc = _ctx_call(ctx, csc1, csh1, w_all[:, KV_LO:KV_HI])

    cos_t, sin_t = _rope_tables(l)
    sbias = jnp.repeat(sg_b[0].T, LANES, axis=1)
    q, k, v, sg = _in_call(x, sc1, sh1, w_all, cos_t, sin_t, sg_ln_g[0][None, :], sg_ln_b[0][None, :],
                           sg_w[0].astype(BF16), sbias)

    lamv = jnp.stack([lam_q1[0], lam_k1[0], lam_q2[0], lam_k2[0]]).astype(F32)
    da = _attn_call(lamv, q, k, kc, v, vc, subln_g[0][None, :])

    e_lo = MOE_EXPERTS_PER_GROUP
    wr = jnp.zeros((LANES, d), F32).at[:MOE_GROUPS].set(router_group_w[0].T).at[e_lo:e_lo + N_EXPERTS].set(router_expert_w[0].T)
    br = jnp.zeros((LANES,), F32).at[:MOE_GROUPS].set(router_group_b[0]).at[e_lo:e_lo + N_EXPERTS].set(router_expert_b[0])
    br = jnp.broadcast_to(br[:, None], (LANES, tm))
    wr_hi = wr.astype(BF16)
    wr_split = jnp.concatenate([wr_hi, (wr - wr_hi.astype(F32)).astype(BF16)], 0)
    x1, t, route, wts, tcnt = _out_call(da.reshape(n, DA_WIDTH), sg.reshape(n, SG_WIDTH), x.reshape(n, d), g1, sc2, sh2,
                                        w_out[0].astype(BF16), ln1_g[0][None, :], ln1_b[0][None, :], wr_split, br, l // tm)

    cnt_te = tcnt[:, :, 0].astype(jnp.int32)
    counts = jnp.sum(cnt_te, 0)
    padded = (counts + MOE_BLOCK - 1) // MOE_BLOCK * MOE_BLOCK
    pad_end = _cumsum_small(padded)
    pad_start = pad_end - padded
    tix = jnp.arange(nt, dtype=jnp.int32)
    tile_prefix = jnp.sum(jnp.where(tix[None, :, None] < tix[:, None, None], cnt_te[None], 0), 1)
    base = pad_start[None, :] + tile_prefix
    ridx = route[:4].astype(jnp.int32).reshape(4, nt, tm)
    ex = jnp.arange(N_EXPERTS, dtype=jnp.int32)

    def slot_dest(eid, rank):
        return jnp.sum(jnp.where(eid[None] == ex[:, None, None], base.T[:, :, None], 0), 0) + rank

    dest3 = jnp.concatenate([slot_dest(ridx[0], ridx[2]), slot_dest(ridx[1], ridx[3])], -1).reshape(nt, 1, 2 * tm)
    n_blocks = (n * 2) // MOE_BLOCK + N_EXPERTS
    p_rows = n_blocks * MOE_BLOCK
    nb_e = padded // MOE_BLOCK
    items_e = (nb_e + ITEM_SUB - 1) // ITEM_SUB
    item_end = _cumsum_small(items_e)
    max_items = (n_blocks + (ITEM_SUB - 1) * N_EXPERTS) // ITEM_SUB
    it = jnp.arange(max_items, dtype=jnp.int32)
    item_e = jnp.minimum(jnp.sum((it[:, None] >= item_end[None, :]).astype(jnp.int32), -1), N_EXPERTS - 1)
    item_j = it - _lookup(item_end - items_e, item_e)
    item_b = _lookup(pad_start // MOE_BLOCK, item_e) + ITEM_SUB * item_j
    item_ns = jnp.clip(_lookup(nb_e, item_e) - ITEM_SUB * item_j, 0, ITEM_SUB)
    nonempty = (items_e > 0).astype(jnp.int32)
    ord_e = _cumsum_small(nonempty) - 1
    later = (ex[None, :] > ex[:, None]) & (nonempty[None, :] > 0)
    nxt_e = jnp.min(jnp.where(later, ex[None, :], N_EXPERTS), -1)
    nxt_e = jnp.where(nxt_e == N_EXPERTS, -1, nxt_e)
    item_ord = _lookup(ord_e, item_e)
    item_nxt = _lookup(nxt_e, item_e)

    xb = _dispatch_call((pad_start + counts).astype(jnp.int32), pad_end.astype(jnp.int32), dest3, t, p_rows)
    yb = _expert_call(item_e, item_b.astype(jnp.int32), item_ns.astype(jnp.int32), item_ord.astype(jnp.int32),
                      item_nxt.astype(jnp.int32), item_end[-1:].astype(jnp.int32),
                      xb, exp_w_gate[0], exp_w_up[0], exp_w_down[0])
    out = _combine_call(dest3, yb, x1, wts, g2, ln2_g[0][None, :], ln2_b[0][None, :], l // tm)
    return out.reshape(b, l, d)
```

```python
import functools
import math

import jax
import jax.numpy as jnp
from jax import lax
from jax.experimental import pallas as pl
from jax.experimental.pallas import tpu as pltpu

F32 = jnp.float32
BF16 = jnp.bfloat16

D_MODEL = 1024
GRID_W = 64
DA_HEAD_DIM = 64
DA_V_DIM = 128
DA_WIDTH = 512
DA_HEADS = 4
QK_COLS = 256
SG_CHUNK = 128
SG_WIDTH = 512
SG_GROUPS = 4
KV_LO = 512
KV_HI = 1536
IN_COLS = 2560
ROPE_THETA = 10000.0
MOE_GROUPS = 4
MOE_EXPERTS_PER_GROUP = 8
N_EXPERTS = 32
MOE_BLOCK = 128
EPS = 1e-5
DEPTH = 1
DEEPNORM_ALPHA = (2.0 * DEPTH) ** 0.25
LAM_INIT = 0.8 - 0.6 * math.exp(-0.3 * 0)
Q_SCALE = DA_HEAD_DIM ** -0.5
LOG2E = math.log2(math.e)
SQRT_HALF = math.sqrt(0.5)

LANES = 128
TOK_TILE = 512
ATTN_TQ = 2048
ATTN_SUB = 256
VMEM_LIMIT = 56 * 1024 * 1024


def _cparams(sem):
    return pltpu.CompilerParams(dimension_semantics=sem, vmem_limit_bytes=VMEM_LIMIT)


def _dot(a, b):
    return jnp.dot(a, b, preferred_element_type=F32)


def _dot_hi(a, b):
    return jnp.dot(a, b, preferred_element_type=F32, precision=lax.Precision.HIGHEST)


def _layer_norm(y, g, b):
    mu = jnp.mean(y, -1, keepdims=True)
    yc = y - mu
    var = jnp.mean(yc * yc, -1, keepdims=True)
    return yc * lax.rsqrt(var + EPS) * g + b


def _row(ref, n):
    return ref.at[pl.ds(n, 1)]


def _mod_kernel(c_ref, w_ref, b_ref, o_ref):
    c = c_ref[...]
    s = c * (1.0 / (1.0 + jnp.exp(-c)))
    o_ref[...] = _dot_hi(s, w_ref[...]) + b_ref[...]


def _mod_call(cc, w_mod, b_mod):
    rows = cc.shape[0]
    n_out = w_mod.shape[1]
    bn = 1024
    return pl.pallas_call(
        _mod_kernel,
        grid=(n_out // bn,),
        in_specs=[
            pl.BlockSpec((rows, D_MODEL), lambda j: (0, 0)),
            pl.BlockSpec((D_MODEL, bn), lambda j: (0, j)),
            pl.BlockSpec((1, bn), lambda j: (0, j)),
        ],
        out_specs=pl.BlockSpec((rows, bn), lambda j: (0, j)),
        out_shape=jax.ShapeDtypeStruct((rows, n_out), F32),
        compiler_params=_cparams(("arbitrary",)),
        name="mod",
    )(cc, w_mod, b_mod)


def _ctx_kernel(c_ref, sc_ref, sh_ref, w_ref, kc_ref, vc_ref):
    h = (c_ref[0] * (1.0 + sc_ref[...]) + sh_ref[...]).astype(BF16)
    p = _dot(h, w_ref[...])
    kc_ref[0] = p[:, :2 * QK_COLS].astype(BF16)
    vc_ref[0] = p[:, 2 * QK_COLS:].astype(BF16)


def _ctx_call(ctx, csc1, csh1, w_kv):
    b, cl, _ = ctx.shape
    return pl.pallas_call(
        _ctx_kernel,
        grid=(b,),
        in_specs=[
            pl.BlockSpec((1, cl, D_MODEL), lambda i: (i, 0, 0)),
            pl.BlockSpec((1, D_MODEL), lambda i: (0, 0)),
            pl.BlockSpec((1, D_MODEL), lambda i: (0, 0)),
            pl.BlockSpec((D_MODEL, 2 * QK_COLS + DA_WIDTH), lambda i: (0, 0)),
        ],
        out_specs=[
            pl.BlockSpec((1, cl, 2 * QK_COLS), lambda i: (i, 0, 0)),
            pl.BlockSpec((1, cl, DA_WIDTH), lambda i: (i, 0, 0)),
        ],
        out_shape=[
            jax.ShapeDtypeStruct((b, cl, 2 * QK_COLS), BF16),
            jax.ShapeDtypeStruct((b, cl, DA_WIDTH), BF16),
        ],
        compiler_params=_cparams(("arbitrary",)),
        name="ctx_kv",
    )(ctx, csc1, csh1, w_kv)


def _in_kernel(x_ref, sc_ref, sh_ref, w_ref, cos_ref, sin_ref, lng_ref, lnb_ref,
               sw_ref, sb_ref, q_ref, k_ref, v_ref, sg_ref):
    tm = x_ref.shape[1]
    h = (x_ref[0] * (1.0 + sc_ref[0]) + sh_ref[0]).astype(BF16)
    p = _dot(h, w_ref[...])

    cos = cos_ref[...]
    sin = sin_ref[...]
    lane = lax.broadcasted_iota(jnp.int32, (tm, LANES), 1)
    first = (lane % 32) < 16

    def rope(t):
        partner = jnp.where(first, pltpu.roll(t, LANES - 16, 1), pltpu.roll(t, 16, 1))
        return t * cos + partner * sin

    for c in range(4):
        cs = slice(c * LANES, (c + 1) * LANES)
        q_ref[0, :, cs] = rope(p[:, cs] * (Q_SCALE * LOG2E)).astype(BF16)
        k_ref[0, :, cs] = rope(p[:, KV_LO + c * LANES:KV_LO + (c + 1) * LANES]).astype(BF16)
    v_ref[0] = p[:, 2 * KV_LO:KV_HI].astype(BF16)

    z = p[:, KV_HI:]
    gz = 0.5 * z * (1.0 + lax.erf(z * SQRT_HALF))
    u = gz[:, :SG_WIDTH]
    vn = _layer_norm(gz[:, SG_WIDTH:], lng_ref[...], lnb_ref[...]).astype(BF16)
    for c in range(tm // SG_CHUNK):
        rs = slice(c * SG_CHUNK, (c + 1) * SG_CHUNK)
        for g in range(SG_GROUPS):
            cs = slice(g * LANES, (g + 1) * LANES)
            s = _dot(sw_ref[g], vn[rs, cs]) + sb_ref[:, cs]
            sg_ref[0, rs, cs] = (u[rs, cs] * s).astype(BF16)


def _in_call(x, sc1, sh1, w_in, cos_t, sin_t, lng, lnb, sw, sbias):
    b, l, _ = x.shape
    tm = TOK_TILE
    full = lambda bi, i: (0, 0)
    return pl.pallas_call(
        _in_kernel,
        grid=(b, l // tm),
        in_specs=[
            pl.BlockSpec((1, tm, D_MODEL), lambda bi, i: (bi, i, 0)),
            pl.BlockSpec((1, 1, D_MODEL), lambda bi, i: (bi, 0, 0)),
            pl.BlockSpec((1, 1, D_MODEL), lambda bi, i: (bi, 0, 0)),
            pl.BlockSpec((D_MODEL, IN_COLS), full),
            pl.BlockSpec((tm, LANES), lambda bi, i: (i, 0)),
            pl.BlockSpec((tm, LANES), lambda bi, i: (i, 0)),
            pl.BlockSpec((1, SG_WIDTH), full),
            pl.BlockSpec((1, SG_WIDTH), full),
            pl.BlockSpec((SG_GROUPS, SG_CHUNK, SG_CHUNK), lambda bi, i: (0, 0, 0)),
            pl.BlockSpec((SG_CHUNK, SG_WIDTH), full),
        ],
        out_specs=[pl.BlockSpec((1, tm, 512), lambda bi, i: (bi, i, 0))] * 4,
        out_shape=[jax.ShapeDtypeStruct((b, l, 512), BF16)] * 4,
        compiler_params=_cparams(("arbitrary", "arbitrary")),
        name="in_proj",
    )(x, sc1, sh1, w_in, cos_t, sin_t, lng, lnb, sw, sbias)


def _attn_kernel(lam_ref, q_ref, k_ref, kc_ref, v_ref, vc_ref, g_ref, o_ref, kall, vall):
    lv = lam_ref[...]
    lam = (jnp.exp(jnp.sum(lv[0:1] * lv[1:2], -1, keepdims=True))
           - jnp.exp(jnp.sum(lv[2:3] * lv[3:4], -1, keepdims=True)) + LAM_INIT)
    l = k_ref.shape[1]

    @pl.when(pl.program_id(2) == 0)
    def _():
        kall[:l, :] = k_ref[0]
        kall[l:, :] = kc_ref[0]
        vall[:l, :DA_V_DIM] = v_ref[0]
        vall[l:, :DA_V_DIM] = vc_ref[0]
        vall[:, DA_V_DIM:] = jnp.ones((vall.shape[0], DA_V_DIM), BF16)

    nt = (((1,), (1,)), ((), ()))

    def branch(qm):
        s = lax.dot_general(qm, kall[...], nt, preferred_element_type=F32)
        m = jnp.max(s, -1, keepdims=True)
        e = jnp.exp2((s - m).astype(BF16))
        oe = _dot(e, vall[...])
        return oe[:, :DA_V_DIM] / oe[:, DA_V_DIM:DA_V_DIM + 1]

    for r in range(q_ref.shape[1] // ATTN_SUB):
        rs = pl.ds(r * ATTN_SUB, ATTN_SUB)
        q = q_ref[0, rs, :]
        lane = lax.broadcasted_iota(jnp.int32, q.shape, 1)
        zero = jnp.zeros_like(q)
        o = branch(jnp.where(lane < DA_HEAD_DIM, q, zero)) - lam * branch(jnp.where(lane >= DA_HEAD_DIM, q, zero))
        of = o * lax.rsqrt(jnp.mean(o * o, -1, keepdims=True) + EPS) * g_ref[...]
        o_ref[0, rs, :] = (of * (1.0 - LAM_INIT)).astype(BF16)


def _attn_call(lamv, q, k, kc, v, vc, subln_g):
    b, l, _ = q.shape
    cl = kc.shape[1]
    tq = ATTN_TQ
    return pl.pallas_call(
        _attn_kernel,
        grid=(b, DA_HEADS, l // tq),
        in_specs=[
            pl.BlockSpec((4, DA_HEAD_DIM), lambda bi, h, i: (0, 0)),
            pl.BlockSpec((1, tq, LANES), lambda bi, h, i: (bi, i, h)),
            pl.BlockSpec((1, l, LANES), lambda bi, h, i: (bi, 0, h)),
            pl.BlockSpec((1, cl, LANES), lambda bi, h, i: (bi, 0, h)),
            pl.BlockSpec((1, l, LANES), lambda bi, h, i: (bi, 0, h)),
            pl.BlockSpec((1, cl, LANES), lambda bi, h, i: (bi, 0, h)),
            pl.BlockSpec((1, DA_V_DIM), lambda bi, h, i: (0, 0)),
        ],
        out_specs=pl.BlockSpec((1, tq, LANES), lambda bi, h, i: (bi, i, h)),
        out_shape=jax.ShapeDtypeStruct((b, l, DA_WIDTH), BF16),
        scratch_shapes=[pltpu.VMEM((l + cl, LANES), BF16), pltpu.VMEM((l + cl, 2 * DA_V_DIM), BF16)],
        compiler_params=_cparams(("arbitrary", "arbitrary", "arbitrary")),
        name="attn",
    )(lamv, q, k, kc, v, vc, subln_g)


def _out_kernel(da_ref, sg_ref, x_ref, g1_ref, sc_ref, sh_ref, w_ref, lg_ref, lb_ref,
                wr_ref, br_ref, x1_ref, t_ref, route_ref, wts_ref, cnt_ref):
    tm = x_ref.shape[0]
    y = _dot(da_ref[...], w_ref[:DA_WIDTH, :]) + _dot(sg_ref[...], w_ref[DA_WIDTH:, :])
    x1 = _layer_norm(DEEPNORM_ALPHA * x_ref[...] + g1_ref[0] * y, lg_ref[...], lb_ref[...])
    x1_ref[...] = x1
    t = x1 * (1.0 + sc_ref[0]) + sh_ref[0]
    t_ref[...] = t

    t_hi = t.astype(BF16)
    t_lo = (t - t_hi.astype(F32)).astype(BF16)
    nt_dims = (((1,), (1,)), ((), ()))
    hw = lax.dot_general(wr_ref[...], t_hi, nt_dims, preferred_element_type=F32)
    lt = (hw[:LANES] + hw[LANES:] + lax.dot_general(wr_ref[:LANES, :], t_lo, nt_dims, preferred_element_type=F32)
          + br_ref[...])
    epg = MOE_EXPERTS_PER_GROUP
    row8 = lax.broadcasted_iota(jnp.int32, (epg, tm), 0).astype(F32)
    ninf = jnp.float32(-jnp.inf)
    big = jnp.float32(epg)
    gmask = row8 < MOE_GROUPS
    gl = jnp.where(gmask, lt[0:epg], ninf)
    gmax = jnp.max(gl, 0, keepdims=True)
    gsel = jnp.min(jnp.where(gl == gmax, row8, big), 0, keepdims=True)
    gsum = jnp.sum(jnp.where(gmask, jnp.exp(gl - gmax), 0.0), 0, keepdims=True)
    gw = 1.0 / gsum
    el = lt[epg:2 * epg]
    for g in range(1, MOE_GROUPS):
        el = jnp.where(gsel == g, lt[epg * (g + 1):epg * (g + 2)], el)
    v1 = jnp.max(el, 0, keepdims=True)
    i1 = jnp.min(jnp.where(el == v1, row8, big), 0, keepdims=True)
    el2 = jnp.where(row8 == i1, ninf, el)
    v2 = jnp.max(el2, 0, keepdims=True)
    i2 = jnp.min(jnp.where(el2 == v2, row8, big), 0, keepdims=True)
    e = jnp.exp(v2 - v1)
    w1 = gw / (1.0 + e)
    w2 = gw * e / (1.0 + e)
    e1 = gsel * epg + i1
    e2 = gsel * epg + i2

    rowe = lax.broadcasted_iota(jnp.int32, (N_EXPERTS, tm), 0).astype(F32)
    oh1 = rowe == e1
    oh2 = rowe == e2
    oh = jnp.where(oh1 | oh2, 1.0, 0.0)
    r_i = lax.broadcasted_iota(jnp.int32, (tm, tm), 0)
    c_i = lax.broadcasted_iota(jnp.int32, (tm, tm), 1)
    upper = jnp.where(r_i < c_i, 1.0, 0.0).astype(BF16)
    pref = _dot(oh.astype(BF16), upper)
    r1 = jnp.sum(jnp.where(oh1, pref, 0.0), 0, keepdims=True)
    r2 = jnp.sum(jnp.where(oh2, pref, 0.0), 0, keepdims=True)
    cnt_ref[0] = jnp.broadcast_to(jnp.sum(oh, 1, keepdims=True), (N_EXPERTS, LANES))

    route_ref[...] = jnp.where(row8 == 0, e1, jnp.where(row8 == 1, e2, jnp.where(row8 == 2, r1, jnp.where(row8 == 3, r2, 0.0))))
    rowl = lax.broadcasted_iota(jnp.int32, (LANES, tm), 0)
    wts_ref[...] = jnp.where(rowl < 64, w1, w2).T


def _out_call(da, sg, x2d, g1, sc2, sh2, w_out, lg, lb, wr, br, tiles_per_batch):
    n = x2d.shape[0]
    tm = TOK_TILE
    nt = n // tm
    tpb = tiles_per_batch
    row = lambda i: (i, 0)
    full = lambda i: (0, 0)
    per_b = lambda i: (i // tpb, 0, 0)
    return pl.pallas_call(
        _out_kernel,
        grid=(nt,),
        in_specs=[
            pl.BlockSpec((tm, DA_WIDTH), row),
            pl.BlockSpec((tm, SG_WIDTH), row),
            pl.BlockSpec((tm, D_MODEL), row),
            pl.BlockSpec((1, 1, D_MODEL), per_b),
            pl.BlockSpec((1, 1, D_MODEL), per_b),
            pl.BlockSpec((1, 1, D_MODEL), per_b),
            pl.BlockSpec((D_MODEL, D_MODEL), full),
            pl.BlockSpec((1, D_MODEL), full),
            pl.BlockSpec((1, D_MODEL), full),
            pl.BlockSpec((2 * LANES, D_MODEL), full),
            pl.BlockSpec((LANES, tm), full),
        ],
        out_specs=[
            pl.BlockSpec((tm, D_MODEL), row),
            pl.BlockSpec((tm, D_MODEL), row),
            pl.BlockSpec((8, tm), lambda i: (0, i)),
            pl.BlockSpec((tm, LANES), row),
            pl.BlockSpec((1, N_EXPERTS, LANES), lambda i: (i, 0, 0)),
        ],
        out_shape=[
            jax.ShapeDtypeStruct((n, D_MODEL), F32),
            jax.ShapeDtypeStruct((n, D_MODEL), F32),
            jax.ShapeDtypeStruct((8, n), F32),
            jax.ShapeDtypeStruct((n, LANES), F32),
            jax.ShapeDtypeStruct((nt, N_EXPERTS, LANES), F32),
        ],
        compiler_params=_cparams(("arbitrary",)),
        name="out_proj",
    )(da, sg, x2d, g1, sc2, sh2, w_out, lg, lb, wr, br)


ISSUE_UNROLL = 8


def _zero_fill(zs_ref, ze_ref, xb_ref, zero_ref, sem, n_blocks):
    zero_ref[...] = jnp.zeros_like(zero_ref)

    def per_expert(e, carry):
        def start(r, c):
            pltpu.make_async_copy(_row(zero_ref, 0), _row(xb_ref, r), sem).start()
            return c

        def wait(r, c):
            pltpu.make_async_copy(_row(zero_ref, 0), _row(xb_ref, r), sem).wait()
            return c

        lax.fori_loop(zs_ref[e], ze_ref[e], start, 0)
        lax.fori_loop(zs_ref[e], ze_ref[e], wait, 0)
        return carry

    lax.fori_loop(0, N_EXPERTS, per_expert, 0)

    def blk_copy(j):
        return pltpu.make_async_copy(zero_ref, xb_ref.at[pl.ds(pl.multiple_of(j * MOE_BLOCK, MOE_BLOCK), MOE_BLOCK)], sem)

    def blk_start(j, c):
        blk_copy(j).start()
        return c

    def blk_wait(j, c):
        blk_copy(j).wait()
        return c

    first_unused = ze_ref[N_EXPERTS - 1] // MOE_BLOCK
    lax.fori_loop(first_unused, n_blocks, blk_start, 0)
    lax.fori_loop(first_unused, n_blocks, blk_wait, 0)


def _dispatch_kernel(zs_ref, ze_ref, dest_ref, t_ref, xb_ref, zero_ref, sem):
    i = pl.program_id(0)
    tm = t_ref.shape[0]

    @pl.when(i == 0)
    def _():
        _zero_fill(zs_ref, ze_ref, xb_ref, zero_ref, sem, xb_ref.shape[0] // MOE_BLOCK)

    def start(n, c):
        pltpu.make_async_copy(_row(t_ref, n), _row(xb_ref, dest_ref[0, 0, n]), sem).start()
        pltpu.make_async_copy(_row(t_ref, n), _row(xb_ref, dest_ref[0, 0, tm + n]), sem).start(priority=1)
        return c

    lax.fori_loop(0, tm, start, 0, unroll=ISSUE_UNROLL)
    for _ in range(2):
        pltpu.make_async_copy(t_ref, xb_ref.at[pl.ds(0, tm)], sem).wait()


def _dispatch_call(zs, ze, dest3, t, p_rows):
    tm = TOK_TILE
    grid_spec = pltpu.PrefetchScalarGridSpec(
        num_scalar_prefetch=2,
        grid=(dest3.shape[0],),
        in_specs=[
            pl.BlockSpec((1, 1, 2 * tm), lambda i, zs, ze: (i, 0, 0), memory_space=pltpu.SMEM),
            pl.BlockSpec((tm, D_MODEL), lambda i, zs, ze: (i, 0)),
        ],
        out_specs=pl.BlockSpec(memory_space=pl.ANY),
        scratch_shapes=[pltpu.VMEM((MOE_BLOCK, D_MODEL), F32), pltpu.SemaphoreType.DMA(())],
    )
    return pl.pallas_call(
        _dispatch_kernel,
        grid_spec=grid_spec,
        out_shape=jax.ShapeDtypeStruct((p_rows, D_MODEL), F32),
        compiler_params=_cparams(("arbitrary",)),
        name="dispatch",
    )(zs, ze, dest3, t)


ITEM_SUB = 4
ITEM_ROWS = ITEM_SUB * MOE_BLOCK


SIDE_STEPS = 64


def _expert_kernel(ie_ref, ib_ref, ins_ref, iord_ref, inxt_ref, ni_ref, zs_ref, ze_ref, sidx_ref,
                   xb_ref, wg_ref, wu_ref, wd_ref, side_src, yb_ref, side_dst,
                   xbuf, ybuf, wf32, wgb, wub, wdb, stage, insem, outsem, wsem, rsem, bsem, *, side_scatter):
    i = pl.program_id(0)
    n_steps = pl.num_programs(0)
    ni = ni_ref[0]
    slot = i % 2
    stok = stage.shape[2]
    n_side_tok = SIDE_STEPS * stok

    def row_copy(q, s):
        k, j = divmod(q, stok)
        if side_scatter:
            return pltpu.make_async_copy(_row(stage.at[s, 0], j), _row(side_dst, sidx_ref[0, 0, q]), rsem)
        return pltpu.make_async_copy(_row(side_src, sidx_ref[0, 0, q]), _row(stage.at[s, k], j), rsem)

    def rows_wait(s):
        for _ in range(2):
            pltpu.make_async_copy(side_dst.at[pl.ds(0, stok)], stage.at[s, 0], rsem).wait()

    def block_copies(step, s):
        if side_scatter:
            start = pl.multiple_of(n_side_tok + step * stok, 8)
            return [pltpu.make_async_copy(side_src.at[pl.ds(start, stok)], stage.at[s, 0], bsem.at[s])]
        return [pltpu.make_async_copy(stage.at[s, k], side_dst.at[pl.ds(pl.multiple_of(k * n_side_tok + step * stok, 8), stok)],
                                      bsem.at[s]) for k in range(2)]

    def side_pre():
        if side_scatter:
            @pl.when(i == 0)
            def _():
                _zero_fill(zs_ref, ze_ref, side_dst, ybuf.at[1, pl.ds(0, MOE_BLOCK)], rsem, side_dst.shape[0] // MOE_BLOCK)
                for c in block_copies(0, 0):
                    c.start()

            @pl.when((i >= 1) & (i <= SIDE_STEPS))
            def _():
                rows_wait(1 - slot)

            @pl.when(i + 1 < SIDE_STEPS)
            def _():
                for c in block_copies(i + 1, 1 - slot):
                    c.start()

            @pl.when(i < SIDE_STEPS)
            def _():
                for c in block_copies(i, slot):
                    c.wait()
        else:
            @pl.when((i >= 2) & (i <= SIDE_STEPS + 1))
            def _():
                for c in block_copies(i - 2, slot):
                    c.wait()

            @pl.when((i >= 1) & (i <= SIDE_STEPS))
            def _():
                rows_wait(1 - slot)
                for c in block_copies(i - 1, 1 - slot):
                    c.start()

    def side_issue():
        for q in range(2 * stok):
            row_copy(q, slot).start()

    def w_copies(e, s):
        return [pltpu.make_async_copy(w.at[e], wf32.at[s, k], wsem.at[s]) for k, w in enumerate((wg_ref, wu_ref, wd_ref))]

    def rows_of(item, j):
        return pl.ds(pl.multiple_of((ib_ref[item] + j) * MOE_BLOCK, MOE_BLOCK), MOE_BLOCK)

    def in_copy(item, s, j):
        return pltpu.make_async_copy(xb_ref.at[rows_of(item, j)], xbuf.at[s, pl.ds(j * MOE_BLOCK, MOE_BLOCK)], insem.at[s])

    def out_copy(item, s, j):
        return pltpu.make_async_copy(ybuf.at[s, pl.ds(j * MOE_BLOCK, MOE_BLOCK)], yb_ref.at[rows_of(item, j)], outsem.at[s])

    def for_blocks(item, fn):
        for j in range(ITEM_SUB):
            @pl.when(j < ins_ref[item])
            def _():
                fn(j)

    @pl.when(i == 0)
    def _():
        for c in w_copies(ie_ref[0], 0):
            c.start()
        for_blocks(0, lambda j: in_copy(0, 0, j).start())

    side_pre()

    @pl.when(i < ni)
    def _():
        @pl.when(i >= 2)
        def _():
            for_blocks(i - 2, lambda j: out_copy(i - 2, slot, j).wait())

        @pl.when(i + 1 < ni)
        def _():
            for_blocks(i + 1, lambda j: in_copy(i + 1, 1 - slot, j).start())

        changed = (i == 0) | (ie_ref[i] != ie_ref[jnp.maximum(i - 1, 0)])

        @pl.when(changed)
        def _():
            ws = iord_ref[i] % 2
            for c in w_copies(ie_ref[i], ws):
                c.wait()

            @pl.when(inxt_ref[i] >= 0)
            def _():
                for c in w_copies(inxt_ref[i], 1 - ws):
                    c.start()

            wgb[...] = wf32[ws, 0].astype(BF16)
            wub[...] = wf32[ws, 1].astype(BF16)
            wdb[...] = wf32[ws, 2].astype(BF16)

        for_blocks(i, lambda j: in_copy(i, slot, j).wait())

        for ns in range(1, ITEM_SUB + 1):
            for with_side in (True, False):
                @pl.when((ins_ref[i] == ns) & ((i < SIDE_STEPS) == with_side))
                def _():
                    if with_side:
                        side_issue()
                    rows = pl.ds(0, ns * MOE_BLOCK)
                    x = xbuf[slot, rows].astype(BF16)
                    g = _dot(x, wgb[...])
                    u = _dot(x, wub[...])
                    hid = (g * (1.0 / (1.0 + jnp.exp(-g))) * u).astype(BF16)
                    ybuf[slot, rows] = _dot(hid, wdb[...])

        for_blocks(i, lambda j: out_copy(i, slot, j).start())

    @pl.when(i == n_steps - 1)
    def _():
        for back in (2, 1):
            k = ni - back

            @pl.when(k >= 0)
            def _():
                for_blocks(k, lambda j: out_copy(k, k % 2, j).wait())

        ybuf[0, pl.ds(0, MOE_BLOCK)] = jnp.zeros((MOE_BLOCK, D_MODEL), F32)
        last = jnp.maximum(ni - 1, 0)
        first_unused = ib_ref[last] + ins_ref[last]
        n_blocks = yb_ref.shape[0] // MOE_BLOCK

        def zero_copy(b):
            dst = yb_ref.at[pl.ds(pl.multiple_of(b * MOE_BLOCK, MOE_BLOCK), MOE_BLOCK)]
            return pltpu.make_async_copy(ybuf.at[0, pl.ds(0, MOE_BLOCK)], dst, outsem.at[0])

        def z_start(b, c):
            zero_copy(b).start()
            return c

        def z_wait(b, c):
            zero_copy(b).wait()
            return c

        lax.fori_loop(first_unused, n_blocks, z_start, 0)
        lax.fori_loop(first_unused, n_blocks, z_wait, 0)


def _expert_call(plan, zs, ze, side_idx, xb, wg, wu, wd, side_src, side_dst_rows, side_scatter, name):
    p_rows = plan["p_rows"]
    item_e, item_b, item_ns, item_ord, item_nxt, n_items = plan["items"]
    max_items = item_e.shape[0]
    side_steps, _, side_rows = side_idx.shape
    assert side_steps == SIDE_STEPS and max_items >= SIDE_STEPS + 2
    stok = side_rows // 2
    side_spec = pl.BlockSpec((1, 1, side_rows), lambda i, *_: (jnp.minimum(i, SIDE_STEPS - 1), 0, 0),
                             memory_space=pltpu.SMEM)
    grid_spec = pltpu.PrefetchScalarGridSpec(
        num_scalar_prefetch=8,
        grid=(max_items,),
        in_specs=[side_spec] + [pl.BlockSpec(memory_space=pl.ANY)] * 5,
        out_specs=[pl.BlockSpec(memory_space=pl.ANY)] * 2,
        scratch_shapes=[
            pltpu.VMEM((2, ITEM_ROWS, D_MODEL), F32),
            pltpu.VMEM((2, ITEM_ROWS, D_MODEL), F32),
            pltpu.VMEM((2, 3, D_MODEL, D_MODEL), F32),
            pltpu.VMEM((D_MODEL, D_MODEL), BF16),
            pltpu.VMEM((D_MODEL, D_MODEL), BF16),
            pltpu.VMEM((D_MODEL, D_MODEL), BF16),
            pltpu.VMEM((2, 1 if side_scatter else 2, stok, D_MODEL), F32),
            pltpu.SemaphoreType.DMA((2,)),
            pltpu.SemaphoreType.DMA((2,)),
            pltpu.SemaphoreType.DMA((2,)),
            pltpu.SemaphoreType.DMA(()),
            pltpu.SemaphoreType.DMA((2,)),
        ],
    )
    return pl.pallas_call(
        functools.partial(_expert_kernel, side_scatter=side_scatter),
        grid_spec=grid_spec,
        out_shape=[jax.ShapeDtypeStruct((p_rows, D_MODEL), F32), jax.ShapeDtypeStruct((side_dst_rows, D_MODEL), F32)],
        compiler_params=_cparams(("arbitrary",)),
        name=name,
    )(item_e, item_b, item_ns, item_ord, item_nxt, n_items, zs, ze, side_idx, xb, wg, wu, wd, side_src)


def _combine_kernel(dnxt_ref, ya0_ref, ya1_ref, yb_ref, x1_ref, wts_ref, g2_ref, lg_ref, lb_ref, o_ref, buf, sem, *, nt_a):
    i = pl.program_id(0)
    tm = x1_ref.shape[0]
    slot = i % 2
    w = wts_ref[...]

    def finish(y0, y1):
        f = w[:, 0:1] * y0 + w[:, 64:65] * y1
        o_ref[...] = _layer_norm(DEEPNORM_ALPHA * x1_ref[...] + g2_ref[0] * f, lg_ref[...], lb_ref[...])

    @pl.when((i + 1 >= nt_a) & (i + 1 < pl.num_programs(0)))
    def _():
        s = 1 - slot

        def start(n, c):
            pltpu.make_async_copy(_row(yb_ref, dnxt_ref[0, 0, n]), _row(buf.at[s, 0], n), sem.at[s]).start()
            pltpu.make_async_copy(_row(yb_ref, dnxt_ref[0, 0, tm + n]), _row(buf.at[s, 1], n), sem.at[s]).start(priority=1)
            return c

        lax.fori_loop(0, tm, start, 0, unroll=ISSUE_UNROLL)

    @pl.when(i < nt_a)
    def _():
        finish(ya0_ref[...], ya1_ref[...])

    @pl.when(i >= nt_a)
    def _():
        for k in range(2):
            pltpu.make_async_copy(yb_ref.at[pl.ds(0, tm)], buf.at[slot, k], sem.at[slot]).wait()
        finish(buf[slot, 0], buf[slot, 1])


def _combine_call(ysl_a, dest3_b, yb_b, x1, wts, g2, lg, lb, tiles_per_batch):
    n = x1.shape[0]
    tm = TOK_TILE
    tpb = tiles_per_batch
    row = lambda i: (i, 0)
    full = lambda i: (0, 0)
    nt = n // tm
    nt_b = dest3_b.shape[0]
    nt_a = nt - nt_b
    return pl.pallas_call(
        functools.partial(_combine_kernel, nt_a=nt_a),
        grid=(nt,),
        in_specs=[
            pl.BlockSpec((1, 1, 2 * tm), lambda i: (jnp.clip(i + 1 - nt_a, 0, nt_b - 1), 0, 0), memory_space=pltpu.SMEM),
            pl.BlockSpec((tm, D_MODEL), lambda i: (jnp.minimum(i, nt_a - 1), 0)),
            pl.BlockSpec((tm, D_MODEL), lambda i: (nt_a + jnp.minimum(i, nt_a - 1), 0)),
            pl.BlockSpec(memory_space=pl.ANY),
            pl.BlockSpec((tm, D_MODEL), row),
            pl.BlockSpec((tm, LANES), row),
            pl.BlockSpec((1, 1, D_MODEL), lambda i: (i // tpb, 0, 0)),
            pl.BlockSpec((1, D_MODEL), full),
            pl.BlockSpec((1, D_MODEL), full),
        ],
        out_specs=pl.BlockSpec((tm, D_MODEL), row),
        out_shape=jax.ShapeDtypeStruct((n, D_MODEL), F32),
        scratch_shapes=[pltpu.VMEM((2, 2, tm, D_MODEL), F32), pltpu.SemaphoreType.DMA((2,))],
        compiler_params=_cparams(("arbitrary",)),
        name="combine",
    )(dest3_b, ysl_a, ysl_a, yb_b, x1, wts, g2, lg, lb)


def _head_interleave(w, lo):
    blk = w[:, lo:lo + 2 * QK_COLS].reshape(D_MODEL, 2, DA_HEADS, DA_HEAD_DIM)
    return blk.transpose(0, 2, 1, 3).reshape(D_MODEL, 2 * QK_COLS)


def _cumsum_small(x):
    idx = jnp.arange(x.shape[0], dtype=jnp.int32)
    return jnp.sum(jnp.where(idx[None, :] <= idx[:, None], x[None, :], 0), -1)


def _lookup(table, idx):
    return jnp.sum(jnp.where(idx[:, None] == jnp.arange(table.shape[0], dtype=jnp.int32)[None, :], table[None, :], 0), -1)


def _rope_tables(seq):
    rows_n = seq // GRID_W
    rows = jnp.repeat(jnp.arange(rows_n, dtype=F32), GRID_W)
    cols = jnp.tile(jnp.arange(GRID_W, dtype=F32), rows_n)
    half = DA_HEAD_DIM // 4
    inv = ROPE_THETA ** (-jnp.arange(half, dtype=F32) / half)
    ang_r = rows[:, None] * inv[None, :]
    ang_c = cols[:, None] * inv[None, :]
    cos64 = jnp.concatenate([jnp.cos(ang_r), jnp.cos(ang_r), jnp.cos(ang_c), jnp.cos(ang_c)], -1)
    sin64 = jnp.concatenate([-jnp.sin(ang_r), jnp.sin(ang_r), -jnp.sin(ang_c), jnp.sin(ang_c)], -1)
    return jnp.tile(cos64, (1, 2)), jnp.tile(sin64, (1, 2))


def _moe_plan(route, tcnt):
    tm = TOK_TILE
    nt = tcnt.shape[0]
    n = nt * tm
    cnt_te = tcnt[:, :, 0].astype(jnp.int32)
    counts = jnp.sum(cnt_te, 0)
    padded = (counts + MOE_BLOCK - 1) // MOE_BLOCK * MOE_BLOCK
    pad_end = _cumsum_small(padded)
    pad_start = pad_end - padded
    tix = jnp.arange(nt, dtype=jnp.int32)
    tile_prefix = jnp.sum(jnp.where(tix[None, :, None] < tix[:, None, None], cnt_te[None], 0), 1)
    base = pad_start[None, :] + tile_prefix
    ridx = route[:4].astype(jnp.int32).reshape(4, nt, tm)
    ex = jnp.arange(N_EXPERTS, dtype=jnp.int32)

    def slot_dest(eid, rank):
        return jnp.sum(jnp.where(eid[None] == ex[:, None, None], base.T[:, :, None], 0), 0) + rank

    dest = jnp.concatenate([slot_dest(ridx[0], ridx[2]), slot_dest(ridx[1], ridx[3])], -1)
    n_blocks = (n * 2) // MOE_BLOCK + N_EXPERTS
    nb_e = padded // MOE_BLOCK
    items_e = (nb_e + ITEM_SUB - 1) // ITEM_SUB
    item_end = _cumsum_small(items_e)
    max_items = (n_blocks + (ITEM_SUB - 1) * N_EXPERTS) // ITEM_SUB
    it = jnp.arange(max_items, dtype=jnp.int32)
    item_e = jnp.minimum(jnp.sum((it[:, None] >= item_end[None, :]).astype(jnp.int32), -1), N_EXPERTS - 1)
    item_j = it - _lookup(item_end - items_e, item_e)
    item_b = _lookup(pad_start // MOE_BLOCK, item_e) + ITEM_SUB * item_j
    item_ns = jnp.clip(_lookup(nb_e, item_e) - ITEM_SUB * item_j, 0, ITEM_SUB)
    nonempty = (items_e > 0).astype(jnp.int32)
    ord_e = _cumsum_small(nonempty) - 1
    later = (ex[None, :] > ex[:, None]) & (nonempty[None, :] > 0)
    nxt_e = jnp.min(jnp.where(later, ex[None, :], N_EXPERTS), -1)
    nxt_e = jnp.where(nxt_e == N_EXPERTS, -1, nxt_e)
    items = (item_e, item_b, item_ns, _lookup(ord_e, item_e), _lookup(nxt_e, item_e), item_end[-1:])
    return {
        "dest": dest,
        "zs": (pad_start + counts).astype(jnp.int32),
        "ze": pad_end.astype(jnp.int32),
        "items": tuple(a.astype(jnp.int32) for a in items),
        "p_rows": n_blocks * MOE_BLOCK,
    }


def _side_list(dest):
    nt = dest.shape[0]
    tm = TOK_TILE
    by_slot = dest.reshape(nt, 2, tm).transpose(1, 0, 2).reshape(2, SIDE_STEPS, -1)
    return by_slot.transpose(1, 0, 2).reshape(SIDE_STEPS, 1, -1).astype(jnp.int32)


def kernel(x, c, ctx, c_ctx, w_mod, b_mod, w_in, lam_q1, lam_k1, lam_q2, lam_k2, subln_g, sg_ln_g, sg_ln_b, sg_w, sg_b, w_out, ln1_g, ln1_b, router_group_w, router_group_b, router_expert_w, router_expert_b, exp_w_gate, exp_w_up, exp_w_down, ln2_g, ln2_b):
    b, l, d = x.shape
    n = b * l
    tm = TOK_TILE
    nt = n // tm

    cc = jnp.zeros((b + 8, d), F32).at[:b].set(c).at[b].set(c_ctx)
    mod = _mod_call(cc, w_mod[0], b_mod[0][None, :])
    sh1, sc1, g1, sh2, sc2, g2 = [mod[:b, j * d:(j + 1) * d].reshape(b, 1, d) for j in range(6)]
    csh1 = mod[b:b + 1, 0:d]
    csc1 = mod[b:b + 1, d:2 * d]

    wi = w_in[0]
    w_all = jnp.concatenate([_head_interleave(wi, 0), _head_interleave(wi, KV_LO), wi[:, 2 * KV_LO:]], -1).astype(BF16)
    kc, vc = _ctx_call(ctx, csc1, csh1, w_all[:, KV_LO:KV_HI])

    cos_t, sin_t = _rope_tables(l)
    sbias = jnp.repeat(sg_b[0].T, LANES, axis=1)
    q, k, v, sg = _in_call(x, sc1, sh1, w_all, cos_t, sin_t, sg_ln_g[0][None, :], sg_ln_b[0][None, :],
                           sg_w[0].astype(BF16), sbias)

    lamv = jnp.stack([lam_q1[0], lam_k1[0], lam_q2[0], lam_k2[0]]).astype(F32)
    da = _attn_call(lamv, q, k, kc, v, vc, subln_g[0][None, :])

    e_lo = MOE_EXPERTS_PER_GROUP
    wr = jnp.zeros((LANES, d), F32).at[:MOE_GROUPS].set(router_group_w[0].T).at[e_lo:e_lo + N_EXPERTS].set(router_expert_w[0].T)
    br = jnp.zeros((LANES,), F32).at[:MOE_GROUPS].set(router_group_b[0]).at[e_lo:e_lo + N_EXPERTS].set(router_expert_b[0])
    br = jnp.broadcast_to(br[:, None], (LANES, tm))
    wr_hi = wr.astype(BF16)
    wr_split = jnp.concatenate([wr_hi, (wr - wr_hi.astype(F32)).astype(BF16)], 0)
    x1, t, route, wts, tcnt = _out_call(da.reshape(n, DA_WIDTH), sg.reshape(n, SG_WIDTH), x.reshape(n, d), g1, sc2, sh2,
                                        w_out[0].astype(BF16), ln1_g[0][None, :], ln1_b[0][None, :], wr_split, br, l // tm)

    nt_h = nt // 2
    n_h = nt_h * tm
    plan_a = _moe_plan(route[:, :n_h], tcnt[:nt_h])
    plan_b = _moe_plan(route[:, n_h:], tcnt[nt_h:])
    p_rows = plan_a["p_rows"]
    weights = (exp_w_gate[0], exp_w_up[0], exp_w_down[0])
    no_pad = jnp.zeros((N_EXPERTS,), jnp.int32)

    xb_a = _dispatch_call(plan_a["zs"], plan_a["ze"], plan_a["dest"].reshape(nt_h, 1, 2 * tm), t, p_rows)
    yb_a, xb_b = _expert_call(plan_a, plan_b["zs"], plan_b["ze"], _side_list(plan_b["dest"]), xb_a, *weights, t,
                              p_rows, True, "experts_a")
    yb_b, ysl_a = _expert_call(plan_b, no_pad, no_pad, _side_list(plan_a["dest"]), xb_b, *weights, yb_a,
                               2 * n_h, False, "experts_b")
    out = _combine_call(ysl_a, plan_b["dest"].reshape(nt_h, 1, 2 * tm), yb_b, x1, wts, g2,
                        ln2_g[0][None, :], ln2_b[0][None, :], l // tm)
    return out.reshape(b, l, d)
```

```python
import functools
import math

import jax
import jax.numpy as jnp
from jax import lax
from jax.experimental import pallas as pl
from jax.experimental.pallas import tpu as pltpu

F32 = jnp.float32
BF16 = jnp.bfloat16

D_MODEL = 1024
GRID_W = 64
DA_HEAD_DIM = 64
DA_V_DIM = 128
DA_WIDTH = 512
DA_HEADS = 4
QK_COLS = 256
SG_CHUNK = 128
SG_WIDTH = 512
SG_GROUPS = 4
KV_LO = 512
KV_HI = 1536
IN_COLS = 2560
ROPE_THETA = 10000.0
MOE_GROUPS = 4
MOE_EXPERTS_PER_GROUP = 8
N_EXPERTS = 32
MOE_BLOCK = 128
EPS = 1e-5
DEPTH = 1
DEEPNORM_ALPHA = (2.0 * DEPTH) ** 0.25
LAM_INIT = 0.8 - 0.6 * math.exp(-0.3 * 0)
Q_SCALE = DA_HEAD_DIM ** -0.5
LOG2E = math.log2(math.e)
SQRT_HALF = math.sqrt(0.5)

LANES = 128
TOK_TILE = 512
ATTN_TQ = 2048
ATTN_SUB = 256
VMEM_LIMIT = 56 * 1024 * 1024


def _cparams(sem):
    return pltpu.CompilerParams(dimension_semantics=sem, vmem_limit_bytes=VMEM_LIMIT)


def _dot(a, b):
    return jnp.dot(a, b, preferred_element_type=F32)


def _dot_hi(a, b):
    return jnp.dot(a, b, preferred_element_type=F32, precision=lax.Precision.HIGHEST)


def _layer_norm(y, g, b):
    mu = jnp.mean(y, -1, keepdims=True)
    yc = y - mu
    var = jnp.mean(yc * yc, -1, keepdims=True)
    return yc * lax.rsqrt(var + EPS) * g + b


def _row(ref, n):
    return ref.at[pl.ds(n, 1)]


def _mod_kernel(c_ref, w_ref, b_ref, o_ref):
    c = c_ref[...]
    s = c * (1.0 / (1.0 + jnp.exp(-c)))
    o_ref[...] = _dot_hi(s, w_ref[...]) + b_ref[...]


def _mod_call(cc, w_mod, b_mod):
    rows = cc.shape[0]
    n_out = w_mod.shape[1]
    bn = 1024
    return pl.pallas_call(
        _mod_kernel,
        grid=(n_out // bn,),
        in_specs=[
            pl.BlockSpec((rows, D_MODEL), lambda j: (0, 0)),
            pl.BlockSpec((D_MODEL, bn), lambda j: (0, j)),
            pl.BlockSpec((1, bn), lambda j: (0, j)),
        ],
        out_specs=pl.BlockSpec((rows, bn), lambda j: (0, j)),
        out_shape=jax.ShapeDtypeStruct((rows, n_out), F32),
        compiler_params=_cparams(("arbitrary",)),
        name="mod",
    )(cc, w_mod, b_mod)


def _ctx_kernel(c_ref, sc_ref, sh_ref, w_ref, kc_ref, vc_ref):
    h = (c_ref[0] * (1.0 + sc_ref[...]) + sh_ref[...]).astype(BF16)
    p = _dot(h, w_ref[...])
    kc_ref[0] = p[:, :2 * QK_COLS].astype(BF16)
    vc_ref[0] = p[:, 2 * QK_COLS:].astype(BF16)


def _ctx_call(ctx, csc1, csh1, w_kv):
    b, cl, _ = ctx.shape
    return pl.pallas_call(
        _ctx_kernel,
        grid=(b,),
        in_specs=[
            pl.BlockSpec((1, cl, D_MODEL), lambda i: (i, 0, 0)),
            pl.BlockSpec((1, D_MODEL), lambda i: (0, 0)),
            pl.BlockSpec((1, D_MODEL), lambda i: (0, 0)),
            pl.BlockSpec((D_MODEL, 2 * QK_COLS + DA_WIDTH), lambda i: (0, 0)),
        ],
        out_specs=[
            pl.BlockSpec((1, cl, 2 * QK_COLS), lambda i: (i, 0, 0)),
            pl.BlockSpec((1, cl, DA_WIDTH), lambda i: (i, 0, 0)),
        ],
        out_shape=[
            jax.ShapeDtypeStruct((b, cl, 2 * QK_COLS), BF16),
            jax.ShapeDtypeStruct((b, cl, DA_WIDTH), BF16),
        ],
        compiler_params=_cparams(("arbitrary",)),
        name="ctx_kv",
    )(ctx, csc1, csh1, w_kv)


def _in_kernel(x_ref, sc_ref, sh_ref, w_ref, cos_ref, sin_ref, lng_ref, lnb_ref,
               sw_ref, sb_ref, q_ref, k_ref, v_ref, sg_ref):
    tm = x_ref.shape[1]
    h = (x_ref[0] * (1.0 + sc_ref[0]) + sh_ref[0]).astype(BF16)
    p = _dot(h, w_ref[...])

    cos = cos_ref[...]
    sin = sin_ref[...]
    lane = lax.broadcasted_iota(jnp.int32, (tm, LANES), 1)
    first = (lane % 32) < 16

    def rope(t):
        partner = jnp.where(first, pltpu.roll(t, LANES - 16, 1), pltpu.roll(t, 16, 1))
        return t * cos + partner * sin

    for c in range(4):
        cs = slice(c * LANES, (c + 1) * LANES)
        q_ref[0, :, cs] = rope(p[:, cs] * (Q_SCALE * LOG2E)).astype(BF16)
        k_ref[0, :, cs] = rope(p[:, KV_LO + c * LANES:KV_LO + (c + 1) * LANES]).astype(BF16)
    v_ref[0] = p[:, 2 * KV_LO:KV_HI].astype(BF16)

    z = p[:, KV_HI:]
    gz = 0.5 * z * (1.0 + lax.erf(z * SQRT_HALF))
    u = gz[:, :SG_WIDTH]
    vn = _layer_norm(gz[:, SG_WIDTH:], lng_ref[...], lnb_ref[...]).astype(BF16)
    for c in range(tm // SG_CHUNK):
        rs = slice(c * SG_CHUNK, (c + 1) * SG_CHUNK)
        for g in range(SG_GROUPS):
            cs = slice(g * LANES, (g + 1) * LANES)
            s = _dot(sw_ref[g], vn[rs, cs]) + sb_ref[:, cs]
            sg_ref[0, rs, cs] = (u[rs, cs] * s).astype(BF16)


def _in_call(x, sc1, sh1, w_in, cos_t, sin_t, lng, lnb, sw, sbias):
    b, l, _ = x.shape
    tm = TOK_TILE
    full = lambda bi, i: (0, 0)
    return pl.pallas_call(
        _in_kernel,
        grid=(b, l // tm),
        in_specs=[
            pl.BlockSpec((1, tm, D_MODEL), lambda bi, i: (bi, i, 0)),
            pl.BlockSpec((1, 1, D_MODEL), lambda bi, i: (bi, 0, 0)),
            pl.BlockSpec((1, 1, D_MODEL), lambda bi, i: (bi, 0, 0)),
            pl.BlockSpec((D_MODEL, IN_COLS), full),
            pl.BlockSpec((tm, LANES), lambda bi, i: (i, 0)),
            pl.BlockSpec((tm, LANES), lambda bi, i: (i, 0)),
            pl.BlockSpec((1, SG_WIDTH), full),
            pl.BlockSpec((1, SG_WIDTH), full),
            pl.BlockSpec((SG_GROUPS, SG_CHUNK, SG_CHUNK), lambda bi, i: (0, 0, 0)),
            pl.BlockSpec((SG_CHUNK, SG_WIDTH), full),
        ],
        out_specs=[pl.BlockSpec((1, tm, 512), lambda bi, i: (bi, i, 0))] * 4,
        out_shape=[jax.ShapeDtypeStruct((b, l, 512), BF16)] * 4,
        compiler_params=_cparams(("arbitrary", "arbitrary")),
        name="in_proj",
    )(x, sc1, sh1, w_in, cos_t, sin_t, lng, lnb, sw, sbias)


def _attn_kernel(lam_ref, q_ref, k_ref, kc_ref, v_ref, vc_ref, g_ref, o_ref, kall, vall):
    lv = lam_ref[...]
    lam = (jnp.exp(jnp.sum(lv[0:1] * lv[1:2], -1, keepdims=True))
           - jnp.exp(jnp.sum(lv[2:3] * lv[3:4], -1, keepdims=True)) + LAM_INIT)
    l = k_ref.shape[1]

    @pl.when(pl.program_id(2) == 0)
    def _():
        kall[:l, :] = k_ref[0]
        kall[l:, :] = kc_ref[0]
        vall[:l, :DA_V_DIM] = v_ref[0]
        vall[l:, :DA_V_DIM] = vc_ref[0]
        vall[:, DA_V_DIM:] = jnp.ones((vall.shape[0], DA_V_DIM), BF16)

    nt = (((1,), (1,)), ((), ()))

    def branch(qm):
        s = lax.dot_general(qm, kall[...], nt, preferred_element_type=F32)
        m = jnp.max(s, -1, keepdims=True)
        e = jnp.exp2((s - m).astype(BF16))
        oe = _dot(e, vall[...])
        return oe[:, :DA_V_DIM] / oe[:, DA_V_DIM:DA_V_DIM + 1]

    for r in range(q_ref.shape[1] // ATTN_SUB):
        rs = pl.ds(r * ATTN_SUB, ATTN_SUB)
        q = q_ref[0, rs, :]
        lane = lax.broadcasted_iota(jnp.int32, q.shape, 1)
        zero = jnp.zeros_like(q)
        o = branch(jnp.where(lane < DA_HEAD_DIM, q, zero)) - lam * branch(jnp.where(lane >= DA_HEAD_DIM, q, zero))
        of = o * lax.rsqrt(jnp.mean(o * o, -1, keepdims=True) + EPS) * g_ref[...]
        o_ref[0, rs, :] = (of * (1.0 - LAM_INIT)).astype(BF16)


def _attn_call(lamv, q, k, kc, v, vc, subln_g):
    b, l, _ = q.shape
    cl = kc.shape[1]
    tq = ATTN_TQ
    return pl.pallas_call(
        _attn_kernel,
        grid=(b, DA_HEADS, l // tq),
        in_specs=[
            pl.BlockSpec((4, DA_HEAD_DIM), lambda bi, h, i: (0, 0)),
            pl.BlockSpec((1, tq, LANES), lambda bi, h, i: (bi, i, h)),
            pl.BlockSpec((1, l, LANES), lambda bi, h, i: (bi, 0, h)),
            pl.BlockSpec((1, cl, LANES), lambda bi, h, i: (bi, 0, h)),
            pl.BlockSpec((1, l, LANES), lambda bi, h, i: (bi, 0, h)),
            pl.BlockSpec((1, cl, LANES), lambda bi, h, i: (bi, 0, h)),
            pl.BlockSpec((1, DA_V_DIM), lambda bi, h, i: (0, 0)),
        ],
        out_specs=pl.BlockSpec((1, tq, LANES), lambda bi, h, i: (bi, i, h)),
        out_shape=jax.ShapeDtypeStruct((b, l, DA_WIDTH), BF16),
        scratch_shapes=[pltpu.VMEM((l + cl, LANES), BF16), pltpu.VMEM((l + cl, 2 * DA_V_DIM), BF16)],
        compiler_params=_cparams(("arbitrary", "arbitrary", "arbitrary")),
        name="attn",
    )(lamv, q, k, kc, v, vc, subln_g)


def _out_kernel(da_ref, sg_ref, x_ref, g1_ref, sc_ref, sh_ref, w_ref, lg_ref, lb_ref,
                wr_ref, br_ref, x1_ref, t_ref, route_ref, wts_ref, cnt_ref):
    tm = x_ref.shape[0]
    y = _dot(da_ref[...], w_ref[:DA_WIDTH, :]) + _dot(sg_ref[...], w_ref[DA_WIDTH:, :])
    x1 = _layer_norm(DEEPNORM_ALPHA * x_ref[...] + g1_ref[0] * y, lg_ref[...], lb_ref[...])
    x1_ref[...] = x1
    t = x1 * (1.0 + sc_ref[0]) + sh_ref[0]
    t_ref[...] = t

    t_hi = t.astype(BF16)
    t_lo = (t - t_hi.astype(F32)).astype(BF16)
    nt_dims = (((1,), (1,)), ((), ()))
    hw = lax.dot_general(wr_ref[...], t_hi, nt_dims, preferred_element_type=F32)
    lt = (hw[:LANES] + hw[LANES:] + lax.dot_general(wr_ref[:LANES, :], t_lo, nt_dims, preferred_element_type=F32)
          + br_ref[...])
    epg = MOE_EXPERTS_PER_GROUP
    row8 = lax.broadcasted_iota(jnp.int32, (epg, tm), 0).astype(F32)
    ninf = jnp.float32(-jnp.inf)
    big = jnp.float32(epg)
    gmask = row8 < MOE_GROUPS
    gl = jnp.where(gmask, lt[0:epg], ninf)
    gmax = jnp.max(gl, 0, keepdims=True)
    gsel = jnp.min(jnp.where(gl == gmax, row8, big), 0, keepdims=True)
    gsum = jnp.sum(jnp.where(gmask, jnp.exp(gl - gmax), 0.0), 0, keepdims=True)
    gw = 1.0 / gsum
    el = lt[epg:2 * epg]
    for g in range(1, MOE_GROUPS):
        el = jnp.where(gsel == g, lt[epg * (g + 1):epg * (g + 2)], el)
    v1 = jnp.max(el, 0, keepdims=True)
    i1 = jnp.min(jnp.where(el == v1, row8, big), 0, keepdims=True)
    el2 = jnp.where(row8 == i1, ninf, el)
    v2 = jnp.max(el2, 0, keepdims=True)
    i2 = jnp.min(jnp.where(el2 == v2, row8, big), 0, keepdims=True)
    e = jnp.exp(v2 - v1)
    w1 = gw / (1.0 + e)
    w2 = gw * e / (1.0 + e)
    e1 = gsel * epg + i1
    e2 = gsel * epg + i2

    rowe = lax.broadcasted_iota(jnp.int32, (N_EXPERTS, tm), 0).astype(F32)
    oh1 = rowe == e1
    oh2 = rowe == e2
    oh = jnp.where(oh1 | oh2, 1.0, 0.0)
    r_i = lax.broadcasted_iota(jnp.int32, (tm, tm), 0)
    c_i = lax.broadcasted_iota(jnp.int32, (tm, tm), 1)
    upper = jnp.where(r_i < c_i, 1.0, 0.0).astype(BF16)
    pref = _dot(oh.astype(BF16), upper)
    r1 = jnp.sum(jnp.where(oh1, pref, 0.0), 0, keepdims=True)
    r2 = jnp.sum(jnp.where(oh2, pref, 0.0), 0, keepdims=True)
    cnt_ref[0] = jnp.broadcast_to(jnp.sum(oh, 1, keepdims=True), (N_EXPERTS, LANES))

    route_ref[...] = jnp.where(row8 == 0, e1, jnp.where(row8 == 1, e2, jnp.where(row8 == 2, r1, jnp.where(row8 == 3, r2, 0.0))))
    rowl = lax.broadcasted_iota(jnp.int32, (LANES, tm), 0)
    wts_ref[...] = jnp.where(rowl < 64, w1, w2).T


def _out_call(da, sg, x2d, g1, sc2, sh2, w_out, lg, lb, wr, br, tiles_per_batch):
    n = x2d.shape[0]
    tm = TOK_TILE
    nt = n // tm
    tpb = tiles_per_batch
    row = lambda i: (i, 0)
    full = lambda i: (0, 0)
    per_b = lambda i: (i // tpb, 0, 0)
    return pl.pallas_call(
        _out_kernel,
        grid=(nt,),
        in_specs=[
            pl.BlockSpec((tm, DA_WIDTH), row),
            pl.BlockSpec((tm, SG_WIDTH), row),
            pl.BlockSpec((tm, D_MODEL), row),
            pl.BlockSpec((1, 1, D_MODEL), per_b),
            pl.BlockSpec((1, 1, D_MODEL), per_b),
            pl.BlockSpec((1, 1, D_MODEL), per_b),
            pl.BlockSpec((D_MODEL, D_MODEL), full),
            pl.BlockSpec((1, D_MODEL), full),
            pl.BlockSpec((1, D_MODEL), full),
            pl.BlockSpec((2 * LANES, D_MODEL), full),
            pl.BlockSpec((LANES, tm), full),
        ],
        out_specs=[
            pl.BlockSpec((tm, D_MODEL), row),
            pl.BlockSpec((tm, D_MODEL), row),
            pl.BlockSpec((8, tm), lambda i: (0, i)),
            pl.BlockSpec((tm, LANES), row),
            pl.BlockSpec((1, N_EXPERTS, LANES), lambda i: (i, 0, 0)),
        ],
        out_shape=[
            jax.ShapeDtypeStruct((n, D_MODEL), F32),
            jax.ShapeDtypeStruct((n, D_MODEL), F32),
            jax.ShapeDtypeStruct((8, n), F32),
            jax.ShapeDtypeStruct((n, LANES), F32),
            jax.ShapeDtypeStruct((nt, N_EXPERTS, LANES), F32),
        ],
        compiler_params=_cparams(("arbitrary",)),
        name="out_proj",
    )(da, sg, x2d, g1, sc2, sh2, w_out, lg, lb, wr, br)


ISSUE_UNROLL = 8


def _zero_fill(zs_ref, ze_ref, xb_ref, zero_ref, sem, n_blocks):
    zero_ref[...] = jnp.zeros_like(zero_ref)

    def per_expert(e, carry):
        def start(r, c):
            pltpu.make_async_copy(_row(zero_ref, 0), _row(xb_ref, r), sem).start()
            return c

        def wait(r, c):
            pltpu.make_async_copy(_row(zero_ref, 0), _row(xb_ref, r), sem).wait()
            return c

        lax.fori_loop(zs_ref[e], ze_ref[e], start, 0)
        lax.fori_loop(zs_ref[e], ze_ref[e], wait, 0)
        return carry

    lax.fori_loop(0, N_EXPERTS, per_expert, 0)

    def blk_copy(j):
        return pltpu.make_async_copy(zero_ref, xb_ref.at[pl.ds(pl.multiple_of(j * MOE_BLOCK, MOE_BLOCK), MOE_BLOCK)], sem)

    def blk_start(j, c):
        blk_copy(j).start()
        return c

    def blk_wait(j, c):
        blk_copy(j).wait()
        return c

    first_unused = ze_ref[N_EXPERTS - 1] // MOE_BLOCK
    lax.fori_loop(first_unused, n_blocks, blk_start, 0)
    lax.fori_loop(first_unused, n_blocks, blk_wait, 0)


def _dispatch_kernel(zs_ref, ze_ref, dest_ref, t_ref, xb_ref, zero_ref, sem):
    i = pl.program_id(0)
    tm = t_ref.shape[0]

    @pl.when(i == 0)
    def _():
        _zero_fill(zs_ref, ze_ref, xb_ref, zero_ref, sem, xb_ref.shape[0] // MOE_BLOCK)

    def start(n, c):
        pltpu.make_async_copy(_row(t_ref, n), _row(xb_ref, dest_ref[0, 0, n]), sem).start()
        pltpu.make_async_copy(_row(t_ref, n), _row(xb_ref, dest_ref[0, 0, tm + n]), sem).start(priority=1)
        return c

    lax.fori_loop(0, tm, start, 0, unroll=ISSUE_UNROLL)
    for _ in range(2):
        pltpu.make_async_copy(t_ref, xb_ref.at[pl.ds(0, tm)], sem).wait()


def _dispatch_call(zs, ze, dest3, t, p_rows):
    tm = TOK_TILE
    grid_spec = pltpu.PrefetchScalarGridSpec(
        num_scalar_prefetch=2,
        grid=(dest3.shape[0],),
        in_specs=[
            pl.BlockSpec((1, 1, 2 * tm), lambda i, zs, ze: (i, 0, 0), memory_space=pltpu.SMEM),
            pl.BlockSpec((tm, D_MODEL), lambda i, zs, ze: (i, 0)),
        ],
        out_specs=pl.BlockSpec(memory_space=pl.ANY),
        scratch_shapes=[pltpu.VMEM((MOE_BLOCK, D_MODEL), F32), pltpu.SemaphoreType.DMA(())],
    )
    return pl.pallas_call(
        _dispatch_kernel,
        grid_spec=grid_spec,
        out_shape=jax.ShapeDtypeStruct((p_rows, D_MODEL), F32),
        compiler_params=_cparams(("arbitrary",)),
        name="dispatch",
    )(zs, ze, dest3, t)


ITEM_SUB = 4
ITEM_ROWS = ITEM_SUB * MOE_BLOCK


SIDE_STEPS = 64


def _expert_kernel(ie_ref, ib_ref, ins_ref, iord_ref, inxt_ref, ni_ref, zs_ref, ze_ref, sidx_ref,
                   xb_ref, wg_ref, wu_ref, wd_ref, side_src, yb_ref, side_dst,
                   xbuf, ybuf, wf32, wgb, wub, wdb, stage, insem, outsem, wsem, rsem, bsem, *, side_scatter):
    i = pl.program_id(0)
    n_steps = pl.num_programs(0)
    ni = ni_ref[0]
    slot = i % 2
    stok = stage.shape[2]
    n_side_tok = SIDE_STEPS * stok

    def row_copy(q, s):
        k, j = divmod(q, stok)
        if side_scatter:
            return pltpu.make_async_copy(_row(stage.at[s, 0], j), _row(side_dst, sidx_ref[0, 0, q]), rsem)
        return pltpu.make_async_copy(_row(side_src, sidx_ref[0, 0, q]), _row(stage.at[s, k], j), rsem)

    def rows_wait(s):
        for _ in range(2):
            pltpu.make_async_copy(side_dst.at[pl.ds(0, stok)], stage.at[s, 0], rsem).wait()

    def block_copies(step, s):
        if side_scatter:
            start = pl.multiple_of(n_side_tok + step * stok, 8)
            return [pltpu.make_async_copy(side_src.at[pl.ds(start, stok)], stage.at[s, 0], bsem.at[s])]
        return [pltpu.make_async_copy(stage.at[s, k], side_dst.at[pl.ds(pl.multiple_of(k * n_side_tok + step * stok, 8), stok)],
                                      bsem.at[s]) for k in range(2)]

    def side_pre():
        if side_scatter:
            @pl.when(i == 0)
            def _():
                _zero_fill(zs_ref, ze_ref, side_dst, ybuf.at[1, pl.ds(0, MOE_BLOCK)], rsem, side_dst.shape[0] // MOE_BLOCK)
                for c in block_copies(0, 0):
                    c.start()

            @pl.when((i >= 1) & (i <= SIDE_STEPS))
            def _():
                rows_wait(1 - slot)

            @pl.when(i + 1 < SIDE_STEPS)
            def _():
                for c in block_copies(i + 1, 1 - slot):
                    c.start()

            @pl.when(i < SIDE_STEPS)
            def _():
                for c in block_copies(i, slot):
                    c.wait()
        else:
            @pl.when((i >= 2) & (i <= SIDE_STEPS + 1))
            def _():
                for c in block_copies(i - 2, slot):
                    c.wait()

            @pl.when((i >= 1) & (i <= SIDE_STEPS))
            def _():
                rows_wait(1 - slot)
                for c in block_copies(i - 1, 1 - slot):
                    c.start()

    def side_issue():
        for q in range(2 * stok):
            row_copy(q, slot).start(priority=1)

    def w_copies(e, s):
        return [pltpu.make_async_copy(w.at[e], wf32.at[s, k], wsem.at[s]) for k, w in enumerate((wg_ref, wu_ref, wd_ref))]

    def rows_of(item, j):
        return pl.ds(pl.multiple_of((ib_ref[item] + j) * MOE_BLOCK, MOE_BLOCK), MOE_BLOCK)

    def in_copy(item, s, j):
        return pltpu.make_async_copy(xb_ref.at[rows_of(item, j)], xbuf.at[s, pl.ds(j * MOE_BLOCK, MOE_BLOCK)], insem.at[s])

    def out_copy(item, s, j):
        return pltpu.make_async_copy(ybuf.at[s, pl.ds(j * MOE_BLOCK, MOE_BLOCK)], yb_ref.at[rows_of(item, j)], outsem.at[s])

    def for_blocks(item, fn):
        for j in range(ITEM_SUB):
            @pl.when(j < ins_ref[item])
            def _():
                fn(j)

    @pl.when(i == 0)
    def _():
        for c in w_copies(ie_ref[0], 0):
            c.start()
        for_blocks(0, lambda j: in_copy(0, 0, j).start())

    side_pre()

    @pl.when(i < ni)
    def _():
        @pl.when(i >= 2)
        def _():
            for_blocks(i - 2, lambda j: out_copy(i - 2, slot, j).wait())

        @pl.when(i + 1 < ni)
        def _():
            for_blocks(i + 1, lambda j: in_copy(i + 1, 1 - slot, j).start())

        changed = (i == 0) | (ie_ref[i] != ie_ref[jnp.maximum(i - 1, 0)])

        @pl.when(changed)
        def _():
            ws = iord_ref[i] % 2
            for c in w_copies(ie_ref[i], ws):
                c.wait()

            @pl.when(inxt_ref[i] >= 0)
            def _():
                for c in w_copies(inxt_ref[i], 1 - ws):
                    c.start()

            wgb[...] = wf32[ws, 0].astype(BF16)
            wub[...] = wf32[ws, 1].astype(BF16)
            wdb[...] = wf32[ws, 2].astype(BF16)

        for_blocks(i, lambda j: in_copy(i, slot, j).wait())

        for ns in range(1, ITEM_SUB + 1):
            for with_side in (True, False):
                @pl.when((ins_ref[i] == ns) & ((i < SIDE_STEPS) == with_side))
                def _():
                    if with_side:
                        side_issue()
                    rows = pl.ds(0, ns * MOE_BLOCK)
                    x = xbuf[slot, rows].astype(BF16)
                    g = _dot(x, wgb[...])
                    u = _dot(x, wub[...])
                    hid = (g * (1.0 / (1.0 + jnp.exp(-g))) * u).astype(BF16)
                    ybuf[slot, rows] = _dot(hid, wdb[...])

        for_blocks(i, lambda j: out_copy(i, slot, j).start())

    @pl.when(i == n_steps - 1)
    def _():
        for back in (2, 1):
            k = ni - back

            @pl.when(k >= 0)
            def _():
                for_blocks(k, lambda j: out_copy(k, k % 2, j).wait())

        ybuf[0, pl.ds(0, MOE_BLOCK)] = jnp.zeros((MOE_BLOCK, D_MODEL), F32)
        last = jnp.maximum(ni - 1, 0)
        first_unused = ib_ref[last] + ins_ref[last]
        n_blocks = yb_ref.shape[0] // MOE_BLOCK

        def zero_copy(b):
            dst = yb_ref.at[pl.ds(pl.multiple_of(b * MOE_BLOCK, MOE_BLOCK), MOE_BLOCK)]
            return pltpu.make_async_copy(ybuf.at[0, pl.ds(0, MOE_BLOCK)], dst, outsem.at[0])

        def z_start(b, c):
            zero_copy(b).start()
            return c

        def z_wait(b, c):
            zero_copy(b).wait()
            return c

        lax.fori_loop(first_unused, n_blocks, z_start, 0)
        lax.fori_loop(first_unused, n_blocks, z_wait, 0)


def _expert_call(plan, zs, ze, side_idx, xb, wg, wu, wd, side_src, side_dst_rows, side_scatter, name):
    p_rows = plan["p_rows"]
    item_e, item_b, item_ns, item_ord, item_nxt, n_items = plan["items"]
    max_items = item_e.shape[0]
    side_steps, _, side_rows = side_idx.shape
    assert side_steps == SIDE_STEPS and max_items >= SIDE_STEPS + 2
    stok = side_rows // 2
    side_spec = pl.BlockSpec((1, 1, side_rows), lambda i, *_: (jnp.minimum(i, SIDE_STEPS - 1), 0, 0),
                             memory_space=pltpu.SMEM)
    grid_spec = pltpu.PrefetchScalarGridSpec(
        num_scalar_prefetch=8,
        grid=(max_items,),
        in_specs=[side_spec] + [pl.BlockSpec(memory_space=pl.ANY)] * 5,
        out_specs=[pl.BlockSpec(memory_space=pl.ANY)] * 2,
        scratch_shapes=[
            pltpu.VMEM((2, ITEM_ROWS, D_MODEL), F32),
            pltpu.VMEM((2, ITEM_ROWS, D_MODEL), F32),
            pltpu.VMEM((2, 3, D_MODEL, D_MODEL), F32),
            pltpu.VMEM((D_MODEL, D_MODEL), BF16),
            pltpu.VMEM((D_MODEL, D_MODEL), BF16),
            pltpu.VMEM((D_MODEL, D_MODEL), BF16),
            pltpu.VMEM((2, 1 if side_scatter else 2, stok, D_MODEL), F32),
            pltpu.SemaphoreType.DMA((2,)),
            pltpu.SemaphoreType.DMA((2,)),
            pltpu.SemaphoreType.DMA((2,)),
            pltpu.SemaphoreType.DMA(()),
            pltpu.SemaphoreType.DMA((2,)),
        ],
    )
    return pl.pallas_call(
        functools.partial(_expert_kernel, side_scatter=side_scatter),
        grid_spec=grid_spec,
        out_shape=[jax.ShapeDtypeStruct((p_rows, D_MODEL), F32), jax.ShapeDtypeStruct((side_dst_rows, D_MODEL), F32)],
        compiler_params=_cparams(("arbitrary",)),
        name=name,
    )(item_e, item_b, item_ns, item_ord, item_nxt, n_items, zs, ze, side_idx, xb, wg, wu, wd, side_src)


def _combine_kernel(dnxt_ref, ya0_ref, ya1_ref, yb_ref, x1_ref, wts_ref, g2_ref, lg_ref, lb_ref, o_ref, buf, sem, *, nt_a):
    i = pl.program_id(0)
    tm = x1_ref.shape[0]
    slot = i % 2
    w = wts_ref[...]

    def finish(y0, y1):
        f = w[:, 0:1] * y0 + w[:, 64:65] * y1
        o_ref[...] = _layer_norm(DEEPNORM_ALPHA * x1_ref[...] + g2_ref[0] * f, lg_ref[...], lb_ref[...])

    @pl.when((i + 1 >= nt_a) & (i + 1 < pl.num_programs(0)))
    def _():
        s = 1 - slot

        def start(n, c):
            pltpu.make_async_copy(_row(yb_ref, dnxt_ref[0, 0, n]), _row(buf.at[s, 0], n), sem.at[s]).start()
            pltpu.make_async_copy(_row(yb_ref, dnxt_ref[0, 0, tm + n]), _row(buf.at[s, 1], n), sem.at[s]).start(priority=1)
            return c

        lax.fori_loop(0, tm, start, 0, unroll=ISSUE_UNROLL)

    @pl.when(i < nt_a)
    def _():
        finish(ya0_ref[...], ya1_ref[...])

    @pl.when(i >= nt_a)
    def _():
        for k in range(2):
            pltpu.make_async_copy(yb_ref.at[pl.ds(0, tm)], buf.at[slot, k], sem.at[slot]).wait()
        finish(buf[slot, 0], buf[slot, 1])


def _combine_call(ysl_a, dest3_b, yb_b, x1, wts, g2, lg, lb, tiles_per_batch):
    n = x1.shape[0]
    tm = TOK_TILE
    tpb = tiles_per_batch
    row = lambda i: (i, 0)
    full = lambda i: (0, 0)
    nt = n // tm
    nt_b = dest3_b.shape[0]
    nt_a = nt - nt_b
    return pl.pallas_call(
        functools.partial(_combine_kernel, nt_a=nt_a),
        grid=(nt,),
        in_specs=[
            pl.BlockSpec((1, 1, 2 * tm), lambda i: (jnp.clip(i + 1 - nt_a, 0, nt_b - 1), 0, 0), memory_space=pltpu.SMEM),
            pl.BlockSpec((tm, D_MODEL), lambda i: (jnp.minimum(i, nt_a - 1), 0)),
            pl.BlockSpec((tm, D_MODEL), lambda i: (nt_a + jnp.minimum(i, nt_a - 1), 0)),
            pl.BlockSpec(memory_space=pl.ANY),
            pl.BlockSpec((tm, D_MODEL), row),
            pl.BlockSpec((tm, LANES), row),
            pl.BlockSpec((1, 1, D_MODEL), lambda i: (i // tpb, 0, 0)),
            pl.BlockSpec((1, D_MODEL), full),
            pl.BlockSpec((1, D_MODEL), full),
        ],
        out_specs=pl.BlockSpec((tm, D_MODEL), row),
        out_shape=jax.ShapeDtypeStruct((n, D_MODEL), F32),
        scratch_shapes=[pltpu.VMEM((2, 2, tm, D_MODEL), F32), pltpu.SemaphoreType.DMA((2,))],
        compiler_params=_cparams(("arbitrary",)),
        name="combine",
    )(dest3_b, ysl_a, ysl_a, yb_b, x1, wts, g2, lg, lb)


def _head_interleave(w, lo):
    blk = w[:, lo:lo + 2 * QK_COLS].reshape(D_MODEL, 2, DA_HEADS, DA_HEAD_DIM)
    return blk.transpose(0, 2, 1, 3).reshape(D_MODEL, 2 * QK_COLS)


def _cumsum_small(x):
    idx = jnp.arange(x.shape[0], dtype=jnp.int32)
    return jnp.sum(jnp.where(idx[None, :] <= idx[:, None], x[None, :], 0), -1)


def _lookup(table, idx):
    return jnp.sum(jnp.where(idx[:, None] == jnp.arange(table.shape[0], dtype=jnp.int32)[None, :], table[None, :], 0), -1)


def _rope_tables(seq):
    rows_n = seq // GRID_W
    rows = jnp.repeat(jnp.arange(rows_n, dtype=F32), GRID_W)
    cols = jnp.tile(jnp.arange(GRID_W, dtype=F32), rows_n)
    half = DA_HEAD_DIM // 4
    inv = ROPE_THETA ** (-jnp.arange(half, dtype=F32) / half)
    ang_r = rows[:, None] * inv[None, :]
    ang_c = cols[:, None] * inv[None, :]
    cos64 = jnp.concatenate([jnp.cos(ang_r), jnp.cos(ang_r), jnp.cos(ang_c), jnp.cos(ang_c)], -1)
    sin64 = jnp.concatenate([-jnp.sin(ang_r), jnp.sin(ang_r), -jnp.sin(ang_c), jnp.sin(ang_c)], -1)
    return jnp.tile(cos64, (1, 2)), jnp.tile(sin64, (1, 2))


def _moe_plan(route, tcnt):
    tm = TOK_TILE
    nt = tcnt.shape[0]
    n = nt * tm
    cnt_te = tcnt[:, :, 0].astype(jnp.int32)
    counts = jnp.sum(cnt_te, 0)
    padded = (counts + MOE_BLOCK - 1) // MOE_BLOCK * MOE_BLOCK
    pad_end = _cumsum_small(padded)
    pad_start = pad_end - padded
    tix = jnp.arange(nt, dtype=jnp.int32)
    tile_prefix = jnp.sum(jnp.where(tix[None, :, None] < tix[:, None, None], cnt_te[None], 0), 1)
    base = pad_start[None, :] + tile_prefix
    ridx = route[:4].astype(jnp.int32).reshape(4, nt, tm)
    ex = jnp.arange(N_EXPERTS, dtype=jnp.int32)

    def slot_dest(eid, rank):
        return jnp.sum(jnp.where(eid[None] == ex[:, None, None], base.T[:, :, None], 0), 0) + rank

    dest = jnp.concatenate([slot_dest(ridx[0], ridx[2]), slot_dest(ridx[1], ridx[3])], -1)
    n_blocks = (n * 2) // MOE_BLOCK + N_EXPERTS
    nb_e = padded // MOE_BLOCK
    items_e = (nb_e + ITEM_SUB - 1) // ITEM_SUB
    item_end = _cumsum_small(items_e)
    max_items = (n_blocks + (ITEM_SUB - 1) * N_EXPERTS) // ITEM_SUB
    it = jnp.arange(max_items, dtype=jnp.int32)
    item_e = jnp.minimum(jnp.sum((it[:, None] >= item_end[None, :]).astype(jnp.int32), -1), N_EXPERTS - 1)
    item_j = it - _lookup(item_end - items_e, item_e)
    item_b = _lookup(pad_start // MOE_BLOCK, item_e) + ITEM_SUB * item_j
    item_ns = jnp.clip(_lookup(nb_e, item_e) - ITEM_SUB * item_j, 0, ITEM_SUB)
    nonempty = (items_e > 0).astype(jnp.int32)
    ord_e = _cumsum_small(nonempty) - 1
    later = (ex[None, :] > ex[:, None]) & (nonempty[None, :] > 0)
    nxt_e = jnp.min(jnp.where(later, ex[None, :], N_EXPERTS), -1)
    nxt_e = jnp.where(nxt_e == N_EXPERTS, -1, nxt_e)
    items = (item_e, item_b, item_ns, _lookup(ord_e, item_e), _lookup(nxt_e, item_e), item_end[-1:])
    return {
        "dest": dest,
        "zs": (pad_start + counts).astype(jnp.int32),
        "ze": pad_end.astype(jnp.int32),
        "items": tuple(a.astype(jnp.int32) for a in items),
        "p_rows": n_blocks * MOE_BLOCK,
    }


def _side_list(dest):
    nt = dest.shape[0]
    tm = TOK_TILE
    by_slot = dest.reshape(nt, 2, tm).transpose(1, 0, 2).reshape(2, SIDE_STEPS, -1)
    return by_slot.transpose(1, 0, 2).reshape(SIDE_STEPS, 1, -1).astype(jnp.int32)


def kernel(x, c, ctx, c_ctx, w_mod, b_mod, w_in, lam_q1, lam_k1, lam_q2, lam_k2, subln_g, sg_ln_g, sg_ln_b, sg_w, sg_b, w_out, ln1_g, ln1_b, router_group_w, router_group_b, router_expert_w, router_expert_b, exp_w_gate, exp_w_up, exp_w_down, ln2_g, ln2_b):
    b, l, d = x.shape
    n = b * l
    tm = TOK_TILE
    nt = n // tm

    cc = jnp.zeros((b + 8, d), F32).at[:b].set(c).at[b].set(c_ctx)
    mod = _mod_call(cc, w_mod[0], b_mod[0][None, :])
    sh1, sc1, g1, sh2, sc2, g2 = [mod[:b, j * d:(j + 1) * d].reshape(b, 1, d) for j in range(6)]
    csh1 = mod[b:b + 1, 0:d]
    csc1 = mod[b:b + 1, d:2 * d]

    wi = w_in[0]
    w_all = jnp.concatenate([_head_interleave(wi, 0), _head_interleave(wi, KV_LO), wi[:, 2 * KV_LO:]], -1).astype(BF16)
    kc, vc = _ctx_call(ctx, csc1, csh1, w_all[:, KV_LO:KV_HI])

    cos_t, sin_t = _rope_tables(l)
    sbias = jnp.repeat(sg_b[0].T, LANES, axis=1)
    q, k, v, sg = _in_call(x, sc1, sh1, w_all, cos_t, sin_t, sg_ln_g[0][None, :], sg_ln_b[0][None, :],
                           sg_w[0].astype(BF16), sbias)

    lamv = jnp.stack([lam_q1[0], lam_k1[0], lam_q2[0], lam_k2[0]]).astype(F32)
    da = _attn_call(lamv, q, k, kc, v, vc, subln_g[0][None, :])

    e_lo = MOE_EXPERTS_PER_GROUP
    wr = jnp.zeros((LANES, d), F32).at[:MOE_GROUPS].set(router_group_w[0].T).at[e_lo:e_lo + N_EXPERTS].set(router_expert_w[0].T)
    br = jnp.zeros((LANES,), F32).at[:MOE_GROUPS].set(router_group_b[0]).at[e_lo:e_lo + N_EXPERTS].set(router_expert_b[0])
    br = jnp.broadcast_to(br[:, None], (LANES, tm))
    wr_hi = wr.astype(BF16)
    wr_split = jnp.concatenate([wr_hi, (wr - wr_hi.astype(F32)).astype(BF16)], 0)
    x1, t, route, wts, tcnt = _out_call(da.reshape(n, DA_WIDTH), sg.reshape(n, SG_WIDTH), x.reshape(n, d), g1, sc2, sh2,
                                        w_out[0].astype(BF16), ln1_g[0][None, :], ln1_b[0][None, :], wr_split, br, l // tm)

    nt_h = nt // 2
    n_h = nt_h * tm
    plan_a = _moe_plan(route[:, :n_h], tcnt[:nt_h])
    plan_b = _moe_plan(route[:, n_h:], tcnt[nt_h:])
    p_rows = plan_a["p_rows"]
    weights = (exp_w_gate[0], exp_w_up[0], exp_w_down[0])
    no_pad = jnp.zeros((N_EXPERTS,), jnp.int32)

    xb_a = _dispatch_call(plan_a["zs"], plan_a["ze"], plan_a["dest"].reshape(nt_h, 1, 2 * tm), t, p_rows)
    yb_a, xb_b = _expert_call(plan_a, plan_b["zs"], plan_b["ze"], _side_list(plan_b["dest"]), xb_a, *weights, t,
                              p_rows, True, "experts_a")
    yb_b, ysl_a = _expert_call(plan_b, no_pad, no_pad, _side_list(plan_a["dest"]), xb_b, *weights, yb_a,
                               2 * n_h, False, "experts_b")
    out = _combine_call(ysl_a, plan_b["dest"].reshape(nt_h, 1, 2 * tm), yb_b, x1, wts, g2,
                        ln2_g[0][None, :], ln2_b[0][None, :], l // tm)
    return out.reshape(b, l, d)
```

```python
import math

import jax
import jax.numpy as jnp
from jax import lax
from jax.experimental import pallas as pl
from jax.experimental.pallas import tpu as pltpu

F32 = jnp.float32
BF16 = jnp.bfloat16

D_MODEL = 1024
GRID_W = 64
DA_HEAD_DIM = 64
DA_V_DIM = 128
DA_WIDTH = 512
DA_HEADS = 4
QK_COLS = 256
SG_CHUNK = 128
SG_WIDTH = 512
SG_GROUPS = 4
KV_LO = 512
KV_HI = 1536
IN_COLS = 2560
ROPE_THETA = 10000.0
MOE_GROUPS = 4
MOE_EXPERTS_PER_GROUP = 8
N_EXPERTS = 32
MOE_BLOCK = 128
EPS = 1e-5
DEPTH = 1
DEEPNORM_ALPHA = (2.0 * DEPTH) ** 0.25
LAM_INIT = 0.8 - 0.6 * math.exp(-0.3 * 0)
Q_SCALE = DA_HEAD_DIM ** -0.5
LOG2E = math.log2(math.e)
SQRT_HALF = math.sqrt(0.5)

LANES = 128
TOK_TILE = 512
ATTN_TQ = 2048
ATTN_SUB = 256
VMEM_LIMIT = 56 * 1024 * 1024


def _cparams(sem):
    return pltpu.CompilerParams(dimension_semantics=sem, vmem_limit_bytes=VMEM_LIMIT)


def _dot(a, b):
    return jnp.dot(a, b, preferred_element_type=F32)


def _dot_hi(a, b):
    return jnp.dot(a, b, preferred_element_type=F32, precision=lax.Precision.HIGHEST)


def _layer_norm(y, g, b):
    mu = jnp.mean(y, -1, keepdims=True)
    yc = y - mu
    var = jnp.mean(yc * yc, -1, keepdims=True)
    return yc * lax.rsqrt(var + EPS) * g + b


def _row(ref, n):
    return ref.at[pl.ds(n, 1)]


def _mod_kernel(c_ref, w_ref, b_ref, o_ref):
    c = c_ref[...]
    s = c * (1.0 / (1.0 + jnp.exp(-c)))
    o_ref[...] = _dot_hi(s, w_ref[...]) + b_ref[...]


def _mod_call(cc, w_mod, b_mod):
    rows = cc.shape[0]
    n_out = w_mod.shape[1]
    bn = 1024
    return pl.pallas_call(
        _mod_kernel,
        grid=(n_out // bn,),
        in_specs=[
            pl.BlockSpec((rows, D_MODEL), lambda j: (0, 0)),
            pl.BlockSpec((D_MODEL, bn), lambda j: (0, j)),
            pl.BlockSpec((1, bn), lambda j: (0, j)),
        ],
        out_specs=pl.BlockSpec((rows, bn), lambda j: (0, j)),
        out_shape=jax.ShapeDtypeStruct((rows, n_out), F32),
        compiler_params=_cparams(("arbitrary",)),
        name="mod",
    )(cc, w_mod, b_mod)


def _ctx_kernel(c_ref, sc_ref, sh_ref, w_ref, kc_ref, vc_ref):
    h = (c_ref[0] * (1.0 + sc_ref[...]) + sh_ref[...]).astype(BF16)
    p = _dot(h, w_ref[...])
    kc_ref[0] = p[:, :2 * QK_COLS].astype(BF16)
    vc_ref[0] = p[:, 2 * QK_COLS:].astype(BF16)


def _ctx_call(ctx, csc1, csh1, w_kv):
    b, cl, _ = ctx.shape
    return pl.pallas_call(
        _ctx_kernel,
        grid=(b,),
        in_specs=[
            pl.BlockSpec((1, cl, D_MODEL), lambda i: (i, 0, 0)),
            pl.BlockSpec((1, D_MODEL), lambda i: (0, 0)),
            pl.BlockSpec((1, D_MODEL), lambda i: (0, 0)),
            pl.BlockSpec((D_MODEL, 2 * QK_COLS + DA_WIDTH), lambda i: (0, 0)),
        ],
        out_specs=[
            pl.BlockSpec((1, cl, 2 * QK_COLS), lambda i: (i, 0, 0)),
            pl.BlockSpec((1, cl, DA_WIDTH), lambda i: (i, 0, 0)),
        ],
        out_shape=[
            jax.ShapeDtypeStruct((b, cl, 2 * QK_COLS), BF16),
            jax.ShapeDtypeStruct((b, cl, DA_WIDTH), BF16),
        ],
        compiler_params=_cparams(("arbitrary",)),
        name="ctx_kv",
    )(ctx, csc1, csh1, w_kv)


def _in_kernel(x_ref, sc_ref, sh_ref, w_ref, cos_ref, sin_ref, lng_ref, lnb_ref,
               sw_ref, sb_ref, q_ref, k_ref, v_ref, sg_ref):
    tm = x_ref.shape[1]
    h = (x_ref[0] * (1.0 + sc_ref[0]) + sh_ref[0]).astype(BF16)
    p = _dot(h, w_ref[...])

    cos = cos_ref[...]
    sin = sin_ref[...]
    lane = lax.broadcasted_iota(jnp.int32, (tm, LANES), 1)
    first = (lane % 32) < 16

    def rope(t):
        partner = jnp.where(first, pltpu.roll(t, LANES - 16, 1), pltpu.roll(t, 16, 1))
        return t * cos + partner * sin

    for c in range(4):
        cs = slice(c * LANES, (c + 1) * LANES)
        q_ref[0, :, cs] = rope(p[:, cs] * (Q_SCALE * LOG2E)).astype(BF16)
        k_ref[0, :, cs] = rope(p[:, KV_LO + c * LANES:KV_LO + (c + 1) * LANES]).astype(BF16)
    v_ref[0] = p[:, 2 * KV_LO:KV_HI].astype(BF16)

    z = p[:, KV_HI:]
    gz = 0.5 * z * (1.0 + lax.erf(z * SQRT_HALF))
    u = gz[:, :SG_WIDTH]
    vn = _layer_norm(gz[:, SG_WIDTH:], lng_ref[...], lnb_ref[...]).astype(BF16)
    for c in range(tm // SG_CHUNK):
        rs = slice(c * SG_CHUNK, (c + 1) * SG_CHUNK)
        for g in range(SG_GROUPS):
            cs = slice(g * LANES, (g + 1) * LANES)
            s = _dot(sw_ref[g], vn[rs, cs]) + sb_ref[:, cs]
            sg_ref[0, rs, cs] = (u[rs, cs] * s).astype(BF16)


def _in_call(x, sc1, sh1, w_in, cos_t, sin_t, lng, lnb, sw, sbias):
    b, l, _ = x.shape
    tm = TOK_TILE
    full = lambda bi, i: (0, 0)
    return pl.pallas_call(
        _in_kernel,
        grid=(b, l // tm),
        in_specs=[
            pl.BlockSpec((1, tm, D_MODEL), lambda bi, i: (bi, i, 0)),
            pl.BlockSpec((1, 1, D_MODEL), lambda bi, i: (bi, 0, 0)),
            pl.BlockSpec((1, 1, D_MODEL), lambda bi, i: (bi, 0, 0)),
            pl.BlockSpec((D_MODEL, IN_COLS), full),
            pl.BlockSpec((tm, LANES), lambda bi, i: (i, 0)),
            pl.BlockSpec((tm, LANES), lambda bi, i: (i, 0)),
            pl.BlockSpec((1, SG_WIDTH), full),
            pl.BlockSpec((1, SG_WIDTH), full),
            pl.BlockSpec((SG_GROUPS, SG_CHUNK, SG_CHUNK), lambda bi, i: (0, 0, 0)),
            pl.BlockSpec((SG_CHUNK, SG_WIDTH), full),
        ],
        out_specs=[pl.BlockSpec((1, tm, 512), lambda bi, i: (bi, i, 0))] * 4,
        out_shape=[jax.ShapeDtypeStruct((b, l, 512), BF16)] * 4,
        compiler_params=_cparams(("arbitrary", "arbitrary")),
        name="in_proj",
    )(x, sc1, sh1, w_in, cos_t, sin_t, lng, lnb, sw, sbias)


def _attn_kernel(lam_ref, q_ref, k_ref, kc_ref, v_ref, vc_ref, g_ref, o_ref, kall, vall):
    lv = lam_ref[...]
    lam = (jnp.exp(jnp.sum(lv[0:1] * lv[1:2], -1, keepdims=True))
           - jnp.exp(jnp.sum(lv[2:3] * lv[3:4], -1, keepdims=True)) + LAM_INIT)
    l = k_ref.shape[1]

    @pl.when(pl.program_id(2) == 0)
    def _():
        kall[:l, :] = k_ref[0]
        kall[l:, :] = kc_ref[0]
        vall[:l, :DA_V_DIM] = v_ref[0]
        vall[l:, :DA_V_DIM] = vc_ref[0]
        vall[:, DA_V_DIM:] = jnp.ones((vall.shape[0], DA_V_DIM), BF16)

    nt = (((1,), (1,)), ((), ()))

    def branch(qm):
        s = lax.dot_general(qm, kall[...], nt, preferred_element_type=F32)
        m = jnp.max(s, -1, keepdims=True)
        e = jnp.exp2((s - m).astype(BF16))
        oe = _dot(e, vall[...])
        return oe[:, :DA_V_DIM] / oe[:, DA_V_DIM:DA_V_DIM + 1]

    for r in range(q_ref.shape[1] // ATTN_SUB):
        rs = pl.ds(r * ATTN_SUB, ATTN_SUB)
        q = q_ref[0, rs, :]
        lane = lax.broadcasted_iota(jnp.int32, q.shape, 1)
        zero = jnp.zeros_like(q)
        o = branch(jnp.where(lane < DA_HEAD_DIM, q, zero)) - lam * branch(jnp.where(lane >= DA_HEAD_DIM, q, zero))
        of = o * lax.rsqrt(jnp.mean(o * o, -1, keepdims=True) + EPS) * g_ref[...]
        o_ref[0, rs, :] = (of * (1.0 - LAM_INIT)).astype(BF16)


def _attn_call(lamv, q, k, kc, v, vc, subln_g):
    b, l, _ = q.shape
    cl = kc.shape[1]
    tq = ATTN_TQ
    return pl.pallas_call(
        _attn_kernel,
        grid=(b, DA_HEADS, l // tq),
        in_specs=[
            pl.BlockSpec((4, DA_HEAD_DIM), lambda bi, h, i: (0, 0)),
            pl.BlockSpec((1, tq, LANES), lambda bi, h, i: (bi, i, h)),
            pl.BlockSpec((1, l, LANES), lambda bi, h, i: (bi, 0, h)),
            pl.BlockSpec((1, cl, LANES), lambda bi, h, i: (bi, 0, h)),
            pl.BlockSpec((1, l, LANES), lambda bi, h, i: (bi, 0, h)),
            pl.BlockSpec((1, cl, LANES), lambda bi, h, i: (bi, 0, h)),
            pl.BlockSpec((1, DA_V_DIM), lambda bi, h, i: (0, 0)),
        ],
        out_specs=pl.BlockSpec((1, tq, LANES), lambda bi, h, i: (bi, i, h)),
        out_shape=jax.ShapeDtypeStruct((b, l, DA_WIDTH), BF16),
        scratch_shapes=[pltpu.VMEM((l + cl, LANES), BF16), pltpu.VMEM((l + cl, 2 * DA_V_DIM), BF16)],
        compiler_params=_cparams(("arbitrary", "arbitrary", "arbitrary")),
        name="attn",
    )(lamv, q, k, kc, v, vc, subln_g)


def _out_kernel(da_ref, sg_ref, x_ref, g1_ref, sc_ref, sh_ref, w_ref, lg_ref, lb_ref,
                wr_ref, br_ref, x1_ref, t_ref, route_ref, wts_ref, cnt_ref):
    tm = x_ref.shape[0]
    y = _dot(da_ref[...], w_ref[:DA_WIDTH, :]) + _dot(sg_ref[...], w_ref[DA_WIDTH:, :])
    x1 = _layer_norm(DEEPNORM_ALPHA * x_ref[...] + g1_ref[0] * y, lg_ref[...], lb_ref[...])
    x1_ref[...] = x1
    t = x1 * (1.0 + sc_ref[0]) + sh_ref[0]
    t_ref[...] = t

    t_hi = t.astype(BF16)
    t_lo = (t - t_hi.astype(F32)).astype(BF16)
    nt_dims = (((1,), (1,)), ((), ()))
    hw = lax.dot_general(wr_ref[...], t_hi, nt_dims, preferred_element_type=F32)
    lt = (hw[:LANES] + hw[LANES:] + lax.dot_general(wr_ref[:LANES, :], t_lo, nt_dims, preferred_element_type=F32)
          + br_ref[...])
    epg = MOE_EXPERTS_PER_GROUP
    row8 = lax.broadcasted_iota(jnp.int32, (epg, tm), 0).astype(F32)
    ninf = jnp.float32(-jnp.inf)
    big = jnp.float32(epg)
    gmask = row8 < MOE_GROUPS
    gl = jnp.where(gmask, lt[0:epg], ninf)
    gmax = jnp.max(gl, 0, keepdims=True)
    gsel = jnp.min(jnp.where(gl == gmax, row8, big), 0, keepdims=True)
    gsum = jnp.sum(jnp.where(gmask, jnp.exp(gl - gmax), 0.0), 0, keepdims=True)
    gw = 1.0 / gsum
    el = lt[epg:2 * epg]
    for g in range(1, MOE_GROUPS):
        el = jnp.where(gsel == g, lt[epg * (g + 1):epg * (g + 2)], el)
    v1 = jnp.max(el, 0, keepdims=True)
    i1 = jnp.min(jnp.where(el == v1, row8, big), 0, keepdims=True)
    el2 = jnp.where(row8 == i1, ninf, el)
    v2 = jnp.max(el2, 0, keepdims=True)
    i2 = jnp.min(jnp.where(el2 == v2, row8, big), 0, keepdims=True)
    e = jnp.exp(v2 - v1)
    w1 = gw / (1.0 + e)
    w2 = gw * e / (1.0 + e)
    e1 = gsel * epg + i1
    e2 = gsel * epg + i2

    rowe = lax.broadcasted_iota(jnp.int32, (N_EXPERTS, tm), 0).astype(F32)
    oh1 = rowe == e1
    oh2 = rowe == e2
    oh = jnp.where(oh1 | oh2, 1.0, 0.0)
    r_i = lax.broadcasted_iota(jnp.int32, (tm, tm), 0)
    c_i = lax.broadcasted_iota(jnp.int32, (tm, tm), 1)
    upper = jnp.where(r_i < c_i, 1.0, 0.0).astype(BF16)
    pref = _dot(oh.astype(BF16), upper)
    r1 = jnp.sum(jnp.where(oh1, pref, 0.0), 0, keepdims=True)
    r2 = jnp.sum(jnp.where(oh2, pref, 0.0), 0, keepdims=True)
    cnt_ref[0] = jnp.broadcast_to(jnp.sum(oh, 1, keepdims=True), (N_EXPERTS, LANES))

    route_ref[...] = jnp.where(row8 == 0, e1, jnp.where(row8 == 1, e2, jnp.where(row8 == 2, r1, jnp.where(row8 == 3, r2, 0.0))))
    rowl = lax.broadcasted_iota(jnp.int32, (LANES, tm), 0)
    wts_ref[...] = jnp.where(rowl < 64, w1, w2).T


def _out_call(da, sg, x2d, g1, sc2, sh2, w_out, lg, lb, wr, br, tiles_per_batch):
    n = x2d.shape[0]
    tm = TOK_TILE
    nt = n // tm
    tpb = tiles_per_batch
    row = lambda i: (i, 0)
    full = lambda i: (0, 0)
    per_b = lambda i: (i // tpb, 0, 0)
    return pl.pallas_call(
        _out_kernel,
        grid=(nt,),
        in_specs=[
            pl.BlockSpec((tm, DA_WIDTH), row),
            pl.BlockSpec((tm, SG_WIDTH), row),
            pl.BlockSpec((tm, D_MODEL), row),
            pl.BlockSpec((1, 1, D_MODEL), per_b),
            pl.BlockSpec((1, 1, D_MODEL), per_b),
            pl.BlockSpec((1, 1, D_MODEL), per_b),
            pl.BlockSpec((D_MODEL, D_MODEL), full),
            pl.BlockSpec((1, D_MODEL), full),
            pl.BlockSpec((1, D_MODEL), full),
            pl.BlockSpec((2 * LANES, D_MODEL), full),
            pl.BlockSpec((LANES, tm), full),
        ],
        out_specs=[
            pl.BlockSpec((tm, D_MODEL), row),
            pl.BlockSpec((tm, D_MODEL), row),
            pl.BlockSpec((8, tm), lambda i: (0, i)),
            pl.BlockSpec((tm, LANES), row),
            pl.BlockSpec((1, N_EXPERTS, LANES), lambda i: (i, 0, 0)),
        ],
        out_shape=[
            jax.ShapeDtypeStruct((n, D_MODEL), F32),
            jax.ShapeDtypeStruct((n, D_MODEL), F32),
            jax.ShapeDtypeStruct((8, n), F32),
            jax.ShapeDtypeStruct((n, LANES), F32),
            jax.ShapeDtypeStruct((nt, N_EXPERTS, LANES), F32),
        ],
        compiler_params=_cparams(("arbitrary",)),
        name="out_proj",
    )(da, sg, x2d, g1, sc2, sh2, w_out, lg, lb, wr, br)


ISSUE_UNROLL = 8
ITEM_SUB = 4
ITEM_ROWS = ITEM_SUB * MOE_BLOCK


def _expert_kernel(ie_ref, ib_ref, ins_ref, iord_ref, inxt_ref, ni_ref,
                   tcur_ref, tnxt_ref, t_ref, wg_ref, wu_ref, wd_ref, yb_ref,
                   xbuf, ybuf, wf32, wgb, wub, wdb, gsem, outsem, wsem):
    i = pl.program_id(0)
    n_steps = pl.num_programs(0)
    ni = ni_ref[0]
    slot = i % 2

    def w_copies(e, s):
        return [pltpu.make_async_copy(w.at[e], wf32.at[s, k], wsem.at[s]) for k, w in enumerate((wg_ref, wu_ref, wd_ref))]

    def gather_start(tok_ref, s, r):
        pltpu.make_async_copy(_row(t_ref, tok_ref[0, 0, r]), _row(xbuf.at[s], r), gsem.at[s]).start()

    def gather_wait(s):
        for j in range(ITEM_SUB):
            pltpu.make_async_copy(t_ref.at[pl.ds(0, MOE_BLOCK)], xbuf.at[s, pl.ds(j * MOE_BLOCK, MOE_BLOCK)], gsem.at[s]).wait()

    def rows_of(item, j):
        return pl.ds(pl.multiple_of((ib_ref[item] + j) * MOE_BLOCK, MOE_BLOCK), MOE_BLOCK)

    def out_copy(item, s, j):
        return pltpu.make_async_copy(ybuf.at[s, pl.ds(j * MOE_BLOCK, MOE_BLOCK)], yb_ref.at[rows_of(item, j)], outsem.at[s])

    def for_blocks(item, fn):
        for j in range(ITEM_SUB):
            @pl.when(j < ins_ref[item])
            def _():
                fn(j)

    @pl.when(i == 0)
    def _():
        for c in w_copies(ie_ref[0], 0):
            c.start()

        def first(r, c):
            gather_start(tcur_ref, 0, r)
            return c

        lax.fori_loop(0, ITEM_ROWS, first, 0, unroll=ISSUE_UNROLL)

    @pl.when(i <= ni)
    def _():
        gather_wait(slot)

    @pl.when(i < ni)
    def _():
        @pl.when(i >= 2)
        def _():
            for_blocks(i - 2, lambda j: out_copy(i - 2, slot, j).wait())

        changed = (i == 0) | (ie_ref[i] != ie_ref[jnp.maximum(i - 1, 0)])

        @pl.when(changed)
        def _():
            ws = iord_ref[i] % 2
            for c in w_copies(ie_ref[i], ws):
                c.wait()

            @pl.when(inxt_ref[i] >= 0)
            def _():
                for c in w_copies(inxt_ref[i], 1 - ws):
                    c.start()

            wgb[...] = wf32[ws, 0].astype(BF16)
            wub[...] = wf32[ws, 1].astype(BF16)
            wdb[...] = wf32[ws, 2].astype(BF16)

        for ns in range(1, ITEM_SUB + 1):
            @pl.when(ins_ref[i] == ns)
            def _():
                rows = pl.ds(0, ns * MOE_BLOCK)
                x = xbuf[slot, rows].astype(BF16)
                for r in range(ITEM_ROWS):
                    gather_start(tnxt_ref, 1 - slot, r)
                g = _dot(x, wgb[...])
                u = _dot(x, wub[...])
                hid = (g * (1.0 / (1.0 + jnp.exp(-g))) * u).astype(BF16)
                ybuf[slot, rows] = _dot(hid, wdb[...])

        for_blocks(i, lambda j: out_copy(i, slot, j).start())

    @pl.when(i == n_steps - 1)
    def _():
        @pl.when(ni == n_steps)
        def _():
            gather_wait(1 - slot)

        for back in (2, 1):
            k = ni - back

            @pl.when(k >= 0)
            def _():
                for_blocks(k, lambda j: out_copy(k, k % 2, j).wait())

        ybuf[0, pl.ds(0, MOE_BLOCK)] = jnp.zeros((MOE_BLOCK, D_MODEL), F32)
        last = jnp.maximum(ni - 1, 0)
        first_unused = ib_ref[last] + ins_ref[last]
        n_blocks = yb_ref.shape[0] // MOE_BLOCK

        def zero_copy(b):
            dst = yb_ref.at[pl.ds(pl.multiple_of(b * MOE_BLOCK, MOE_BLOCK), MOE_BLOCK)]
            return pltpu.make_async_copy(ybuf.at[0, pl.ds(0, MOE_BLOCK)], dst, outsem.at[0])

        def z_start(b, c):
            zero_copy(b).start()
            return c

        def z_wait(b, c):
            zero_copy(b).wait()
            return c

        lax.fori_loop(first_unused, n_blocks, z_start, 0)
        lax.fori_loop(first_unused, n_blocks, z_wait, 0)


def _expert_call(items, item_tok, t, wg, wu, wd, p_rows):
    max_items = item_tok.shape[0] - 1
    tok_spec = lambda off: pl.BlockSpec((1, 1, ITEM_ROWS), lambda i, *_: (i + off, 0, 0), memory_space=pltpu.SMEM)
    grid_spec = pltpu.PrefetchScalarGridSpec(
        num_scalar_prefetch=6,
        grid=(max_items,),
        in_specs=[tok_spec(0), tok_spec(1)] + [pl.BlockSpec(memory_space=pl.ANY)] * 4,
        out_specs=pl.BlockSpec(memory_space=pl.ANY),
        scratch_shapes=[
            pltpu.VMEM((2, ITEM_ROWS, D_MODEL), F32),
            pltpu.VMEM((2, ITEM_ROWS, D_MODEL), F32),
            pltpu.VMEM((2, 3, D_MODEL, D_MODEL), F32),
            pltpu.VMEM((D_MODEL, D_MODEL), BF16),
            pltpu.VMEM((D_MODEL, D_MODEL), BF16),
            pltpu.VMEM((D_MODEL, D_MODEL), BF16),
            pltpu.SemaphoreType.DMA((2,)),
            pltpu.SemaphoreType.DMA((2,)),
            pltpu.SemaphoreType.DMA((2,)),
        ],
    )
    return pl.pallas_call(
        _expert_kernel,
        grid_spec=grid_spec,
        out_shape=jax.ShapeDtypeStruct((p_rows, D_MODEL), F32),
        compiler_params=_cparams(("arbitrary",)),
        name="experts",
    )(*items, item_tok, item_tok, t, wg, wu, wd)


def _combine_kernel(dcur_ref, dnxt_ref, yb_ref, x1_ref, wts_ref, g2_ref, lg_ref, lb_ref, o_ref, buf, sem):
    i = pl.program_id(0)
    tm = x1_ref.shape[0]
    slot = i % 2

    def issue(dref, s):
        def start(n, c):
            pltpu.make_async_copy(_row(yb_ref, dref[0, 0, n]), _row(buf.at[s, 0], n), sem.at[s]).start()
            pltpu.make_async_copy(_row(yb_ref, dref[0, 0, tm + n]), _row(buf.at[s, 1], n), sem.at[s]).start(priority=1)
            return c

        lax.fori_loop(0, tm, start, 0, unroll=ISSUE_UNROLL)

    @pl.when(i == 0)
    def _():
        issue(dcur_ref, 0)

    @pl.when(i + 1 < pl.num_programs(0))
    def _():
        issue(dnxt_ref, 1 - slot)

    for k in range(2):
        pltpu.make_async_copy(yb_ref.at[pl.ds(0, tm)], buf.at[slot, k], sem.at[slot]).wait()
    w = wts_ref[...]
    f = w[:, 0:1] * buf[slot, 0] + w[:, 64:65] * buf[slot, 1]
    o_ref[...] = _layer_norm(DEEPNORM_ALPHA * x1_ref[...] + g2_ref[0] * f, lg_ref[...], lb_ref[...])


def _combine_call(dest3, yb, x1, wts, g2, lg, lb, tiles_per_batch):
    n = x1.shape[0]
    tm = TOK_TILE
    tpb = tiles_per_batch
    row = lambda i: (i, 0)
    full = lambda i: (0, 0)
    nt = n // tm
    return pl.pallas_call(
        _combine_kernel,
        grid=(nt,),
        in_specs=[
            pl.BlockSpec((1, 1, 2 * tm), lambda i: (i, 0, 0), memory_space=pltpu.SMEM),
            pl.BlockSpec((1, 1, 2 * tm), lambda i: (jnp.minimum(i + 1, nt - 1), 0, 0), memory_space=pltpu.SMEM),
            pl.BlockSpec(memory_space=pl.ANY),
            pl.BlockSpec((tm, D_MODEL), row),
            pl.BlockSpec((tm, LANES), row),
            pl.BlockSpec((1, 1, D_MODEL), lambda i: (i // tpb, 0, 0)),
            pl.BlockSpec((1, D_MODEL), full),
            pl.BlockSpec((1, D_MODEL), full),
        ],
        out_specs=pl.BlockSpec((tm, D_MODEL), row),
        out_shape=jax.ShapeDtypeStruct((n, D_MODEL), F32),
        scratch_shapes=[pltpu.VMEM((2, 2, tm, D_MODEL), F32), pltpu.SemaphoreType.DMA((2,))],
        compiler_params=_cparams(("arbitrary",)),
        name="combine",
    )(dest3, dest3, yb, x1, wts, g2, lg, lb)


def _head_interleave(w, lo):
    blk = w[:, lo:lo + 2 * QK_COLS].reshape(D_MODEL, 2, DA_HEADS, DA_HEAD_DIM)
    return blk.transpose(0, 2, 1, 3).reshape(D_MODEL, 2 * QK_COLS)


def _cumsum_small(x):
    idx = jnp.arange(x.shape[0], dtype=jnp.int32)
    return jnp.sum(jnp.where(idx[None, :] <= idx[:, None], x[None, :], 0), -1)


def _lookup(table, idx):
    return jnp.sum(jnp.where(idx[:, None] == jnp.arange(table.shape[0], dtype=jnp.int32)[None, :], table[None, :], 0), -1)


def _rope_tables(seq):
    rows_n = seq // GRID_W
    rows = jnp.repeat(jnp.arange(rows_n, dtype=F32), GRID_W)
    cols = jnp.tile(jnp.arange(GRID_W, dtype=F32), rows_n)
    half = DA_HEAD_DIM // 4
    inv = ROPE_THETA ** (-jnp.arange(half, dtype=F32) / half)
    ang_r = rows[:, None] * inv[None, :]
    ang_c = cols[:, None] * inv[None, :]
    cos64 = jnp.concatenate([jnp.cos(ang_r), jnp.cos(ang_r), jnp.cos(ang_c), jnp.cos(ang_c)], -1)
    sin64 = jnp.concatenate([-jnp.sin(ang_r), jnp.sin(ang_r), -jnp.sin(ang_c), jnp.sin(ang_c)], -1)
    return jnp.tile(cos64, (1, 2)), jnp.tile(sin64, (1, 2))


def kernel(x, c, ctx, c_ctx, w_mod, b_mod, w_in, lam_q1, lam_k1, lam_q2, lam_k2, subln_g, sg_ln_g, sg_ln_b, sg_w, sg_b, w_out, ln1_g, ln1_b, router_group_w, router_group_b, router_expert_w, router_expert_b, exp_w_gate, exp_w_up, exp_w_down, ln2_g, ln2_b):
    b, l, d = x.shape
    n = b * l
    tm = TOK_TILE
    nt = n // tm

    cc = jnp.zeros((b + 8, d), F32).at[:b].set(c).at[b].set(c_ctx)
    mod = _mod_call(cc, w_mod[0], b_mod[0][None, :])
    sh1, sc1, g1, sh2, sc2, g2 = [mod[:b, j * d:(j + 1) * d].reshape(b, 1, d) for j in range(6)]
    csh1 = mod[b:b + 1, 0:d]
    csc1 = mod[b:b + 1, d:2 * d]

    wi = w_in[0]
    w_all = jnp.concatenate([_head_interleave(wi, 0), _head_interleave(wi, KV_LO), wi[:, 2 * KV_LO:]], -1).astype(BF16)
    kc, vc = _ctx_call(ctx, csc1, csh1, w_all[:, KV_LO:KV_HI])

    cos_t, sin_t = _rope_tables(l)
    sbias = jnp.repeat(sg_b[0].T, LANES, axis=1)
    q, k, v, sg = _in_call(x, sc1, sh1, w_all, cos_t, sin_t, sg_ln_g[0][None, :], sg_ln_b[0][None, :],
                           sg_w[0].astype(BF16), sbias)

    lamv = jnp.stack([lam_q1[0], lam_k1[0], lam_q2[0], lam_k2[0]]).astype(F32)
    da = _attn_call(lamv, q, k, kc, v, vc, subln_g[0][None, :])

    e_lo = MOE_EXPERTS_PER_GROUP
    wr = jnp.zeros((LANES, d), F32).at[:MOE_GROUPS].set(router_group_w[0].T).at[e_lo:e_lo + N_EXPERTS].set(router_expert_w[0].T)
    br = jnp.zeros((LANES,), F32).at[:MOE_GROUPS].set(router_group_b[0]).at[e_lo:e_lo + N_EXPERTS].set(router_expert_b[0])
    br = jnp.broadcast_to(br[:, None], (LANES, tm))
    wr_hi = wr.astype(BF16)
    wr_split = jnp.concatenate([wr_hi, (wr - wr_hi.astype(F32)).astype(BF16)], 0)
    x1, t, route, wts, tcnt = _out_call(da.reshape(n, DA_WIDTH), sg.reshape(n, SG_WIDTH), x.reshape(n, d), g1, sc2, sh2,
                                        w_out[0].astype(BF16), ln1_g[0][None, :], ln1_b[0][None, :], wr_split, br, l // tm)

    cnt_te = tcnt[:, :, 0].astype(jnp.int32)
    counts = jnp.sum(cnt_te, 0)
    padded = (counts + MOE_BLOCK - 1) // MOE_BLOCK * MOE_BLOCK
    pad_end = _cumsum_small(padded)
    pad_start = pad_end - padded
    tix = jnp.arange(nt, dtype=jnp.int32)
    tile_prefix = jnp.sum(jnp.where(tix[None, :, None] < tix[:, None, None], cnt_te[None], 0), 1)
    base = pad_start[None, :] + tile_prefix
    ridx = route[:4].astype(jnp.int32).reshape(4, nt, tm)
    ex = jnp.arange(N_EXPERTS, dtype=jnp.int32)

    def slot_dest(eid, rank):
        return jnp.sum(jnp.where(eid[None] == ex[:, None, None], base.T[:, :, None], 0), 0) + rank

    dest3 = jnp.concatenate([slot_dest(ridx[0], ridx[2]), slot_dest(ridx[1], ridx[3])], -1).reshape(nt, 1, 2 * tm)
    n_blocks = (n * 2) // MOE_BLOCK + N_EXPERTS
    p_rows = n_blocks * MOE_BLOCK
    nb_e = padded // MOE_BLOCK
    items_e = (nb_e + ITEM_SUB - 1) // ITEM_SUB
    item_end = _cumsum_small(items_e)
    max_items = (n_blocks + (ITEM_SUB - 1) * N_EXPERTS) // ITEM_SUB
    it = jnp.arange(max_items + 1, dtype=jnp.int32)
    valid = it < item_end[-1]
    item_e = jnp.minimum(jnp.sum((it[:, None] >= item_end[None, :]).astype(jnp.int32), -1), N_EXPERTS - 1)
    item_j = it - _lookup(item_end - items_e, item_e)
    item_b = _lookup(pad_start // MOE_BLOCK, item_e) + ITEM_SUB * item_j
    item_ns = jnp.clip(_lookup(nb_e, item_e) - ITEM_SUB * item_j, 0, ITEM_SUB)
    item_ns = jnp.where(valid, item_ns, 0)
    item_b = jnp.where(valid, item_b, 0)
    flat_e = jnp.stack([ridx[0], ridx[1]], -1).reshape(-1)
    order = jnp.argsort(flat_e, stable=True).astype(jnp.int32)
    sorted_end = _cumsum_small(counts)
    u0 = jnp.where(valid, _lookup(sorted_end - counts, item_e) + ITEM_ROWS * item_j, 0)
    u_last = jnp.where(valid, _lookup(sorted_end, item_e) - 1, 0)
    u = jnp.minimum(u0[:, None] + jnp.arange(ITEM_ROWS, dtype=jnp.int32)[None, :], u_last[:, None])
    item_tok = (order[u] // 2).reshape(max_items + 1, 1, ITEM_ROWS)
    nonempty = (items_e > 0).astype(jnp.int32)
    ord_e = _cumsum_small(nonempty) - 1
    later = (ex[None, :] > ex[:, None]) & (nonempty[None, :] > 0)
    nxt_e = jnp.min(jnp.where(later, ex[None, :], N_EXPERTS), -1)
    nxt_e = jnp.where(nxt_e == N_EXPERTS, -1, nxt_e)
    item_ord = _lookup(ord_e, item_e)
    item_nxt = _lookup(nxt_e, item_e)

    items = [a.astype(jnp.int32) for a in (item_e, item_b, item_ns, item_ord, item_nxt, item_end[-1:])]
    yb = _expert_call(items, item_tok, t, exp_w_gate[0], exp_w_up[0], exp_w_down[0], p_rows)
    out = _combine_call(dest3, yb, x1, wts, g2, ln2_g[0][None, :], ln2_b[0][None, :], l // tm)
    return out.reshape(b, l, d)
```

```python
import math

import jax
import jax.numpy as jnp
from jax import lax
from jax.experimental import pallas as pl
from jax.experimental.pallas import tpu as pltpu

F32 = jnp.float32
BF16 = jnp.bfloat16

D_MODEL = 1024
GRID_W = 64
DA_HEAD_DIM = 64
DA_V_DIM = 128
DA_WIDTH = 512
DA_HEADS = 4
QK_COLS = 256
SG_CHUNK = 128
SG_WIDTH = 512
SG_GROUPS = 4
KV_LO = 512
KV_HI = 1536
IN_COLS = 2560
ROPE_THETA = 10000.0
MOE_GROUPS = 4
MOE_EXPERTS_PER_GROUP = 8
N_EXPERTS = 32
MOE_BLOCK = 128
EPS = 1e-5
DEPTH = 1
DEEPNORM_ALPHA = (2.0 * DEPTH) ** 0.25
LAM_INIT = 0.8 - 0.6 * math.exp(-0.3 * 0)
Q_SCALE = DA_HEAD_DIM ** -0.5
LOG2E = math.log2(math.e)
SQRT_HALF = math.sqrt(0.5)

LANES = 128
TOK_TILE = 512
ATTN_TQ = 2048
ATTN_SUB = 256
VMEM_LIMIT = 56 * 1024 * 1024


def _cparams(sem):
    return pltpu.CompilerParams(dimension_semantics=sem, vmem_limit_bytes=VMEM_LIMIT)


def _dot(a, b):
    return jnp.dot(a, b, preferred_element_type=F32)


def _dot_hi(a, b):
    return jnp.dot(a, b, preferred_element_type=F32, precision=lax.Precision.HIGHEST)


def _layer_norm(y, g, b):
    mu = jnp.mean(y, -1, keepdims=True)
    yc = y - mu
    var = jnp.mean(yc * yc, -1, keepdims=True)
    return yc * lax.rsqrt(var + EPS) * g + b


def _row(ref, n):
    return ref.at[pl.ds(n, 1)]


def _mod_kernel(c_ref, w_ref, b_ref, o_ref):
    c = c_ref[...]
    s = c * (1.0 / (1.0 + jnp.exp(-c)))
    o_ref[...] = _dot_hi(s, w_ref[...]) + b_ref[...]


def _mod_call(cc, w_mod, b_mod):
    rows = cc.shape[0]
    n_out = w_mod.shape[1]
    bn = 1024
    return pl.pallas_call(
        _mod_kernel,
        grid=(n_out // bn,),
        in_specs=[
            pl.BlockSpec((rows, D_MODEL), lambda j: (0, 0)),
            pl.BlockSpec((D_MODEL, bn), lambda j: (0, j)),
            pl.BlockSpec((1, bn), lambda j: (0, j)),
        ],
        out_specs=pl.BlockSpec((rows, bn), lambda j: (0, j)),
        out_shape=jax.ShapeDtypeStruct((rows, n_out), F32),
        compiler_params=_cparams(("arbitrary",)),
        name="mod",
    )(cc, w_mod, b_mod)


def _ctx_kernel(c_ref, sc_ref, sh_ref, w_ref, kc_ref, vc_ref):
    h = (c_ref[0] * (1.0 + sc_ref[...]) + sh_ref[...]).astype(BF16)
    p = _dot(h, w_ref[...])
    kc_ref[0] = p[:, :2 * QK_COLS].astype(BF16)
    vc_ref[0] = p[:, 2 * QK_COLS:].astype(BF16)


def _ctx_call(ctx, csc1, csh1, w_kv):
    b, cl, _ = ctx.shape
    return pl.pallas_call(
        _ctx_kernel,
        grid=(b,),
        in_specs=[
            pl.BlockSpec((1, cl, D_MODEL), lambda i: (i, 0, 0)),
            pl.BlockSpec((1, D_MODEL), lambda i: (0, 0)),
            pl.BlockSpec((1, D_MODEL), lambda i: (0, 0)),
            pl.BlockSpec((D_MODEL, 2 * QK_COLS + DA_WIDTH), lambda i: (0, 0)),
        ],
        out_specs=[
            pl.BlockSpec((1, cl, 2 * QK_COLS), lambda i: (i, 0, 0)),
            pl.BlockSpec((1, cl, DA_WIDTH), lambda i: (i, 0, 0)),
        ],
        out_shape=[
            jax.ShapeDtypeStruct((b, cl, 2 * QK_COLS), BF16),
            jax.ShapeDtypeStruct((b, cl, DA_WIDTH), BF16),
        ],
        compiler_params=_cparams(("arbitrary",)),
        name="ctx_kv",
    )(ctx, csc1, csh1, w_kv)


def _in_kernel(x_ref, sc_ref, sh_ref, w_ref, cos_ref, sin_ref, lng_ref, lnb_ref,
               sw_ref, sb_ref, q_ref, k_ref, v_ref, sg_ref):
    tm = x_ref.shape[1]
    h = (x_ref[0] * (1.0 + sc_ref[0]) + sh_ref[0]).astype(BF16)
    p = _dot(h, w_ref[...])

    cos = cos_ref[...]
    sin = sin_ref[...]
    lane = lax.broadcasted_iota(jnp.int32, (tm, LANES), 1)
    first = (lane % 32) < 16

    def rope(t):
        partner = jnp.where(first, pltpu.roll(t, LANES - 16, 1), pltpu.roll(t, 16, 1))
        return t * cos + partner * sin

    for c in range(4):
        cs = slice(c * LANES, (c + 1) * LANES)
        q_ref[0, :, cs] = rope(p[:, cs] * (Q_SCALE * LOG2E)).astype(BF16)
        k_ref[0, :, cs] = rope(p[:, KV_LO + c * LANES:KV_LO + (c + 1) * LANES]).astype(BF16)
    v_ref[0] = p[:, 2 * KV_LO:KV_HI].astype(BF16)

    z = p[:, KV_HI:]
    gz = 0.5 * z * (1.0 + lax.erf(z * SQRT_HALF))
    u = gz[:, :SG_WIDTH]
    vn = _layer_norm(gz[:, SG_WIDTH:], lng_ref[...], lnb_ref[...]).astype(BF16)
    for c in range(tm // SG_CHUNK):
        rs = slice(c * SG_CHUNK, (c + 1) * SG_CHUNK)
        for g in range(SG_GROUPS):
            cs = slice(g * LANES, (g + 1) * LANES)
            s = _dot(sw_ref[g], vn[rs, cs]) + sb_ref[:, cs]
            sg_ref[0, rs, cs] = (u[rs, cs] * s).astype(BF16)


def _in_call(x, sc1, sh1, w_in, cos_t, sin_t, lng, lnb, sw, sbias):
    b, l, _ = x.shape
    tm = TOK_TILE
    full = lambda bi, i: (0, 0)
    return pl.pallas_call(
        _in_kernel,
        grid=(b, l // tm),
        in_specs=[
            pl.BlockSpec((1, tm, D_MODEL), lambda bi, i: (bi, i, 0)),
            pl.BlockSpec((1, 1, D_MODEL), lambda bi, i: (bi, 0, 0)),
            pl.BlockSpec((1, 1, D_MODEL), lambda bi, i: (bi, 0, 0)),
            pl.BlockSpec((D_MODEL, IN_COLS), full),
            pl.BlockSpec((tm, LANES), lambda bi, i: (i, 0)),
            pl.BlockSpec((tm, LANES), lambda bi, i: (i, 0)),
            pl.BlockSpec((1, SG_WIDTH), full),
            pl.BlockSpec((1, SG_WIDTH), full),
            pl.BlockSpec((SG_GROUPS, SG_CHUNK, SG_CHUNK), lambda bi, i: (0, 0, 0)),
            pl.BlockSpec((SG_CHUNK, SG_WIDTH), full),
        ],
        out_specs=[pl.BlockSpec((1, tm, 512), lambda bi, i: (bi, i, 0))] * 4,
        out_shape=[jax.ShapeDtypeStruct((b, l, 512), BF16)] * 4,
        compiler_params=_cparams(("arbitrary", "arbitrary")),
        name="in_proj",
    )(x, sc1, sh1, w_in, cos_t, sin_t, lng, lnb, sw, sbias)


def _attn_kernel(lam_ref, q_ref, k_ref, kc_ref, v_ref, vc_ref, g_ref, o_ref, kall, vall):
    lv = lam_ref[...]
    lam = (jnp.exp(jnp.sum(lv[0:1] * lv[1:2], -1, keepdims=True))
           - jnp.exp(jnp.sum(lv[2:3] * lv[3:4], -1, keepdims=True)) + LAM_INIT)
    l = k_ref.shape[1]

    @pl.when(pl.program_id(2) == 0)
    def _():
        kall[:l, :] = k_ref[0]
        kall[l:, :] = kc_ref[0]
        vall[:l, :DA_V_DIM] = v_ref[0]
        vall[l:, :DA_V_DIM] = vc_ref[0]
        vall[:, DA_V_DIM:] = jnp.ones((vall.shape[0], DA_V_DIM), BF16)

    nt = (((1,), (1,)), ((), ()))

    def branch(qm):
        s = lax.dot_general(qm, kall[...], nt, preferred_element_type=F32)
        m = jnp.max(s, -1, keepdims=True)
        e = jnp.exp2((s - m).astype(BF16))
        oe = _dot(e, vall[...])
        return oe[:, :DA_V_DIM] / oe[:, DA_V_DIM:DA_V_DIM + 1]

    for r in range(q_ref.shape[1] // ATTN_SUB):
        rs = pl.ds(r * ATTN_SUB, ATTN_SUB)
        q = q_ref[0, rs, :]
        lane = lax.broadcasted_iota(jnp.int32, q.shape, 1)
        zero = jnp.zeros_like(q)
        o = branch(jnp.where(lane < DA_HEAD_DIM, q, zero)) - lam * branch(jnp.where(lane >= DA_HEAD_DIM, q, zero))
        of = o * lax.rsqrt(jnp.mean(o * o, -1, keepdims=True) + EPS) * g_ref[...]
        o_ref[0, rs, :] = (of * (1.0 - LAM_INIT)).astype(BF16)


def _attn_call(lamv, q, k, kc, v, vc, subln_g):
    b, l, _ = q.shape
    cl = kc.shape[1]
    tq = ATTN_TQ
    return pl.pallas_call(
        _attn_kernel,
        grid=(b, DA_HEADS, l // tq),
        in_specs=[
            pl.BlockSpec((4, DA_HEAD_DIM), lambda bi, h, i: (0, 0)),
            pl.BlockSpec((1, tq, LANES), lambda bi, h, i: (bi, i, h)),
            pl.BlockSpec((1, l, LANES), lambda bi, h, i: (bi, 0, h)),
            pl.BlockSpec((1, cl, LANES), lambda bi, h, i: (bi, 0, h)),
            pl.BlockSpec((1, l, LANES), lambda bi, h, i: (bi, 0, h)),
            pl.BlockSpec((1, cl, LANES), lambda bi, h, i: (bi, 0, h)),
            pl.BlockSpec((1, DA_V_DIM), lambda bi, h, i: (0, 0)),
        ],
        out_specs=pl.BlockSpec((1, tq, LANES), lambda bi, h, i: (bi, i, h)),
        out_shape=jax.ShapeDtypeStruct((b, l, DA_WIDTH), BF16),
        scratch_shapes=[pltpu.VMEM((l + cl, LANES), BF16), pltpu.VMEM((l + cl, 2 * DA_V_DIM), BF16)],
        compiler_params=_cparams(("arbitrary", "arbitrary", "arbitrary")),
        name="attn",
    )(lamv, q, k, kc, v, vc, subln_g)


def _out_kernel(da_ref, sg_ref, x_ref, g1_ref, sc_ref, sh_ref, w_ref, lg_ref, lb_ref,
                wr_ref, br_ref, x1_ref, t_ref, route_ref, wts_ref, cnt_ref):
    tm = x_ref.shape[0]
    y = _dot(da_ref[...], w_ref[:DA_WIDTH, :]) + _dot(sg_ref[...], w_ref[DA_WIDTH:, :])
    x1 = _layer_norm(DEEPNORM_ALPHA * x_ref[...] + g1_ref[0] * y, lg_ref[...], lb_ref[...])
    x1_ref[...] = x1
    t = x1 * (1.0 + sc_ref[0]) + sh_ref[0]
    t_ref[...] = t

    t_hi = t.astype(BF16)
    t_lo = (t - t_hi.astype(F32)).astype(BF16)
    nt_dims = (((1,), (1,)), ((), ()))
    hw = lax.dot_general(wr_ref[...], t_hi, nt_dims, preferred_element_type=F32)
    lt = (hw[:LANES] + hw[LANES:] + lax.dot_general(wr_ref[:LANES, :], t_lo, nt_dims, preferred_element_type=F32)
          + br_ref[...])
    epg = MOE_EXPERTS_PER_GROUP
    row8 = lax.broadcasted_iota(jnp.int32, (epg, tm), 0).astype(F32)
    ninf = jnp.float32(-jnp.inf)
    big = jnp.float32(epg)
    gmask = row8 < MOE_GROUPS
    gl = jnp.where(gmask, lt[0:epg], ninf)
    gmax = jnp.max(gl, 0, keepdims=True)
    gsel = jnp.min(jnp.where(gl == gmax, row8, big), 0, keepdims=True)
    gsum = jnp.sum(jnp.where(gmask, jnp.exp(gl - gmax), 0.0), 0, keepdims=True)
    gw = 1.0 / gsum
    el = lt[epg:2 * epg]
    for g in range(1, MOE_GROUPS):
        el = jnp.where(gsel == g, lt[epg * (g + 1):epg * (g + 2)], el)
    v1 = jnp.max(el, 0, keepdims=True)
    i1 = jnp.min(jnp.where(el == v1, row8, big), 0, keepdims=True)
    el2 = jnp.where(row8 == i1, ninf, el)
    v2 = jnp.max(el2, 0, keepdims=True)
    i2 = jnp.min(jnp.where(el2 == v2, row8, big), 0, keepdims=True)
    e = jnp.exp(v2 - v1)
    w1 = gw / (1.0 + e)
    w2 = gw * e / (1.0 + e)
    e1 = gsel * epg + i1
    e2 = gsel * epg + i2

    rowe = lax.broadcasted_iota(jnp.int32, (N_EXPERTS, tm), 0).astype(F32)
    oh1 = rowe == e1
    oh2 = rowe == e2
    oh = jnp.where(oh1 | oh2, 1.0, 0.0)
    r_i = lax.broadcasted_iota(jnp.int32, (tm, tm), 0)
    c_i = lax.broadcasted_iota(jnp.int32, (tm, tm), 1)
    upper = jnp.where(r_i < c_i, 1.0, 0.0).astype(BF16)
    pref = _dot(oh.astype(BF16), upper)
    r1 = jnp.sum(jnp.where(oh1, pref, 0.0), 0, keepdims=True)
    r2 = jnp.sum(jnp.where(oh2, pref, 0.0), 0, keepdims=True)
    cnt_ref[0] = jnp.broadcast_to(jnp.sum(oh, 1, keepdims=True), (N_EXPERTS, LANES))

    route_ref[...] = jnp.where(row8 == 0, e1, jnp.where(row8 == 1, e2, jnp.where(row8 == 2, r1, jnp.where(row8 == 3, r2, 0.0))))
    rowl = lax.broadcasted_iota(jnp.int32, (LANES, tm), 0)
    wts_ref[...] = jnp.where(rowl < 64, w1, w2).T


def _out_call(da, sg, x2d, g1, sc2, sh2, w_out, lg, lb, wr, br, tiles_per_batch):
    n = x2d.shape[0]
    tm = TOK_TILE
    nt = n // tm
    tpb = tiles_per_batch
    row = lambda i: (i, 0)
    full = lambda i: (0, 0)
    per_b = lambda i: (i // tpb, 0, 0)
    return pl.pallas_call(
        _out_kernel,
        grid=(nt,),
        in_specs=[
            pl.BlockSpec((tm, DA_WIDTH), row),
            pl.BlockSpec((tm, SG_WIDTH), row),
            pl.BlockSpec((tm, D_MODEL), row),
            pl.BlockSpec((1, 1, D_MODEL), per_b),
            pl.BlockSpec((1, 1, D_MODEL), per_b),
            pl.BlockSpec((1, 1, D_MODEL), per_b),
            pl.BlockSpec((D_MODEL, D_MODEL), full),
            pl.BlockSpec((1, D_MODEL), full),
            pl.BlockSpec((1, D_MODEL), full),
            pl.BlockSpec((2 * LANES, D_MODEL), full),
            pl.BlockSpec((LANES, tm), full),
        ],
        out_specs=[
            pl.BlockSpec((tm, D_MODEL), row),
            pl.BlockSpec((tm, D_MODEL), row),
            pl.BlockSpec((8, tm), lambda i: (0, i)),
            pl.BlockSpec((tm, LANES), row),
            pl.BlockSpec((1, N_EXPERTS, LANES), lambda i: (i, 0, 0)),
        ],
        out_shape=[
            jax.ShapeDtypeStruct((n, D_MODEL), F32),
            jax.ShapeDtypeStruct((n, D_MODEL), F32),
            jax.ShapeDtypeStruct((8, n), F32),
            jax.ShapeDtypeStruct((n, LANES), F32),
            jax.ShapeDtypeStruct((nt, N_EXPERTS, LANES), F32),
        ],
        compiler_params=_cparams(("arbitrary",)),
        name="out_proj",
    )(da, sg, x2d, g1, sc2, sh2, w_out, lg, lb, wr, br)


ISSUE_UNROLL = 8
ITEM_SUB = 4
ITEM_ROWS = ITEM_SUB * MOE_BLOCK
EXP_CHUNK = 256


def _expert_kernel(ie_ref, ib_ref, ins_ref, iord_ref, inxt_ref, ni_ref,
                   tcur_ref, tnxt_ref, t_ref, wg_ref, wu_ref, wd_ref, yb_ref,
                   xbuf, ybuf, wf32, wgb, wub, wdb, gsem, outsem, wsem):
    i = pl.program_id(0)
    n_steps = pl.num_programs(0)
    ni = ni_ref[0]
    slot = i % 2

    def w_copies(e, s):
        return [pltpu.make_async_copy(w.at[e], wf32.at[s, k], wsem.at[s]) for k, w in enumerate((wg_ref, wu_ref, wd_ref))]

    def gather_start(tok_ref, s, r, zero=0):
        pltpu.make_async_copy(_row(t_ref, tok_ref[0, 0, r] + zero), _row(xbuf.at[s], r), gsem.at[s]).start()

    def gather_wait(s):
        for j in range(ITEM_SUB):
            pltpu.make_async_copy(t_ref.at[pl.ds(0, MOE_BLOCK)], xbuf.at[s, pl.ds(j * MOE_BLOCK, MOE_BLOCK)], gsem.at[s]).wait()

    def rows_of(item, j):
        return pl.ds(pl.multiple_of((ib_ref[item] + j) * MOE_BLOCK, MOE_BLOCK), MOE_BLOCK)

    def out_copy(item, s, j):
        return pltpu.make_async_copy(ybuf.at[s, pl.ds(j * MOE_BLOCK, MOE_BLOCK)], yb_ref.at[rows_of(item, j)], outsem.at[s])

    def for_blocks(item, fn):
        for j in range(ITEM_SUB):
            @pl.when(j < ins_ref[item])
            def _():
                fn(j)

    @pl.when(i == 0)
    def _():
        for c in w_copies(ie_ref[0], 0):
            c.start()

        def first(r, c):
            gather_start(tcur_ref, 0, r)
            return c

        lax.fori_loop(0, ITEM_ROWS, first, 0, unroll=ISSUE_UNROLL)

    @pl.when(i <= ni)
    def _():
        gather_wait(slot)

    @pl.when(i < ni)
    def _():
        @pl.when(i >= 2)
        def _():
            for_blocks(i - 2, lambda j: out_copy(i - 2, slot, j).wait())

        changed = (i == 0) | (ie_ref[i] != ie_ref[jnp.maximum(i - 1, 0)])

        @pl.when(changed)
        def _():
            ws = iord_ref[i] % 2
            for c in w_copies(ie_ref[i], ws):
                c.wait()

            @pl.when(inxt_ref[i] >= 0)
            def _():
                for c in w_copies(inxt_ref[i], 1 - ws):
                    c.start()

            wgb[...] = wf32[ws, 0].astype(BF16)
            wub[...] = wf32[ws, 1].astype(BF16)
            wdb[...] = wf32[ws, 2].astype(BF16)

        def paced_zero(v):
            bits = lax.bitcast_convert_type(v[0, 0], jnp.int32)
            return lax.shift_right_logical(lax.shift_right_logical(bits, 16), 16)

        for ns in range(1, ITEM_SUB + 1):
            @pl.when(ins_ref[i] == ns)
            def _():
                rows = pl.ds(0, ns * MOE_BLOCK)
                x = xbuf[slot, rows].astype(BF16)
                n_chunks = D_MODEL // EXP_CHUNK
                bounds = [ITEM_ROWS * k // (3 * n_chunks) for k in range(3 * n_chunks + 1)]
                group = iter(zip(bounds[:-1], bounds[1:]))

                def issue_group(v):
                    lo, hi = next(group)
                    zero = paced_zero(v)
                    for r in range(lo, hi):
                        gather_start(tnxt_ref, 1 - slot, r, zero)

                hid = []
                for c in range(n_chunks):
                    cs = slice(c * EXP_CHUNK, (c + 1) * EXP_CHUNK)
                    g = _dot(x, wgb[:, cs])
                    issue_group(g)
                    u = _dot(x, wub[:, cs])
                    issue_group(u)
                    hid.append((g * (1.0 / (1.0 + jnp.exp(-g))) * u).astype(BF16))
                hid = jnp.concatenate(hid, axis=1)
                for c in range(n_chunks):
                    cs = slice(c * EXP_CHUNK, (c + 1) * EXP_CHUNK)
                    y = _dot(hid, wdb[:, cs])
                    ybuf[slot, rows, cs] = y
                    issue_group(y)

        for_blocks(i, lambda j: out_copy(i, slot, j).start())

    @pl.when(i == n_steps - 1)
    def _():
        @pl.when(ni == n_steps)
        def _():
            gather_wait(1 - slot)

        for back in (2, 1):
            k = ni - back

            @pl.when(k >= 0)
            def _():
                for_blocks(k, lambda j: out_copy(k, k % 2, j).wait())

        ybuf[0, pl.ds(0, MOE_BLOCK)] = jnp.zeros((MOE_BLOCK, D_MODEL), F32)
        last = jnp.maximum(ni - 1, 0)
        first_unused = ib_ref[last] + ins_ref[last]
        n_blocks = yb_ref.shape[0] // MOE_BLOCK

        def zero_copy(b):
            dst = yb_ref.at[pl.ds(pl.multiple_of(b * MOE_BLOCK, MOE_BLOCK), MOE_BLOCK)]
            return pltpu.make_async_copy(ybuf.at[0, pl.ds(0, MOE_BLOCK)], dst, outsem.at[0])

        def z_start(b, c):
            zero_copy(b).start()
            return c

        def z_wait(b, c):
            zero_copy(b).wait()
            return c

        lax.fori_loop(first_unused, n_blocks, z_start, 0)
        lax.fori_loop(first_unused, n_blocks, z_wait, 0)


def _expert_call(items, item_tok, t, wg, wu, wd, p_rows):
    max_items = item_tok.shape[0] - 1
    tok_spec = lambda off: pl.BlockSpec((1, 1, ITEM_ROWS), lambda i, *_: (i + off, 0, 0), memory_space=pltpu.SMEM)
    grid_spec = pltpu.PrefetchScalarGridSpec(
        num_scalar_prefetch=6,
        grid=(max_items,),
        in_specs=[tok_spec(0), tok_spec(1)] + [pl.BlockSpec(memory_space=pl.ANY)] * 4,
        out_specs=pl.BlockSpec(memory_space=pl.ANY),
        scratch_shapes=[
            pltpu.VMEM((2, ITEM_ROWS, D_MODEL), F32),
            pltpu.VMEM((2, ITEM_ROWS, D_MODEL), F32),
            pltpu.VMEM((2, 3, D_MODEL, D_MODEL), F32),
            pltpu.VMEM((D_MODEL, D_MODEL), BF16),
            pltpu.VMEM((D_MODEL, D_MODEL), BF16),
            pltpu.VMEM((D_MODEL, D_MODEL), BF16),
            pltpu.SemaphoreType.DMA((2,)),
            pltpu.SemaphoreType.DMA((2,)),
            pltpu.SemaphoreType.DMA((2,)),
        ],
    )
    return pl.pallas_call(
        _expert_kernel,
        grid_spec=grid_spec,
        out_shape=jax.ShapeDtypeStruct((p_rows, D_MODEL), F32),
        compiler_params=_cparams(("arbitrary",)),
        name="experts",
    )(*items, item_tok, item_tok, t, wg, wu, wd)


def _combine_kernel(dcur_ref, dnxt_ref, yb_ref, x1_ref, wts_ref, g2_ref, lg_ref, lb_ref, o_ref, buf, sem):
    i = pl.program_id(0)
    tm = x1_ref.shape[0]
    slot = i % 2

    def issue(dref, s):
        def start(n, c):
            pltpu.make_async_copy(_row(yb_ref, dref[0, 0, n]), _row(buf.at[s, 0], n), sem.at[s]).start()
            pltpu.make_async_copy(_row(yb_ref, dref[0, 0, tm + n]), _row(buf.at[s, 1], n), sem.at[s]).start(priority=1)
            return c

        lax.fori_loop(0, tm, start, 0, unroll=ISSUE_UNROLL)

    @pl.when(i == 0)
    def _():
        issue(dcur_ref, 0)

    @pl.when(i + 1 < pl.num_programs(0))
    def _():
        issue(dnxt_ref, 1 - slot)

    for k in range(2):
        pltpu.make_async_copy(yb_ref.at[pl.ds(0, tm)], buf.at[slot, k], sem.at[slot]).wait()
    w = wts_ref[...]
    f = w[:, 0:1] * buf[slot, 0] + w[:, 64:65] * buf[slot, 1]
    o_ref[...] = _layer_norm(DEEPNORM_ALPHA * x1_ref[...] + g2_ref[0] * f, lg_ref[...], lb_ref[...])


def _combine_call(dest3, yb, x1, wts, g2, lg, lb, tiles_per_batch):
    n = x1.shape[0]
    tm = TOK_TILE
    tpb = tiles_per_batch
    row = lambda i: (i, 0)
    full = lambda i: (0, 0)
    nt = n // tm
    return pl.pallas_call(
        _combine_kernel,
        grid=(nt,),
        in_specs=[
            pl.BlockSpec((1, 1, 2 * tm), lambda i: (i, 0, 0), memory_space=pltpu.SMEM),
            pl.BlockSpec((1, 1, 2 * tm), lambda i: (jnp.minimum(i + 1, nt - 1), 0, 0), memory_space=pltpu.SMEM),
            pl.BlockSpec(memory_space=pl.ANY),
            pl.BlockSpec((tm, D_MODEL), row),
            pl.BlockSpec((tm, LANES), row),
            pl.BlockSpec((1, 1, D_MODEL), lambda i: (i // tpb, 0, 0)),
            pl.BlockSpec((1, D_MODEL), full),
            pl.BlockSpec((1, D_MODEL), full),
        ],
        out_specs=pl.BlockSpec((tm, D_MODEL), row),
        out_shape=jax.ShapeDtypeStruct((n, D_MODEL), F32),
        scratch_shapes=[pltpu.VMEM((2, 2, tm, D_MODEL), F32), pltpu.SemaphoreType.DMA((2,))],
        compiler_params=_cparams(("arbitrary",)),
        name="combine",
    )(dest3, dest3, yb, x1, wts, g2, lg, lb)


def _head_interleave(w, lo):
    blk = w[:, lo:lo + 2 * QK_COLS].reshape(D_MODEL, 2, DA_HEADS, DA_HEAD_DIM)
    return blk.transpose(0, 2, 1, 3).reshape(D_MODEL, 2 * QK_COLS)


def _cumsum_small(x):
    idx = jnp.arange(x.shape[0], dtype=jnp.int32)
    return jnp.sum(jnp.where(idx[None, :] <= idx[:, None], x[None, :], 0), -1)


def _lookup(table, idx):
    return jnp.sum(jnp.where(idx[:, None] == jnp.arange(table.shape[0], dtype=jnp.int32)[None, :], table[None, :], 0), -1)


def _rope_tables(seq):
    rows_n = seq // GRID_W
    rows = jnp.repeat(jnp.arange(rows_n, dtype=F32), GRID_W)
    cols = jnp.tile(jnp.arange(GRID_W, dtype=F32), rows_n)
    half = DA_HEAD_DIM // 4
    inv = ROPE_THETA ** (-jnp.arange(half, dtype=F32) / half)
    ang_r = rows[:, None] * inv[None, :]
    ang_c = cols[:, None] * inv[None, :]
    cos64 = jnp.concatenate([jnp.cos(ang_r), jnp.cos(ang_r), jnp.cos(ang_c), jnp.cos(ang_c)], -1)
    sin64 = jnp.concatenate([-jnp.sin(ang_r), jnp.sin(ang_r), -jnp.sin(ang_c), jnp.sin(ang_c)], -1)
    return jnp.tile(cos64, (1, 2)), jnp.tile(sin64, (1, 2))


def kernel(x, c, ctx, c_ctx, w_mod, b_mod, w_in, lam_q1, lam_k1, lam_q2, lam_k2, subln_g, sg_ln_g, sg_ln_b, sg_w, sg_b, w_out, ln1_g, ln1_b, router_group_w, router_group_b, router_expert_w, router_expert_b, exp_w_gate, exp_w_up, exp_w_down, ln2_g, ln2_b):
    b, l, d = x.shape
    n = b * l
    tm = TOK_TILE
    nt = n // tm

    cc = jnp.zeros((b + 8, d), F32).at[:b].set(c).at[b].set(c_ctx)
    mod = _mod_call(cc, w_mod[0], b_mod[0][None, :])
    sh1, sc1, g1, sh2, sc2, g2 = [mod[:b, j * d:(j + 1) * d].reshape(b, 1, d) for j in range(6)]
    csh1 = mod[b:b + 1, 0:d]
    csc1 = mod[b:b + 1, d:2 * d]

    wi = w_in[0]
    w_all = jnp.concatenate([_head_interleave(wi, 0), _head_interleave(wi, KV_LO), wi[:, 2 * KV_LO:]], -1).astype(BF16)
    kc, vc = _ctx_call(ctx, csc1, csh1, w_all[:, KV_LO:KV_HI])

    cos_t, sin_t = _rope_tables(l)
    sbias = jnp.repeat(sg_b[0].T, LANES, axis=1)
    q, k, v, sg = _in_call(x, sc1, sh1, w_all, cos_t, sin_t, sg_ln_g[0][None, :], sg_ln_b[0][None, :],
                           sg_w[0].astype(BF16), sbias)

    lamv = jnp.stack([lam_q1[0], lam_k1[0], lam_q2[0], lam_k2[0]]).astype(F32)
    da = _attn_call(lamv, q, k, kc, v, vc, subln_g[0][None, :])

    e_lo = MOE_EXPERTS_PER_GROUP
    wr = jnp.zeros((LANES, d), F32).at[:MOE_GROUPS].set(router_group_w[0].T).at[e_lo:e_lo + N_EXPERTS].set(router_expert_w[0].T)
    br = jnp.zeros((LANES,), F32).at[:MOE_GROUPS].set(router_group_b[0]).at[e_lo:e_lo + N_EXPERTS].set(router_expert_b[0])
    br = jnp.broadcast_to(br[:, None], (LANES, tm))
    wr_hi = wr.astype(BF16)
    wr_split = jnp.concatenate([wr_hi, (wr - wr_hi.astype(F32)).astype(BF16)], 0)
    x1, t, route, wts, tcnt = _out_call(da.reshape(n, DA_WIDTH), sg.reshape(n, SG_WIDTH), x.reshape(n, d), g1, sc2, sh2,
                                        w_out[0].astype(BF16), ln1_g[0][None, :], ln1_b[0][None, :], wr_split, br, l // tm)

    cnt_te = tcnt[:, :, 0].astype(jnp.int32)
    counts = jnp.sum(cnt_te, 0)
    padded = (counts + MOE_BLOCK - 1) // MOE_BLOCK * MOE_BLOCK
    pad_end = _cumsum_small(padded)
    pad_start = pad_end - padded
    tix = jnp.arange(nt, dtype=jnp.int32)
    tile_prefix = jnp.sum(jnp.where(tix[None, :, None] < tix[:, None, None], cnt_te[None], 0), 1)
    base = pad_start[None, :] + tile_prefix
    ridx = route[:4].astype(jnp.int32).reshape(4, nt, tm)
    ex = jnp.arange(N_EXPERTS, dtype=jnp.int32)

    def slot_dest(eid, rank):
        return jnp.sum(jnp.where(eid[None] == ex[:, None, None], base.T[:, :, None], 0), 0) + rank

    dest3 = jnp.concatenate([slot_dest(ridx[0], ridx[2]), slot_dest(ridx[1], ridx[3])], -1).reshape(nt, 1, 2 * tm)
    n_blocks = (n * 2) // MOE_BLOCK + N_EXPERTS
    p_rows = n_blocks * MOE_BLOCK
    nb_e = padded // MOE_BLOCK
    items_e = (nb_e + ITEM_SUB - 1) // ITEM_SUB
    item_end = _cumsum_small(items_e)
    max_items = (n_blocks + (ITEM_SUB - 1) * N_EXPERTS) // ITEM_SUB
    it = jnp.arange(max_items + 1, dtype=jnp.int32)
    valid = it < item_end[-1]
    item_e = jnp.minimum(jnp.sum((it[:, None] >= item_end[None, :]).astype(jnp.int32), -1), N_EXPERTS - 1)
    item_j = it - _lookup(item_end - items_e, item_e)
    item_b = _lookup(pad_start // MOE_BLOCK, item_e) + ITEM_SUB * item_j
    item_ns = jnp.clip(_lookup(nb_e, item_e) - ITEM_SUB * item_j, 0, ITEM_SUB)
    item_ns = jnp.where(valid, item_ns, 0)
    item_b = jnp.where(valid, item_b, 0)
    flat_e = jnp.stack([ridx[0], ridx[1]], -1).reshape(-1)
    order = jnp.argsort(flat_e, stable=True).astype(jnp.int32)
    sorted_end = _cumsum_small(counts)
    u0 = jnp.where(valid, _lookup(sorted_end - counts, item_e) + ITEM_ROWS * item_j, 0)
    u_last = jnp.where(valid, _lookup(sorted_end, item_e) - 1, 0)
    u = jnp.minimum(u0[:, None] + jnp.arange(ITEM_ROWS, dtype=jnp.int32)[None, :], u_last[:, None])
    item_tok = (order[u] // 2).reshape(max_items + 1, 1, ITEM_ROWS)
    nonempty = (items_e > 0).astype(jnp.int32)
    ord_e = _cumsum_small(nonempty) - 1
    later = (ex[None, :] > ex[:, None]) & (nonempty[None, :] > 0)
    nxt_e = jnp.min(jnp.where(later, ex[None, :], N_EXPERTS), -1)
    nxt_e = jnp.where(nxt_e == N_EXPERTS, -1, nxt_e)
    item_ord = _lookup(ord_e, item_e)
    item_nxt = _lookup(nxt_e, item_e)

    items = [a.astype(jnp.int32) for a in (item_e, item_b, item_ns, item_ord, item_nxt, item_end[-1:])]
    yb = _expert_call(items, item_tok, t, exp_w_gate[0], exp_w_up[0], exp_w_down[0], p_rows)
    out = _combine_call(dest3, yb, x1, wts, g2, ln2_g[0][None, :], ln2_b[0][None, :], l // tm)
    return out.reshape(b, l, d)
```

```python
import math

import jax
import jax.numpy as jnp
from jax import lax
from jax.experimental import pallas as pl
from jax.experimental.pallas import tpu as pltpu

F32 = jnp.float32
BF16 = jnp.bfloat16

D_MODEL = 1024
GRID_W = 64
DA_HEAD_DIM = 64
DA_V_DIM = 128
DA_WIDTH = 512
DA_HEADS = 4
QK_COLS = 256
SG_CHUNK = 128
SG_WIDTH = 512
SG_GROUPS = 4
KV_LO = 512
KV_HI = 1536
IN_COLS = 2560
ROPE_THETA = 10000.0
MOE_GROUPS = 4
MOE_EXPERTS_PER_GROUP = 8
N_EXPERTS = 32
MOE_BLOCK = 128
EPS = 1e-5
DEPTH = 1
DEEPNORM_ALPHA = (2.0 * DEPTH) ** 0.25
LAM_INIT = 0.8 - 0.6 * math.exp(-0.3 * 0)
Q_SCALE = DA_HEAD_DIM ** -0.5
LOG2E = math.log2(math.e)
SQRT_HALF = math.sqrt(0.5)

LANES = 128
TOK_TILE = 512
ATTN_TQ = 2048
ATTN_SUB = 256
VMEM_LIMIT = 56 * 1024 * 1024


def _cparams(sem):
    return pltpu.CompilerParams(dimension_semantics=sem, vmem_limit_bytes=VMEM_LIMIT)


def _dot(a, b):
    return jnp.dot(a, b, preferred_element_type=F32)


def _dot_hi(a, b):
    return jnp.dot(a, b, preferred_element_type=F32, precision=lax.Precision.HIGHEST)


def _layer_norm(y, g, b):
    mu = jnp.mean(y, -1, keepdims=True)
    yc = y - mu
    var = jnp.mean(yc * yc, -1, keepdims=True)
    return yc * lax.rsqrt(var + EPS) * g + b


def _row(ref, n):
    return ref.at[pl.ds(n, 1)]


def _mod_kernel(c_ref, w_ref, b_ref, o_ref):
    c = c_ref[...]
    s = c * (1.0 / (1.0 + jnp.exp(-c)))
    o_ref[...] = _dot_hi(s, w_ref[...]) + b_ref[...]


def _mod_call(cc, w_mod, b_mod):
    rows = cc.shape[0]
    n_out = w_mod.shape[1]
    bn = 3072
    return pl.pallas_call(
        _mod_kernel,
        grid=(n_out // bn,),
        in_specs=[
            pl.BlockSpec((rows, D_MODEL), lambda j: (0, 0)),
            pl.BlockSpec((D_MODEL, bn), lambda j: (0, j)),
            pl.BlockSpec((1, bn), lambda j: (0, j)),
        ],
        out_specs=pl.BlockSpec((rows, bn), lambda j: (0, j)),
        out_shape=jax.ShapeDtypeStruct((rows, n_out), F32),
        compiler_params=_cparams(("arbitrary",)),
        name="mod",
    )(cc, w_mod, b_mod)


CTX_BATCH = 4


def _ctx_kernel(c_ref, sc_ref, sh_ref, w_ref, kc_ref, vc_ref):
    nb, cl, _ = c_ref.shape
    c = c_ref[...].reshape(nb * cl, D_MODEL)
    h = (c * (1.0 + sc_ref[...]) + sh_ref[...]).astype(BF16)
    p = _dot(h, w_ref[...])
    kc_ref[...] = p[:, :2 * QK_COLS].astype(BF16).reshape(nb, cl, 2 * QK_COLS)
    vc_ref[...] = p[:, 2 * QK_COLS:].astype(BF16).reshape(nb, cl, DA_WIDTH)


def _ctx_call(ctx, csc1, csh1, w_kv):
    b, cl, _ = ctx.shape
    nb = CTX_BATCH
    return pl.pallas_call(
        _ctx_kernel,
        grid=(b // nb,),
        in_specs=[
            pl.BlockSpec((nb, cl, D_MODEL), lambda i: (i, 0, 0)),
            pl.BlockSpec((1, D_MODEL), lambda i: (0, 0)),
            pl.BlockSpec((1, D_MODEL), lambda i: (0, 0)),
            pl.BlockSpec((D_MODEL, 2 * QK_COLS + DA_WIDTH), lambda i: (0, 0)),
        ],
        out_specs=[
            pl.BlockSpec((nb, cl, 2 * QK_COLS), lambda i: (i, 0, 0)),
            pl.BlockSpec((nb, cl, DA_WIDTH), lambda i: (i, 0, 0)),
        ],
        out_shape=[
            jax.ShapeDtypeStruct((b, cl, 2 * QK_COLS), BF16),
            jax.ShapeDtypeStruct((b, cl, DA_WIDTH), BF16),
        ],
        compiler_params=_cparams(("arbitrary",)),
        name="ctx_kv",
    )(ctx, csc1, csh1, w_kv)


def _in_kernel(x_ref, sc_ref, sh_ref, w_ref, cos_ref, sin_ref, lng_ref, lnb_ref,
               sw_ref, sb_ref, q_ref, k_ref, v_ref, sg_ref):
    tm = x_ref.shape[1]
    h = (x_ref[0] * (1.0 + sc_ref[0]) + sh_ref[0]).astype(BF16)
    p = _dot(h, w_ref[...])

    cos = cos_ref[...]
    sin = sin_ref[...]
    lane = lax.broadcasted_iota(jnp.int32, (tm, LANES), 1)
    first = (lane % 32) < 16

    def rope(t):
        partner = jnp.where(first, pltpu.roll(t, LANES - 16, 1), pltpu.roll(t, 16, 1))
        return t * cos + partner * sin

    for c in range(4):
        cs = slice(c * LANES, (c + 1) * LANES)
        q_ref[0, :, cs] = rope(p[:, cs] * (Q_SCALE * LOG2E)).astype(BF16)
        k_ref[0, :, cs] = rope(p[:, KV_LO + c * LANES:KV_LO + (c + 1) * LANES]).astype(BF16)
    v_ref[0] = p[:, 2 * KV_LO:KV_HI].astype(BF16)

    z = p[:, KV_HI:]
    gz = 0.5 * z * (1.0 + lax.erf(z * SQRT_HALF))
    u = gz[:, :SG_WIDTH]
    vn = _layer_norm(gz[:, SG_WIDTH:], lng_ref[...], lnb_ref[...]).astype(BF16)
    for c in range(tm // SG_CHUNK):
        rs = slice(c * SG_CHUNK, (c + 1) * SG_CHUNK)
        for g in range(SG_GROUPS):
            cs = slice(g * LANES, (g + 1) * LANES)
            s = _dot(sw_ref[g], vn[rs, cs]) + sb_ref[:, cs]
            sg_ref[0, rs, cs] = (u[rs, cs] * s).astype(BF16)


def _in_call(x, sc1, sh1, w_in, cos_t, sin_t, lng, lnb, sw, sbias):
    b, l, _ = x.shape
    tm = TOK_TILE
    full = lambda bi, i: (0, 0)
    return pl.pallas_call(
        _in_kernel,
        grid=(b, l // tm),
        in_specs=[
            pl.BlockSpec((1, tm, D_MODEL), lambda bi, i: (bi, i, 0)),
            pl.BlockSpec((1, 1, D_MODEL), lambda bi, i: (bi, 0, 0)),
            pl.BlockSpec((1, 1, D_MODEL), lambda bi, i: (bi, 0, 0)),
            pl.BlockSpec((D_MODEL, IN_COLS), full),
            pl.BlockSpec((tm, LANES), lambda bi, i: (i, 0)),
            pl.BlockSpec((tm, LANES), lambda bi, i: (i, 0)),
            pl.BlockSpec((1, SG_WIDTH), full),
            pl.BlockSpec((1, SG_WIDTH), full),
            pl.BlockSpec((SG_GROUPS, SG_CHUNK, SG_CHUNK), lambda bi, i: (0, 0, 0)),
            pl.BlockSpec((SG_CHUNK, SG_WIDTH), full),
        ],
        out_specs=[pl.BlockSpec((1, tm, 512), lambda bi, i: (bi, i, 0))] * 4,
        out_shape=[jax.ShapeDtypeStruct((b, l, 512), BF16)] * 4,
        compiler_params=_cparams(("arbitrary", "arbitrary")),
        name="in_proj",
    )(x, sc1, sh1, w_in, cos_t, sin_t, lng, lnb, sw, sbias)


def _attn_kernel(lam_ref, q_ref, k_ref, kc_ref, v_ref, vc_ref, g_ref, o_ref, kall, vall):
    lv = lam_ref[...]
    lam = (jnp.exp(jnp.sum(lv[0:1] * lv[1:2], -1, keepdims=True))
           - jnp.exp(jnp.sum(lv[2:3] * lv[3:4], -1, keepdims=True)) + LAM_INIT)
    l = k_ref.shape[1]

    @pl.when(pl.program_id(2) == 0)
    def _():
        kall[:l, :] = k_ref[0]
        kall[l:, :] = kc_ref[0]
        vall[:l, :DA_V_DIM] = v_ref[0]
        vall[l:, :DA_V_DIM] = vc_ref[0]
        vall[:, DA_V_DIM:] = jnp.ones((vall.shape[0], DA_V_DIM), BF16)

    nt = (((1,), (1,)), ((), ()))

    def branch(qm):
        s = lax.dot_general(qm, kall[...], nt, preferred_element_type=F32)
        m = jnp.max(s, -1, keepdims=True)
        e = jnp.exp2((s - m).astype(BF16))
        oe = _dot(e, vall[...])
        return oe[:, :DA_V_DIM] / oe[:, DA_V_DIM:DA_V_DIM + 1]

    for r in range(q_ref.shape[1] // ATTN_SUB):
        rs = pl.ds(r * ATTN_SUB, ATTN_SUB)
        q = q_ref[0, rs, :]
        lane = lax.broadcasted_iota(jnp.int32, q.shape, 1)
        zero = jnp.zeros_like(q)
        o = branch(jnp.where(lane < DA_HEAD_DIM, q, zero)) - lam * branch(jnp.where(lane >= DA_HEAD_DIM, q, zero))
        of = o * lax.rsqrt(jnp.mean(o * o, -1, keepdims=True) + EPS) * g_ref[...]
        o_ref[0, rs, :] = (of * (1.0 - LAM_INIT)).astype(BF16)


def _attn_call(lamv, q, k, kc, v, vc, subln_g):
    b, l, _ = q.shape
    cl = kc.shape[1]
    tq = ATTN_TQ
    return pl.pallas_call(
        _attn_kernel,
        grid=(b, DA_HEADS, l // tq),
        in_specs=[
            pl.BlockSpec((4, DA_HEAD_DIM), lambda bi, h, i: (0, 0)),
            pl.BlockSpec((1, tq, LANES), lambda bi, h, i: (bi, i, h)),
            pl.BlockSpec((1, l, LANES), lambda bi, h, i: (bi, 0, h)),
            pl.BlockSpec((1, cl, LANES), lambda bi, h, i: (bi, 0, h)),
            pl.BlockSpec((1, l, LANES), lambda bi, h, i: (bi, 0, h)),
            pl.BlockSpec((1, cl, LANES), lambda bi, h, i: (bi, 0, h)),
            pl.BlockSpec((1, DA_V_DIM), lambda bi, h, i: (0, 0)),
        ],
        out_specs=pl.BlockSpec((1, tq, LANES), lambda bi, h, i: (bi, i, h)),
        out_shape=jax.ShapeDtypeStruct((b, l, DA_WIDTH), BF16),
        scratch_shapes=[pltpu.VMEM((l + cl, LANES), BF16), pltpu.VMEM((l + cl, 2 * DA_V_DIM), BF16)],
        compiler_params=_cparams(("arbitrary", "arbitrary", "arbitrary")),
        name="attn",
    )(lamv, q, k, kc, v, vc, subln_g)


def _out_kernel(da_ref, sg_ref, x_ref, g1_ref, sc_ref, sh_ref, w_ref, lg_ref, lb_ref,
                wr_ref, br_ref, x1_ref, t_ref, route_ref, wts_ref, cnt_ref):
    tm = x_ref.shape[0]
    y = _dot(da_ref[...], w_ref[:DA_WIDTH, :]) + _dot(sg_ref[...], w_ref[DA_WIDTH:, :])
    x1 = _layer_norm(DEEPNORM_ALPHA * x_ref[...] + g1_ref[0] * y, lg_ref[...], lb_ref[...])
    x1_ref[...] = x1
    t = x1 * (1.0 + sc_ref[0]) + sh_ref[0]
    t_ref[...] = t

    t_hi = t.astype(BF16)
    t_lo = (t - t_hi.astype(F32)).astype(BF16)
    nt_dims = (((1,), (1,)), ((), ()))
    hw = lax.dot_general(wr_ref[...], t_hi, nt_dims, preferred_element_type=F32)
    lt = (hw[:LANES] + hw[LANES:] + lax.dot_general(wr_ref[:LANES, :], t_lo, nt_dims, preferred_element_type=F32)
          + br_ref[...])
    epg = MOE_EXPERTS_PER_GROUP
    row8 = lax.broadcasted_iota(jnp.int32, (epg, tm), 0).astype(F32)
    ninf = jnp.float32(-jnp.inf)
    big = jnp.float32(epg)
    gmask = row8 < MOE_GROUPS
    gl = jnp.where(gmask, lt[0:epg], ninf)
    gmax = jnp.max(gl, 0, keepdims=True)
    gsel = jnp.min(jnp.where(gl == gmax, row8, big), 0, keepdims=True)
    gsum = jnp.sum(jnp.where(gmask, jnp.exp(gl - gmax), 0.0), 0, keepdims=True)
    gw = 1.0 / gsum
    el = lt[epg:2 * epg]
    for g in range(1, MOE_GROUPS):
        el = jnp.where(gsel == g, lt[epg * (g + 1):epg * (g + 2)], el)
    v1 = jnp.max(el, 0, keepdims=True)
    i1 = jnp.min(jnp.where(el == v1, row8, big), 0, keepdims=True)
    el2 = jnp.where(row8 == i1, ninf, el)
    v2 = jnp.max(el2, 0, keepdims=True)
    i2 = jnp.min(jnp.where(el2 == v2, row8, big), 0, keepdims=True)
    e = jnp.exp(v2 - v1)
    w1 = gw / (1.0 + e)
    w2 = gw * e / (1.0 + e)
    e1 = gsel * epg + i1
    e2 = gsel * epg + i2

    rowe = lax.broadcasted_iota(jnp.int32, (N_EXPERTS, tm), 0).astype(F32)
    oh1 = rowe == e1
    oh2 = rowe == e2
    oh = jnp.where(oh1 | oh2, 1.0, 0.0)
    r_i = lax.broadcasted_iota(jnp.int32, (tm, tm), 0)
    c_i = lax.broadcasted_iota(jnp.int32, (tm, tm), 1)
    upper = jnp.where(r_i < c_i, 1.0, 0.0).astype(BF16)
    pref = _dot(oh.astype(BF16), upper)
    r1 = jnp.sum(jnp.where(oh1, pref, 0.0), 0, keepdims=True)
    r2 = jnp.sum(jnp.where(oh2, pref, 0.0), 0, keepdims=True)
    cnt_ref[0] = jnp.broadcast_to(jnp.sum(oh, 1, keepdims=True), (N_EXPERTS, LANES))

    route_ref[...] = jnp.where(row8 == 0, e1, jnp.where(row8 == 1, e2, jnp.where(row8 == 2, r1, jnp.where(row8 == 3, r2, 0.0))))
    rowl = lax.broadcasted_iota(jnp.int32, (LANES, tm), 0)
    wts_ref[...] = jnp.where(rowl < 64, w1, w2).T


def _out_call(da, sg, x2d, g1, sc2, sh2, w_out, lg, lb, wr, br, tiles_per_batch):
    n = x2d.shape[0]
    tm = TOK_TILE
    nt = n // tm
    tpb = tiles_per_batch
    row = lambda i: (i, 0)
    full = lambda i: (0, 0)
    per_b = lambda i: (i // tpb, 0, 0)
    return pl.pallas_call(
        _out_kernel,
        grid=(nt,),
        in_specs=[
            pl.BlockSpec((tm, DA_WIDTH), row),
            pl.BlockSpec((tm, SG_WIDTH), row),
            pl.BlockSpec((tm, D_MODEL), row),
            pl.BlockSpec((1, 1, D_MODEL), per_b),
            pl.BlockSpec((1, 1, D_MODEL), per_b),
            pl.BlockSpec((1, 1, D_MODEL), per_b),
            pl.BlockSpec((D_MODEL, D_MODEL), full),
            pl.BlockSpec((1, D_MODEL), full),
            pl.BlockSpec((1, D_MODEL), full),
            pl.BlockSpec((2 * LANES, D_MODEL), full),
            pl.BlockSpec((LANES, tm), full),
        ],
        out_specs=[
            pl.BlockSpec((tm, D_MODEL), row),
            pl.BlockSpec((tm, D_MODEL), row),
            pl.BlockSpec((8, tm), lambda i: (0, i)),
            pl.BlockSpec((tm, LANES), row),
            pl.BlockSpec((1, N_EXPERTS, LANES), lambda i: (i, 0, 0)),
        ],
        out_shape=[
            jax.ShapeDtypeStruct((n, D_MODEL), F32),
            jax.ShapeDtypeStruct((n, D_MODEL), F32),
            jax.ShapeDtypeStruct((8, n), F32),
            jax.ShapeDtypeStruct((n, LANES), F32),
            jax.ShapeDtypeStruct((nt, N_EXPERTS, LANES), F32),
        ],
        compiler_params=_cparams(("arbitrary",)),
        name="out_proj",
    )(da, sg, x2d, g1, sc2, sh2, w_out, lg, lb, wr, br)


ISSUE_UNROLL = 8


def _dispatch_kernel(zs_ref, ze_ref, dest_ref, t_ref, xb_ref, zero_ref, sem):
    i = pl.program_id(0)
    tm = t_ref.shape[0]

    @pl.when(i == 0)
    def _():
        zero_ref[...] = jnp.zeros_like(zero_ref)

        def per_expert(e, carry):
            def start(r, c):
                pltpu.make_async_copy(_row(zero_ref, 0), _row(xb_ref, r), sem).start()
                return c

            def wait(r, c):
                pltpu.make_async_copy(_row(zero_ref, 0), _row(xb_ref, r), sem).wait()
                return c

            lax.fori_loop(zs_ref[e], ze_ref[e], start, 0)
            lax.fori_loop(zs_ref[e], ze_ref[e], wait, 0)
            return carry

        lax.fori_loop(0, N_EXPERTS, per_expert, 0)

        def blk_copy(j):
            return pltpu.make_async_copy(zero_ref, xb_ref.at[pl.ds(pl.multiple_of(j * MOE_BLOCK, MOE_BLOCK), MOE_BLOCK)], sem)

        def blk_start(j, c):
            blk_copy(j).start()
            return c

        def blk_wait(j, c):
            blk_copy(j).wait()
            return c

        first_unused = ze_ref[N_EXPERTS - 1] // MOE_BLOCK
        n_blocks = xb_ref.shape[0] // MOE_BLOCK
        lax.fori_loop(first_unused, n_blocks, blk_start, 0)
        lax.fori_loop(first_unused, n_blocks, blk_wait, 0)

    def start(n, c):
        pltpu.make_async_copy(_row(t_ref, n), _row(xb_ref, dest_ref[0, 0, n]), sem).start()
        pltpu.make_async_copy(_row(t_ref, n), _row(xb_ref, dest_ref[0, 0, tm + n]), sem).start(priority=1)
        return c

    lax.fori_loop(0, tm, start, 0, unroll=ISSUE_UNROLL)
    for _ in range(2):
        pltpu.make_async_copy(t_ref, xb_ref.at[pl.ds(0, tm)], sem).wait()


def _dispatch_call(zs, ze, dest3, t, p_rows):
    n = t.shape[0]
    tm = TOK_TILE
    grid_spec = pltpu.PrefetchScalarGridSpec(
        num_scalar_prefetch=2,
        grid=(n // tm,),
        in_specs=[
            pl.BlockSpec((1, 1, 2 * tm), lambda i, zs, ze: (i, 0, 0), memory_space=pltpu.SMEM),
            pl.BlockSpec((tm, D_MODEL), lambda i, zs, ze: (i, 0)),
        ],
        out_specs=pl.BlockSpec(memory_space=pl.ANY),
        scratch_shapes=[pltpu.VMEM((MOE_BLOCK, D_MODEL), F32), pltpu.SemaphoreType.DMA(())],
    )
    return pl.pallas_call(
        _dispatch_kernel,
        grid_spec=grid_spec,
        out_shape=jax.ShapeDtypeStruct((p_rows, D_MODEL), F32),
        compiler_params=_cparams(("arbitrary",)),
        name="dispatch",
    )(zs, ze, dest3, t)


ITEM_SUB = 6
ITEM_ROWS = ITEM_SUB * MOE_BLOCK


def _expert_kernel(ie_ref, ib_ref, ins_ref, iord_ref, inxt_ref, ni_ref, xb_ref, wg_ref, wu_ref, wd_ref, yb_ref,
                   xbuf, ybuf, wf32, wgb, wub, wdb, insem, outsem, wsem):
    i = pl.program_id(0)
    n_steps = pl.num_programs(0)
    ni = ni_ref[0]
    slot = i % 2

    def w_copies(e, s):
        return [pltpu.make_async_copy(w.at[e], wf32.at[s, k], wsem.at[s]) for k, w in enumerate((wg_ref, wu_ref, wd_ref))]

    def rows_of(item, j):
        return pl.ds(pl.multiple_of((ib_ref[item] + j) * MOE_BLOCK, MOE_BLOCK), MOE_BLOCK)

    def in_copy(item, s, j):
        return pltpu.make_async_copy(xb_ref.at[rows_of(item, j)], xbuf.at[s, pl.ds(j * MOE_BLOCK, MOE_BLOCK)], insem.at[s])

    def out_copy(item, s, j):
        return pltpu.make_async_copy(ybuf.at[s, pl.ds(j * MOE_BLOCK, MOE_BLOCK)], yb_ref.at[rows_of(item, j)], outsem.at[s])

    def for_blocks(item, fn):
        for j in range(ITEM_SUB):
            @pl.when(j < ins_ref[item])
            def _():
                fn(j)

    @pl.when(i == 0)
    def _():
        for c in w_copies(ie_ref[0], 0):
            c.start()
        for_blocks(0, lambda j: in_copy(0, 0, j).start())

    @pl.when(i < ni)
    def _():
        @pl.when(i >= 2)
        def _():
            for_blocks(i - 2, lambda j: out_copy(i - 2, slot, j).wait())

        @pl.when(i + 1 < ni)
        def _():
            for_blocks(i + 1, lambda j: in_copy(i + 1, 1 - slot, j).start())

        changed = (i == 0) | (ie_ref[i] != ie_ref[jnp.maximum(i - 1, 0)])

        @pl.when(changed)
        def _():
            ws = iord_ref[i] % 2
            for c in w_copies(ie_ref[i], ws):
                c.wait()

            @pl.when(inxt_ref[i] >= 0)
            def _():
                for c in w_copies(inxt_ref[i], 1 - ws):
                    c.start()

            wgb[...] = wf32[ws, 0].astype(BF16)
            wub[...] = wf32[ws, 1].astype(BF16)
            wdb[...] = wf32[ws, 2].astype(BF16)

        for_blocks(i, lambda j: in_copy(i, slot, j).wait())

        for ns in range(1, ITEM_SUB + 1):
            @pl.when(ins_ref[i] == ns)
            def _():
                rows = pl.ds(0, ns * MOE_BLOCK)
                x = xbuf[slot, rows].astype(BF16)
                g = _dot(x, wgb[...])
                u = _dot(x, wub[...])
                hid = (g * (1.0 / (1.0 + jnp.exp(-g))) * u).astype(BF16)
                ybuf[slot, rows] = _dot(hid, wdb[...])

        for_blocks(i, lambda j: out_copy(i, slot, j).start())

    @pl.when(i == n_steps - 1)
    def _():
        for back in (2, 1):
            k = ni - back

            @pl.when(k >= 0)
            def _():
                for_blocks(k, lambda j: out_copy(k, k % 2, j).wait())

        ybuf[0, pl.ds(0, MOE_BLOCK)] = jnp.zeros((MOE_BLOCK, D_MODEL), F32)
        last = jnp.maximum(ni - 1, 0)
        first_unused = ib_ref[last] + ins_ref[last]
        n_blocks = yb_ref.shape[0] // MOE_BLOCK

        def zero_copy(b):
            dst = yb_ref.at[pl.ds(pl.multiple_of(b * MOE_BLOCK, MOE_BLOCK), MOE_BLOCK)]
            return pltpu.make_async_copy(ybuf.at[0, pl.ds(0, MOE_BLOCK)], dst, outsem.at[0])

        def z_start(b, c):
            zero_copy(b).start()
            return c

        def z_wait(b, c):
            zero_copy(b).wait()
            return c

        lax.fori_loop(first_unused, n_blocks, z_start, 0)
        lax.fori_loop(first_unused, n_blocks, z_wait, 0)


def _expert_call(item_e, item_b, item_ns, item_ord, item_nxt, n_items, xb, wg, wu, wd):
    p_rows = xb.shape[0]
    max_items = item_e.shape[0]
    grid_spec = pltpu.PrefetchScalarGridSpec(
        num_scalar_prefetch=6,
        grid=(max_items,),
        in_specs=[pl.BlockSpec(memory_space=pl.ANY)] * 4,
        out_specs=pl.BlockSpec(memory_space=pl.ANY),
        scratch_shapes=[
            pltpu.VMEM((2, ITEM_ROWS, D_MODEL), F32),
            pltpu.VMEM((2, ITEM_ROWS, D_MODEL), F32),
            pltpu.VMEM((2, 3, D_MODEL, D_MODEL), F32),
            pltpu.VMEM((D_MODEL, D_MODEL), BF16),
            pltpu.VMEM((D_MODEL, D_MODEL), BF16),
            pltpu.VMEM((D_MODEL, D_MODEL), BF16),
            pltpu.SemaphoreType.DMA((2,)),
            pltpu.SemaphoreType.DMA((2,)),
            pltpu.SemaphoreType.DMA((2,)),
        ],
    )
    return pl.pallas_call(
        _expert_kernel,
        grid_spec=grid_spec,
        out_shape=jax.ShapeDtypeStruct((p_rows, D_MODEL), F32),
        compiler_params=_cparams(("arbitrary",)),
        name="experts",
    )(item_e, item_b, item_ns, item_ord, item_nxt, n_items, xb, wg, wu, wd)


def _combine_kernel(dcur_ref, dnxt_ref, yb_ref, x1_ref, wts_ref, g2_ref, lg_ref, lb_ref, o_ref, buf, sem):
    i = pl.program_id(0)
    tm = x1_ref.shape[0]
    slot = i % 2

    def issue(dref, s):
        def start(n, c):
            pltpu.make_async_copy(_row(yb_ref, dref[0, 0, n]), _row(buf.at[s, 0], n), sem.at[s]).start()
            pltpu.make_async_copy(_row(yb_ref, dref[0, 0, tm + n]), _row(buf.at[s, 1], n), sem.at[s]).start(priority=1)
            return c

        lax.fori_loop(0, tm, start, 0, unroll=ISSUE_UNROLL)

    @pl.when(i == 0)
    def _():
        issue(dcur_ref, 0)

    @pl.when(i + 1 < pl.num_programs(0))
    def _():
        issue(dnxt_ref, 1 - slot)

    for k in range(2):
        pltpu.make_async_copy(yb_ref.at[pl.ds(0, tm)], buf.at[slot, k], sem.at[slot]).wait()
    w = wts_ref[...]
    f = w[:, 0:1] * buf[slot, 0] + w[:, 64:65] * buf[slot, 1]
    o_ref[...] = _layer_norm(DEEPNORM_ALPHA * x1_ref[...] + g2_ref[0] * f, lg_ref[...], lb_ref[...])


def _combine_call(dest3, yb, x1, wts, g2, lg, lb, tiles_per_batch):
    n = x1.shape[0]
    tm = TOK_TILE
    tpb = tiles_per_batch
    row = lambda i: (i, 0)
    full = lambda i: (0, 0)
    nt = n // tm
    return pl.pallas_call(
        _combine_kernel,
        grid=(nt,),
        in_specs=[
            pl.BlockSpec((1, 1, 2 * tm), lambda i: (i, 0, 0), memory_space=pltpu.SMEM),
            pl.BlockSpec((1, 1, 2 * tm), lambda i: (jnp.minimum(i + 1, nt - 1), 0, 0), memory_space=pltpu.SMEM),
            pl.BlockSpec(memory_space=pl.ANY),
            pl.BlockSpec((tm, D_MODEL), row),
            pl.BlockSpec((tm, LANES), row),
            pl.BlockSpec((1, 1, D_MODEL), lambda i: (i // tpb, 0, 0)),
            pl.BlockSpec((1, D_MODEL), full),
            pl.BlockSpec((1, D_MODEL), full),
        ],
        out_specs=pl.BlockSpec((tm, D_MODEL), row),
        out_shape=jax.ShapeDtypeStruct((n, D_MODEL), F32),
        scratch_shapes=[pltpu.VMEM((2, 2, tm, D_MODEL), F32), pltpu.SemaphoreType.DMA((2,))],
        compiler_params=_cparams(("arbitrary",)),
        name="combine",
    )(dest3, dest3, yb, x1, wts, g2, lg, lb)


def _head_interleave(w, lo):
    blk = w[:, lo:lo + 2 * QK_COLS].reshape(D_MODEL, 2, DA_HEADS, DA_HEAD_DIM)
    return blk.transpose(0, 2, 1, 3).reshape(D_MODEL, 2 * QK_COLS)


def _cumsum_small(x):
    idx = jnp.arange(x.shape[0], dtype=jnp.int32)
    return jnp.sum(jnp.where(idx[None, :] <= idx[:, None], x[None, :], 0), -1)


def _lookup(table, idx):
    return jnp.sum(jnp.where(idx[:, None] == jnp.arange(table.shape[0], dtype=jnp.int32)[None, :], table[None, :], 0), -1)


def _rope_tables(seq):
    rows_n = seq // GRID_W
    rows = jnp.repeat(jnp.arange(rows_n, dtype=F32), GRID_W)
    cols = jnp.tile(jnp.arange(GRID_W, dtype=F32), rows_n)
    half = DA_HEAD_DIM // 4
    inv = ROPE_THETA ** (-jnp.arange(half, dtype=F32) / half)
    ang_r = rows[:, None] * inv[None, :]
    ang_c = cols[:, None] * inv[None, :]
    cos64 = jnp.concatenate([jnp.cos(ang_r), jnp.cos(ang_r), jnp.cos(ang_c), jnp.cos(ang_c)], -1)
    sin64 = jnp.concatenate([-jnp.sin(ang_r), jnp.sin(ang_r), -jnp.sin(ang_c), jnp.sin(ang_c)], -1)
    return jnp.tile(cos64, (1, 2)), jnp.tile(sin64, (1, 2))


def kernel(x, c, ctx, c_ctx, w_mod, b_mod, w_in, lam_q1, lam_k1, lam_q2, lam_k2, subln_g, sg_ln_g, sg_ln_b, sg_w, sg_b, w_out, ln1_g, ln1_b, router_group_w, router_group_b, router_expert_w, router_expert_b, exp_w_gate, exp_w_up, exp_w_down, ln2_g, ln2_b):
    b, l, d = x.shape
    n = b * l
    tm = TOK_TILE
    nt = n // tm

    cc = jnp.zeros((b + 8, d), F32).at[:b].set(c).at[b].set(c_ctx)
    mod = _mod_call(cc, w_mod[0], b_mod[0][None, :])
    sh1, sc1, g1, sh2, sc2, g2 = [mod[:b, j * d:(j + 1) * d].reshape(b, 1, d) for j in range(6)]
    csh1 = mod[b:b + 1, 0:d]
    csc1 = mod[b:b + 1, d:2 * d]

    wi = w_in[0]
    w_all = jnp.concatenate([_head_interleave(wi, 0), _head_interleave(wi, KV_LO), wi[:, 2 * KV_LO:]], -1).astype(BF16)
    kc, vc = _ctx_call(ctx, csc1, csh1, w_all[:, KV_LO:KV_HI])

    cos_t, sin_t = _rope_tables(l)
    sbias = jnp.repeat(sg_b[0].T, LANES, axis=1)
    q, k, v, sg = _in_call(x, sc1, sh1, w_all, cos_t, sin_t, sg_ln_g[0][None, :], sg_ln_b[0][None, :],
                           sg_w[0].astype(BF16), sbias)

    lamv = jnp.stack([lam_q1[0], lam_k1[0], lam_q2[0], lam_k2[0]]).astype(F32)
    da = _attn_call(lamv, q, k, kc, v, vc, subln_g[0][None, :])

    e_lo = MOE_EXPERTS_PER_GROUP
    wr = jnp.zeros((LANES, d), F32).at[:MOE_GROUPS].set(router_group_w[0].T).at[e_lo:e_lo + N_EXPERTS].set(router_expert_w[0].T)
    br = jnp.zeros((LANES,), F32).at[:MOE_GROUPS].set(router_group_b[0]).at[e_lo:e_lo + N_EXPERTS].set(router_expert_b[0])
    br = jnp.broadcast_to(br[:, None], (LANES, tm))
    wr_hi = wr.astype(BF16)
    wr_split = jnp.concatenate([wr_hi, (wr - wr_hi.astype(F32)).astype(BF16)], 0)
    x1, t, route, wts, tcnt = _out_call(da.reshape(n, DA_WIDTH), sg.reshape(n, SG_WIDTH), x.reshape(n, d), g1, sc2, sh2,
                                        w_out[0].astype(BF16), ln1_g[0][None, :], ln1_b[0][None, :], wr_split, br, l // tm)

    cnt_te = tcnt[:, :, 0].astype(jnp.int32)
    counts = jnp.sum(cnt_te, 0)
    padded = (counts + MOE_BLOCK - 1) // MOE_BLOCK * MOE_BLOCK
    pad_end = _cumsum_small(padded)
    pad_start = pad_end - padded
    tix = jnp.arange(nt, dtype=jnp.int32)
    tile_prefix = jnp.sum(jnp.where(tix[None, :, None] < tix[:, None, None], cnt_te[None], 0), 1)
    base = pad_start[None, :] + tile_prefix
    ridx = route[:4].astype(jnp.int32).reshape(4, nt, tm)
    ex = jnp.arange(N_EXPERTS, dtype=jnp.int32)

    def slot_dest(eid, rank):
        return jnp.sum(jnp.where(eid[None] == ex[:, None, None], base.T[:, :, None], 0), 0) + rank

    dest3 = jnp.concatenate([slot_dest(ridx[0], ridx[2]), slot_dest(ridx[1], ridx[3])], -1).reshape(nt, 1, 2 * tm)
    n_blocks = (n * 2) // MOE_BLOCK + N_EXPERTS
    p_rows = n_blocks * MOE_BLOCK
    nb_e = padded // MOE_BLOCK
    items_e = (nb_e + ITEM_SUB - 1) // ITEM_SUB
    item_end = _cumsum_small(items_e)
    max_items = (n_blocks + (ITEM_SUB - 1) * N_EXPERTS) // ITEM_SUB
    it = jnp.arange(max_items, dtype=jnp.int32)
    item_e = jnp.minimum(jnp.sum((it[:, None] >= item_end[None, :]).astype(jnp.int32), -1), N_EXPERTS - 1)
    item_j = it - _lookup(item_end - items_e, item_e)
    item_b = _lookup(pad_start // MOE_BLOCK, item_e) + ITEM_SUB * item_j
    item_ns = jnp.clip(_lookup(nb_e, item_e) - ITEM_SUB * item_j, 0, ITEM_SUB)
    nonempty = (items_e > 0).astype(jnp.int32)
    ord_e = _cumsum_small(nonempty) - 1
    later = (ex[None, :] > ex[:, None]) & (nonempty[None, :] > 0)
    nxt_e = jnp.min(jnp.where(later, ex[None, :], N_EXPERTS), -1)
    nxt_e = jnp.where(nxt_e == N_EXPERTS, -1, nxt_e)
    item_ord = _lookup(ord_e, item_e)
    item_nxt = _lookup(nxt_e, item_e)

    xb = _dispatch_call((pad_start + counts).astype(jnp.int32), pad_end.astype(jnp.int32), dest3, t, p_rows)
    yb = _expert_call(item_e, item_b.astype(jnp.int32), item_ns.astype(jnp.int32), item_ord.astype(jnp.int32),
                      item_nxt.astype(jnp.int32), item_end[-1:].astype(jnp.int32),
                      xb, exp_w_gate[0], exp_w_up[0], exp_w_down[0])
    out = _combine_call(dest3, yb, x1, wts, g2, ln2_g[0][None, :], ln2_b[0][None, :], l // tm)
    return out.reshape(b, l, d)
```

```python
import math

import jax
import jax.numpy as jnp
from jax import lax
from jax.experimental import pallas as pl
from jax.experimental.pallas import tpu as pltpu

F32 = jnp.float32
BF16 = jnp.bfloat16

D_MODEL = 1024
GRID_W = 64
DA_HEAD_DIM = 64
DA_V_DIM = 128
DA_WIDTH = 512
DA_HEADS = 4
QK_COLS = 256
SG_CHUNK = 128
SG_WIDTH = 512
SG_GROUPS = 4
KV_LO = 512
KV_HI = 1536
IN_COLS = 2560
ROPE_THETA = 10000.0
MOE_GROUPS = 4
MOE_EXPERTS_PER_GROUP = 8
N_EXPERTS = 32
MOE_BLOCK = 128
EPS = 1e-5
DEPTH = 1
DEEPNORM_ALPHA = (2.0 * DEPTH) ** 0.25
LAM_INIT = 0.8 - 0.6 * math.exp(-0.3 * 0)
Q_SCALE = DA_HEAD_DIM ** -0.5
LOG2E = math.log2(math.e)
SQRT_HALF = math.sqrt(0.5)

LANES = 128
TOK_TILE = 512
IN_SUB = 512
ATTN_TQ = 2048
ATTN_SUB = 256
VMEM_LIMIT = 56 * 1024 * 1024


def _cparams(sem):
    return pltpu.CompilerParams(dimension_semantics=sem, vmem_limit_bytes=VMEM_LIMIT)


def _dot(a, b):
    return jnp.dot(a, b, preferred_element_type=F32)


def _dot_hi(a, b):
    return jnp.dot(a, b, preferred_element_type=F32, precision=lax.Precision.HIGHEST)


def _layer_norm(y, g, b):
    mu = jnp.mean(y, -1, keepdims=True)
    yc = y - mu
    var = jnp.mean(yc * yc, -1, keepdims=True)
    return yc * lax.rsqrt(var + EPS) * g + b


def _row(ref, n):
    return ref.at[pl.ds(n, 1)]


def _mod_kernel(c_ref, w_ref, b_ref, o_ref):
    c = c_ref[...]
    s = c * (1.0 / (1.0 + jnp.exp(-c)))
    o_ref[...] = _dot_hi(s, w_ref[...]) + b_ref[...]


def _mod_call(cc, w_mod, b_mod):
    rows = cc.shape[0]
    n_out = w_mod.shape[1]
    bn = 1024
    return pl.pallas_call(
        _mod_kernel,
        grid=(n_out // bn,),
        in_specs=[
            pl.BlockSpec((rows, D_MODEL), lambda j: (0, 0)),
            pl.BlockSpec((D_MODEL, bn), lambda j: (0, j)),
            pl.BlockSpec((1, bn), lambda j: (0, j)),
        ],
        out_specs=pl.BlockSpec((rows, bn), lambda j: (0, j)),
        out_shape=jax.ShapeDtypeStruct((rows, n_out), F32),
        compiler_params=_cparams(("arbitrary",)),
        name="mod",
    )(cc, w_mod, b_mod)


CTX_BATCH = 4


def _ctx_kernel(c_ref, sc_ref, sh_ref, w_ref, kc_ref, vc_ref):
    nb, cl, _ = c_ref.shape
    c = c_ref[...].reshape(nb * cl, D_MODEL)
    h = (c * (1.0 + sc_ref[...]) + sh_ref[...]).astype(BF16)
    p = _dot(h, w_ref[...])
    kc_ref[...] = p[:, :2 * QK_COLS].astype(BF16).reshape(nb, cl, 2 * QK_COLS)
    vc_ref[...] = p[:, 2 * QK_COLS:].astype(BF16).reshape(nb, cl, DA_WIDTH)


def _ctx_call(ctx, csc1, csh1, w_kv):
    b, cl, _ = ctx.shape
    nb = CTX_BATCH
    return pl.pallas_call(
        _ctx_kernel,
        grid=(b // nb,),
        in_specs=[
            pl.BlockSpec((nb, cl, D_MODEL), lambda i: (i, 0, 0)),
            pl.BlockSpec((1, D_MODEL), lambda i: (0, 0)),
            pl.BlockSpec((1, D_MODEL), lambda i: (0, 0)),
            pl.BlockSpec((D_MODEL, 2 * QK_COLS + DA_WIDTH), lambda i: (0, 0)),
        ],
        out_specs=[
            pl.BlockSpec((nb, cl, 2 * QK_COLS), lambda i: (i, 0, 0)),
            pl.BlockSpec((nb, cl, DA_WIDTH), lambda i: (i, 0, 0)),
        ],
        out_shape=[
            jax.ShapeDtypeStruct((b, cl, 2 * QK_COLS), BF16),
            jax.ShapeDtypeStruct((b, cl, DA_WIDTH), BF16),
        ],
        compiler_params=_cparams(("arbitrary",)),
        name="ctx_kv",
    )(ctx, csc1, csh1, w_kv)


def _in_kernel(x_ref, sc_ref, sh_ref, w_ref, cos_ref, sin_ref, lng_ref, lnb_ref,
               sw_ref, sb_ref, q_ref, k_ref, v_ref, sg_ref):
    tm = x_ref.shape[1]
    sub = IN_SUB
    lane = lax.broadcasted_iota(jnp.int32, (sub, LANES), 1)
    first = (lane % 32) < 16

    for r0 in range(0, tm, sub):
        rr = pl.ds(r0, sub)
        h = (x_ref[0, rr, :] * (1.0 + sc_ref[0]) + sh_ref[0]).astype(BF16)
        p = _dot(h, w_ref[...])

        cos = cos_ref[rr, :]
        sin = sin_ref[rr, :]

        def rope(t):
            partner = jnp.where(first, pltpu.roll(t, LANES - 16, 1), pltpu.roll(t, 16, 1))
            return t * cos + partner * sin

        for c in range(4):
            cs = slice(c * LANES, (c + 1) * LANES)
            q_ref[0, rr, cs] = rope(p[:, cs] * (Q_SCALE * LOG2E)).astype(BF16)
            k_ref[0, rr, cs] = rope(p[:, KV_LO + c * LANES:KV_LO + (c + 1) * LANES]).astype(BF16)
        v_ref[0, rr, :] = p[:, 2 * KV_LO:KV_HI].astype(BF16)

        z = p[:, KV_HI:]
        gz = 0.5 * z * (1.0 + lax.erf(z * SQRT_HALF))
        u = gz[:, :SG_WIDTH]
        vn = _layer_norm(gz[:, SG_WIDTH:], lng_ref[...], lnb_ref[...]).astype(BF16)
        for c in range(sub // SG_CHUNK):
            rs = slice(c * SG_CHUNK, (c + 1) * SG_CHUNK)
            for g in range(SG_GROUPS):
                cs = slice(g * LANES, (g + 1) * LANES)
                s = _dot(sw_ref[g], vn[rs, cs]) + sb_ref[:, cs]
                sg_ref[0, pl.ds(r0 + c * SG_CHUNK, SG_CHUNK), cs] = (u[rs, cs] * s).astype(BF16)


def _in_call(x, sc1, sh1, w_in, cos_t, sin_t, lng, lnb, sw, sbias):
    b, l, _ = x.shape
    tm = 2 * TOK_TILE
    full = lambda bi, i: (0, 0)
    return pl.pallas_call(
        _in_kernel,
        grid=(b, l // tm),
        in_specs=[
            pl.BlockSpec((1, tm, D_MODEL), lambda bi, i: (bi, i, 0)),
            pl.BlockSpec((1, 1, D_MODEL), lambda bi, i: (bi, 0, 0)),
            pl.BlockSpec((1, 1, D_MODEL), lambda bi, i: (bi, 0, 0)),
            pl.BlockSpec((D_MODEL, IN_COLS), full),
            pl.BlockSpec((tm, LANES), lambda bi, i: (i, 0)),
            pl.BlockSpec((tm, LANES), lambda bi, i: (i, 0)),
            pl.BlockSpec((1, SG_WIDTH), full),
            pl.BlockSpec((1, SG_WIDTH), full),
            pl.BlockSpec((SG_GROUPS, SG_CHUNK, SG_CHUNK), lambda bi, i: (0, 0, 0)),
            pl.BlockSpec((SG_CHUNK, SG_WIDTH), full),
        ],
        out_specs=[pl.BlockSpec((1, tm, 512), lambda bi, i: (bi, i, 0))] * 4,
        out_shape=[jax.ShapeDtypeStruct((b, l, 512), BF16)] * 4,
        compiler_params=_cparams(("arbitrary", "arbitrary")),
        name="in_proj",
    )(x, sc1, sh1, w_in, cos_t, sin_t, lng, lnb, sw, sbias)


def _attn_kernel(lam_ref, q_ref, k_ref, kc_ref, v_ref, vc_ref, g_ref, o_ref, kall, vall):
    lv = lam_ref[...]
    lam = (jnp.exp(jnp.sum(lv[0:1] * lv[1:2], -1, keepdims=True))
           - jnp.exp(jnp.sum(lv[2:3] * lv[3:4], -1, keepdims=True)) + LAM_INIT)
    l = k_ref.shape[1]

    @pl.when(pl.program_id(2) == 0)
    def _():
        kall[:l, :] = k_ref[0]
        kall[l:, :] = kc_ref[0]
        vall[:l, :DA_V_DIM] = v_ref[0]
        vall[l:, :DA_V_DIM] = vc_ref[0]
        vall[:, DA_V_DIM:] = jnp.ones((vall.shape[0], DA_V_DIM), BF16)

    nt = (((1,), (1,)), ((), ()))

    def branch(qm):
        s = lax.dot_general(qm, kall[...], nt, preferred_element_type=F32)
        m = jnp.max(s, -1, keepdims=True)
        e = jnp.exp2((s - m).astype(BF16))
        oe = _dot(e, vall[...])
        return oe[:, :DA_V_DIM] / oe[:, DA_V_DIM:DA_V_DIM + 1]

    for r in range(q_ref.shape[1] // ATTN_SUB):
        rs = pl.ds(r * ATTN_SUB, ATTN_SUB)
        q = q_ref[0, rs, :]
        lane = lax.broadcasted_iota(jnp.int32, q.shape, 1)
        zero = jnp.zeros_like(q)
        o = branch(jnp.where(lane < DA_HEAD_DIM, q, zero)) - lam * branch(jnp.where(lane >= DA_HEAD_DIM, q, zero))
        of = o * lax.rsqrt(jnp.mean(o * o, -1, keepdims=True) + EPS) * g_ref[...]
        o_ref[0, rs, :] = (of * (1.0 - LAM_INIT)).astype(BF16)


def _attn_call(lamv, q, k, kc, v, vc, subln_g):
    b, l, _ = q.shape
    cl = kc.shape[1]
    tq = ATTN_TQ
    return pl.pallas_call(
        _attn_kernel,
        grid=(b, DA_HEADS, l // tq),
        in_specs=[
            pl.BlockSpec((4, DA_HEAD_DIM), lambda bi, h, i: (0, 0)),
            pl.BlockSpec((1, tq, LANES), lambda bi, h, i: (bi, i, h)),
            pl.BlockSpec((1, l, LANES), lambda bi, h, i: (bi, 0, h)),
            pl.BlockSpec((1, cl, LANES), lambda bi, h, i: (bi, 0, h)),
            pl.BlockSpec((1, l, LANES), lambda bi, h, i: (bi, 0, h)),
            pl.BlockSpec((1, cl, LANES), lambda bi, h, i: (bi, 0, h)),
            pl.BlockSpec((1, DA_V_DIM), lambda bi, h, i: (0, 0)),
        ],
        out_specs=pl.BlockSpec((1, tq, LANES), lambda bi, h, i: (bi, i, h)),
        out_shape=jax.ShapeDtypeStruct((b, l, DA_WIDTH), BF16),
        scratch_shapes=[pltpu.VMEM((l + cl, LANES), BF16), pltpu.VMEM((l + cl, 2 * DA_V_DIM), BF16)],
        compiler_params=_cparams(("arbitrary", "arbitrary", "arbitrary")),
        name="attn",
    )(lamv, q, k, kc, v, vc, subln_g)


def _out_kernel(da_ref, sg_ref, x_ref, g1_ref, sc_ref, sh_ref, w_ref, lg_ref, lb_ref,
                wr_ref, br_ref, x1_ref, t_ref, route_ref, wts_ref, cnt_ref):
    tm = x_ref.shape[0]
    y = _dot(da_ref[...], w_ref[:DA_WIDTH, :]) + _dot(sg_ref[...], w_ref[DA_WIDTH:, :])
    x1 = _layer_norm(DEEPNORM_ALPHA * x_ref[...] + g1_ref[0] * y, lg_ref[...], lb_ref[...])
    x1_ref[...] = x1
    t = x1 * (1.0 + sc_ref[0]) + sh_ref[0]
    t_ref[...] = t

    t_hi = t.astype(BF16)
    t_lo = (t - t_hi.astype(F32)).astype(BF16)
    nt_dims = (((1,), (1,)), ((), ()))
    hw = lax.dot_general(wr_ref[...], t_hi, nt_dims, preferred_element_type=F32)
    lt = (hw[:LANES] + hw[LANES:] + lax.dot_general(wr_ref[:LANES, :], t_lo, nt_dims, preferred_element_type=F32)
          + br_ref[...])
    epg = MOE_EXPERTS_PER_GROUP
    row8 = lax.broadcasted_iota(jnp.int32, (epg, tm), 0).astype(F32)
    ninf = jnp.float32(-jnp.inf)
    big = jnp.float32(epg)
    gmask = row8 < MOE_GROUPS
    gl = jnp.where(gmask, lt[0:epg], ninf)
    gmax = jnp.max(gl, 0, keepdims=True)
    gsel = jnp.min(jnp.where(gl == gmax, row8, big), 0, keepdims=True)
    gsum = jnp.sum(jnp.where(gmask, jnp.exp(gl - gmax), 0.0), 0, keepdims=True)
    gw = 1.0 / gsum
    el = lt[epg:2 * epg]
    for g in range(1, MOE_GROUPS):
        el = jnp.where(gsel == g, lt[epg * (g + 1):epg * (g + 2)], el)
    v1 = jnp.max(el, 0, keepdims=True)
    i1 = jnp.min(jnp.where(el == v1, row8, big), 0, keepdims=True)
    el2 = jnp.where(row8 == i1, ninf, el)
    v2 = jnp.max(el2, 0, keepdims=True)
    i2 = jnp.min(jnp.where(el2 == v2, row8, big), 0, keepdims=True)
    e = jnp.exp(v2 - v1)
    w1 = gw / (1.0 + e)
    w2 = gw * e / (1.0 + e)
    e1 = gsel * epg + i1
    e2 = gsel * epg + i2

    rowe = lax.broadcasted_iota(jnp.int32, (N_EXPERTS, tm), 0).astype(F32)
    oh1 = rowe == e1
    oh2 = rowe == e2
    oh = jnp.where(oh1 | oh2, 1.0, 0.0)
    r_i = lax.broadcasted_iota(jnp.int32, (tm, tm), 0)
    c_i = lax.broadcasted_iota(jnp.int32, (tm, tm), 1)
    upper = jnp.where(r_i < c_i, 1.0, 0.0).astype(BF16)
    pref = _dot(oh.astype(BF16), upper)
    r1 = jnp.sum(jnp.where(oh1, pref, 0.0), 0, keepdims=True)
    r2 = jnp.sum(jnp.where(oh2, pref, 0.0), 0, keepdims=True)
    cnt_ref[0] = jnp.broadcast_to(jnp.sum(oh, 1, keepdims=True), (N_EXPERTS, LANES))

    route_ref[...] = jnp.where(row8 == 0, e1, jnp.where(row8 == 1, e2, jnp.where(row8 == 2, r1, jnp.where(row8 == 3, r2, 0.0))))
    rowl = lax.broadcasted_iota(jnp.int32, (LANES, tm), 0)
    wts_ref[...] = jnp.where(rowl < 64, w1, w2).T


def _out_call(da, sg, x2d, g1, sc2, sh2, w_out, lg, lb, wr, br, tiles_per_batch):
    n = x2d.shape[0]
    tm = TOK_TILE
    nt = n // tm
    tpb = tiles_per_batch
    row = lambda i: (i, 0)
    full = lambda i: (0, 0)
    per_b = lambda i: (i // tpb, 0, 0)
    return pl.pallas_call(
        _out_kernel,
        grid=(nt,),
        in_specs=[
            pl.BlockSpec((tm, DA_WIDTH), row),
            pl.BlockSpec((tm, SG_WIDTH), row),
            pl.BlockSpec((tm, D_MODEL), row),
            pl.BlockSpec((1, 1, D_MODEL), per_b),
            pl.BlockSpec((1, 1, D_MODEL), per_b),
            pl.BlockSpec((1, 1, D_MODEL), per_b),
            pl.BlockSpec((D_MODEL, D_MODEL), full),
            pl.BlockSpec((1, D_MODEL), full),
            pl.BlockSpec((1, D_MODEL), full),
            pl.BlockSpec((2 * LANES, D_MODEL), full),
            pl.BlockSpec((LANES, tm), full),
        ],
        out_specs=[
            pl.BlockSpec((tm, D_MODEL), row),
            pl.BlockSpec((tm, D_MODEL), row),
            pl.BlockSpec((8, tm), lambda i: (0, i)),
            pl.BlockSpec((tm, LANES), row),
            pl.BlockSpec((1, N_EXPERTS, LANES), lambda i: (i, 0, 0)),
        ],
        out_shape=[
            jax.ShapeDtypeStruct((n, D_MODEL), F32),
            jax.ShapeDtypeStruct((n, D_MODEL), F32),
            jax.ShapeDtypeStruct((8, n), F32),
            jax.ShapeDtypeStruct((n, LANES), F32),
            jax.ShapeDtypeStruct((nt, N_EXPERTS, LANES), F32),
        ],
        compiler_params=_cparams(("arbitrary",)),
        name="out_proj",
    )(da, sg, x2d, g1, sc2, sh2, w_out, lg, lb, wr, br)


ISSUE_UNROLL = 8


def _dispatch_kernel(zs_ref, ze_ref, dest_ref, t_ref, xb_ref, zero_ref, sem):
    i = pl.program_id(0)
    tm = t_ref.shape[0]

    @pl.when(i == 0)
    def _():
        zero_ref[...] = jnp.zeros_like(zero_ref)

        def per_expert(e, carry):
            def start(r, c):
                pltpu.make_async_copy(_row(zero_ref, 0), _row(xb_ref, r), sem).start()
                return c

            def wait(r, c):
                pltpu.make_async_copy(_row(zero_ref, 0), _row(xb_ref, r), sem).wait()
                return c

            lax.fori_loop(zs_ref[e], ze_ref[e], start, 0)
            lax.fori_loop(zs_ref[e], ze_ref[e], wait, 0)
            return carry

        lax.fori_loop(0, N_EXPERTS, per_expert, 0)

        def blk_copy(j):
            return pltpu.make_async_copy(zero_ref, xb_ref.at[pl.ds(pl.multiple_of(j * MOE_BLOCK, MOE_BLOCK), MOE_BLOCK)], sem)

        def blk_start(j, c):
            blk_copy(j).start()
            return c

        def blk_wait(j, c):
            blk_copy(j).wait()
            return c

        first_unused = ze_ref[N_EXPERTS - 1] // MOE_BLOCK
        n_blocks = xb_ref.shape[0] // MOE_BLOCK
        lax.fori_loop(first_unused, n_blocks, blk_start, 0)
        lax.fori_loop(first_unused, n_blocks, blk_wait, 0)

    def start(n, c):
        pltpu.make_async_copy(_row(t_ref, n), _row(xb_ref, dest_ref[0, 0, n]), sem).start()
        pltpu.make_async_copy(_row(t_ref, n), _row(xb_ref, dest_ref[0, 0, tm + n]), sem).start(priority=1)
        return c

    lax.fori_loop(0, tm, start, 0, unroll=ISSUE_UNROLL)
    for _ in range(2):
        pltpu.make_async_copy(t_ref, xb_ref.at[pl.ds(0, tm)], sem).wait()


def _dispatch_call(zs, ze, dest3, t, p_rows):
    n = t.shape[0]
    tm = TOK_TILE
    grid_spec = pltpu.PrefetchScalarGridSpec(
        num_scalar_prefetch=2,
        grid=(n // tm,),
        in_specs=[
            pl.BlockSpec((1, 1, 2 * tm), lambda i, zs, ze: (i, 0, 0), memory_space=pltpu.SMEM),
            pl.BlockSpec((tm, D_MODEL), lambda i, zs, ze: (i, 0)),
        ],
        out_specs=pl.BlockSpec(memory_space=pl.ANY),
        scratch_shapes=[pltpu.VMEM((MOE_BLOCK, D_MODEL), F32), pltpu.SemaphoreType.DMA(())],
    )
    return pl.pallas_call(
        _dispatch_kernel,
        grid_spec=grid_spec,
        out_shape=jax.ShapeDtypeStruct((p_rows, D_MODEL), F32),
        compiler_params=_cparams(("arbitrary",)),
        name="dispatch",
    )(zs, ze, dest3, t)


ITEM_SUB = 6
ITEM_ROWS = ITEM_SUB * MOE_BLOCK


def _expert_kernel(ie_ref, ib_ref, ins_ref, iord_ref, inxt_ref, ni_ref, xb_ref, wg_ref, wu_ref, wd_ref, yb_ref,
                   xbuf, ybuf, wf32, wgb, wub, wdb, insem, outsem, wsem):
    i = pl.program_id(0)
    n_steps = pl.num_programs(0)
    ni = ni_ref[0]
    slot = i % 2

    def w_copies(e, s):
        return [pltpu.make_async_copy(w.at[e], wf32.at[s, k], wsem.at[s]) for k, w in enumerate((wg_ref, wu_ref, wd_ref))]

    def rows_of(item, j):
        return pl.ds(pl.multiple_of((ib_ref[item] + j) * MOE_BLOCK, MOE_BLOCK), MOE_BLOCK)

    def in_copy(item, s, j):
        return pltpu.make_async_copy(xb_ref.at[rows_of(item, j)], xbuf.at[s, pl.ds(j * MOE_BLOCK, MOE_BLOCK)], insem.at[s])

    def out_copy(item, s, j):
        return pltpu.make_async_copy(ybuf.at[s, pl.ds(j * MOE_BLOCK, MOE_BLOCK)], yb_ref.at[rows_of(item, j)], outsem.at[s])

    def for_blocks(item, fn):
        for j in range(ITEM_SUB):
            @pl.when(j < ins_ref[item])
            def _():
                fn(j)

    @pl.when(i == 0)
    def _():
        for c in w_copies(ie_ref[0], 0):
            c.start()
        for_blocks(0, lambda j: in_copy(0, 0, j).start())

    @pl.when(i < ni)
    def _():
        @pl.when(i >= 2)
        def _():
            for_blocks(i - 2, lambda j: out_copy(i - 2, slot, j).wait())

        @pl.when(i + 1 < ni)
        def _():
            for_blocks(i + 1, lambda j: in_copy(i + 1, 1 - slot, j).start())

        changed = (i == 0) | (ie_ref[i] != ie_ref[jnp.maximum(i - 1, 0)])

        @pl.when(changed)
        def _():
            ws = iord_ref[i] % 2
            for c in w_copies(ie_ref[i], ws):
                c.wait()

            @pl.when(inxt_ref[i] >= 0)
            def _():
                for c in w_copies(inxt_ref[i], 1 - ws):
                    c.start()

            wgb[...] = wf32[ws, 0].astype(BF16)
            wub[...] = wf32[ws, 1].astype(BF16)
            wdb[...] = wf32[ws, 2].astype(BF16)

        for_blocks(i, lambda j: in_copy(i, slot, j).wait())

        for ns in range(1, ITEM_SUB + 1):
            @pl.when(ins_ref[i] == ns)
            def _():
                rows = pl.ds(0, ns * MOE_BLOCK)
                x = xbuf[slot, rows].astype(BF16)
                g = _dot(x, wgb[...])
                u = _dot(x, wub[...])
                hid = (g * (1.0 / (1.0 + jnp.exp(-g))) * u).astype(BF16)
                ybuf[slot, rows] = _dot(hid, wdb[...])

        for_blocks(i, lambda j: out_copy(i, slot, j).start())

    @pl.when(i == n_steps - 1)
    def _():
        for back in (2, 1):
            k = ni - back

            @pl.when(k >= 0)
            def _():
                for_blocks(k, lambda j: out_copy(k, k % 2, j).wait())

        ybuf[0, pl.ds(0, MOE_BLOCK)] = jnp.zeros((MOE_BLOCK, D_MODEL), F32)
        last = jnp.maximum(ni - 1, 0)
        first_unused = ib_ref[last] + ins_ref[last]
        n_blocks = yb_ref.shape[0] // MOE_BLOCK

        def zero_copy(b):
            dst = yb_ref.at[pl.ds(pl.multiple_of(b * MOE_BLOCK, MOE_BLOCK), MOE_BLOCK)]
            return pltpu.make_async_copy(ybuf.at[0, pl.ds(0, MOE_BLOCK)], dst, outsem.at[0])

        def z_start(b, c):
            zero_copy(b).start()
            return c

        def z_wait(b, c):
            zero_copy(b).wait()
            return c

        lax.fori_loop(first_unused, n_blocks, z_start, 0)
        lax.fori_loop(first_unused, n_blocks, z_wait, 0)


def _expert_call(item_e, item_b, item_ns, item_ord, item_nxt, n_items, xb, wg, wu, wd):
    p_rows = xb.shape[0]
    max_items = item_e.shape[0]
    grid_spec = pltpu.PrefetchScalarGridSpec(
        num_scalar_prefetch=6,
        grid=(max_items,),
        in_specs=[pl.BlockSpec(memory_space=pl.ANY)] * 4,
        out_specs=pl.BlockSpec(memory_space=pl.ANY),
        scratch_shapes=[
            pltpu.VMEM((2, ITEM_ROWS, D_MODEL), F32),
            pltpu.VMEM((2, ITEM_ROWS, D_MODEL), F32),
            pltpu.VMEM((2, 3, D_MODEL, D_MODEL), F32),
            pltpu.VMEM((D_MODEL, D_MODEL), BF16),
            pltpu.VMEM((D_MODEL, D_MODEL), BF16),
            pltpu.VMEM((D_MODEL, D_MODEL), BF16),
            pltpu.SemaphoreType.DMA((2,)),
            pltpu.SemaphoreType.DMA((2,)),
            pltpu.SemaphoreType.DMA((2,)),
        ],
    )
    return pl.pallas_call(
        _expert_kernel,
        grid_spec=grid_spec,
        out_shape=jax.ShapeDtypeStruct((p_rows, D_MODEL), F32),
        compiler_params=_cparams(("arbitrary",)),
        name="experts",
    )(item_e, item_b, item_ns, item_ord, item_nxt, n_items, xb, wg, wu, wd)


def _combine_kernel(dcur_ref, dnxt_ref, yb_ref, x1_ref, wts_ref, g2_ref, lg_ref, lb_ref, o_ref, buf, sem):
    i = pl.program_id(0)
    tm = x1_ref.shape[0]
    slot = i % 2

    def issue(dref, s):
        def start(n, c):
            pltpu.make_async_copy(_row(yb_ref, dref[0, 0, n]), _row(buf.at[s, 0], n), sem.at[s]).start()
            pltpu.make_async_copy(_row(yb_ref, dref[0, 0, tm + n]), _row(buf.at[s, 1], n), sem.at[s]).start(priority=1)
            return c

        lax.fori_loop(0, tm, start, 0, unroll=ISSUE_UNROLL)

    @pl.when(i == 0)
    def _():
        issue(dcur_ref, 0)

    @pl.when(i + 1 < pl.num_programs(0))
    def _():
        issue(dnxt_ref, 1 - slot)

    for k in range(2):
        pltpu.make_async_copy(yb_ref.at[pl.ds(0, tm)], buf.at[slot, k], sem.at[slot]).wait()
    w = wts_ref[...]
    f = w[:, 0:1] * buf[slot, 0] + w[:, 64:65] * buf[slot, 1]
    o_ref[...] = _layer_norm(DEEPNORM_ALPHA * x1_ref[...] + g2_ref[0] * f, lg_ref[...], lb_ref[...])


def _combine_call(dest3, yb, x1, wts, g2, lg, lb, tiles_per_batch):
    n = x1.shape[0]
    tm = TOK_TILE
    tpb = tiles_per_batch
    row = lambda i: (i, 0)
    full = lambda i: (0, 0)
    nt = n // tm
    return pl.pallas_call(
        _combine_kernel,
        grid=(nt,),
        in_specs=[
            pl.BlockSpec((1, 1, 2 * tm), lambda i: (i, 0, 0), memory_space=pltpu.SMEM),
            pl.BlockSpec((1, 1, 2 * tm), lambda i: (jnp.minimum(i + 1, nt - 1), 0, 0), memory_space=pltpu.SMEM),
            pl.BlockSpec(memory_space=pl.ANY),
            pl.BlockSpec((tm, D_MODEL), row),
            pl.BlockSpec((tm, LANES), row),
            pl.BlockSpec((1, 1, D_MODEL), lambda i: (i // tpb, 0, 0)),
            pl.BlockSpec((1, D_MODEL), full),
            pl.BlockSpec((1, D_MODEL), full),
        ],
        out_specs=pl.BlockSpec((tm, D_MODEL), row),
        out_shape=jax.ShapeDtypeStruct((n, D_MODEL), F32),
        scratch_shapes=[pltpu.VMEM((2, 2, tm, D_MODEL), F32), pltpu.SemaphoreType.DMA((2,))],
        compiler_params=_cparams(("arbitrary",)),
        name="combine",
    )(dest3, dest3, yb, x1, wts, g2, lg, lb)


def _head_interleave(w, lo):
    blk = w[:, lo:lo + 2 * QK_COLS].reshape(D_MODEL, 2, DA_HEADS, DA_HEAD_DIM)
    return blk.transpose(0, 2, 1, 3).reshape(D_MODEL, 2 * QK_COLS)


def _cumsum_small(x):
    idx = jnp.arange(x.shape[0], dtype=jnp.int32)
    return jnp.sum(jnp.where(idx[None, :] <= idx[:, None], x[None, :], 0), -1)


def _lookup(table, idx):
    return jnp.sum(jnp.where(idx[:, None] == jnp.arange(table.shape[0], dtype=jnp.int32)[None, :], table[None, :], 0), -1)


def _rope_tables(seq):
    rows_n = seq // GRID_W
    rows = jnp.repeat(jnp.arange(rows_n, dtype=F32), GRID_W)
    cols = jnp.tile(jnp.arange(GRID_W, dtype=F32), rows_n)
    half = DA_HEAD_DIM // 4
    inv = ROPE_THETA ** (-jnp.arange(half, dtype=F32) / half)
    ang_r = rows[:, None] * inv[None, :]
    ang_c = cols[:, None] * inv[None, :]
    cos64 = jnp.concatenate([jnp.cos(ang_r), jnp.cos(ang_r), jnp.cos(ang_c), jnp.cos(ang_c)], -1)
    sin64 = jnp.concatenate([-jnp.sin(ang_r), jnp.sin(ang_r), -jnp.sin(ang_c), jnp.sin(ang_c)], -1)
    return jnp.tile(cos64, (1, 2)), jnp.tile(sin64, (1, 2))


def kernel(x, c, ctx, c_ctx, w_mod, b_mod, w_in, lam_q1, lam_k1, lam_q2, lam_k2, subln_g, sg_ln_g, sg_ln_b, sg_w, sg_b, w_out, ln1_g, ln1_b, router_group_w, router_group_b, router_expert_w, router_expert_b, exp_w_gate, exp_w_up, exp_w_down, ln2_g, ln2_b):
    b, l, d = x.shape
    n = b * l
    tm = TOK_TILE
    nt = n // tm

    cc = jnp.zeros((b + 8, d), F32).at[:b].set(c).at[b].set(c_ctx)
    mod = _mod_call(cc, w_mod[0], b_mod[0][None, :])
    sh1, sc1, g1, sh2, sc2, g2 = [mod[:b, j * d:(j + 1) * d].reshape(b, 1, d) for j in range(6)]
    csh1 = mod[b:b + 1, 0:d]
    csc1 = mod[b:b + 1, d:2 * d]

    wi = w_in[0]
    w_all = jnp.concatenate([_head_interleave(wi, 0), _head_interleave(wi, KV_LO), wi[:, 2 * KV_LO:]], -1).astype(BF16)
    kc, vc = _ctx_call(ctx, csc1, csh1, w_all[:, KV_LO:KV_HI])

    cos_t, sin_t = _rope_tables(l)
    sbias = jnp.repeat(sg_b[0].T, LANES, axis=1)
    q, k, v, sg = _in_call(x, sc1, sh1, w_all, cos_t, sin_t, sg_ln_g[0][None, :], sg_ln_b[0][None, :],
                           sg_w[0].astype(BF16), sbias)

    lamv = jnp.stack([lam_q1[0], lam_k1[0], lam_q2[0], lam_k2[0]]).astype(F32)
    da = _attn_call(lamv, q, k, kc, v, vc, subln_g[0][None, :])

    e_lo = MOE_EXPERTS_PER_GROUP
    wr = jnp.zeros((LANES, d), F32).at[:MOE_GROUPS].set(router_group_w[0].T).at[e_lo:e_lo + N_EXPERTS].set(router_expert_w[0].T)
    br = jnp.zeros((LANES,), F32).at[:MOE_GROUPS].set(router_group_b[0]).at[e_lo:e_lo + N_EXPERTS].set(router_expert_b[0])
    br = jnp.broadcast_to(br[:, None], (LANES, tm))
    wr_hi = wr.astype(BF16)
    wr_split = jnp.concatenate([wr_hi, (wr - wr_hi.astype(F32)).astype(BF16)], 0)
    x1, t, route, wts, tcnt = _out_call(da.reshape(n, DA_WIDTH), sg.reshape(n, SG_WIDTH), x.reshape(n, d), g1, sc2, sh2,
                                        w_out[0].astype(BF16), ln1_g[0][None, :], ln1_b[0][None, :], wr_split, br, l // tm)

    cnt_te = tcnt[:, :, 0].astype(jnp.int32)
    counts = jnp.sum(cnt_te, 0)
    padded = (counts + MOE_BLOCK - 1) // MOE_BLOCK * MOE_BLOCK
    pad_end = _cumsum_small(padded)
    pad_start = pad_end - padded
    tix = jnp.arange(nt, dtype=jnp.int32)
    tile_prefix = jnp.sum(jnp.where(tix[None, :, None] < tix[:, None, None], cnt_te[None], 0), 1)
    base = pad_start[None, :] + tile_prefix
    ridx = route[:4].astype(jnp.int32).reshape(4, nt, tm)
    ex = jnp.arange(N_EXPERTS, dtype=jnp.int32)

    def slot_dest(eid, rank):
        return jnp.sum(jnp.where(eid[None] == ex[:, None, None], base.T[:, :, None], 0), 0) + rank

    dest3 = jnp.concatenate([slot_dest(ridx[0], ridx[2]), slot_dest(ridx[1], ridx[3])], -1).reshape(nt, 1, 2 * tm)
    n_blocks = (n * 2) // MOE_BLOCK + N_EXPERTS
    p_rows = n_blocks * MOE_BLOCK
    nb_e = padded // MOE_BLOCK
    items_e = (nb_e + ITEM_SUB - 1) // ITEM_SUB
    item_end = _cumsum_small(items_e)
    max_items = (n_blocks + (ITEM_SUB - 1) * N_EXPERTS) // ITEM_SUB
    it = jnp.arange(max_items, dtype=jnp.int32)
    item_e = jnp.minimum(jnp.sum((it[:, None] >= item_end[None, :]).astype(jnp.int32), -1), N_EXPERTS - 1)
    item_j = it - _lookup(item_end - items_e, item_e)
    item_b = _lookup(pad_start // MOE_BLOCK, item_e) + ITEM_SUB * item_j
    item_ns = jnp.clip(_lookup(nb_e, item_e) - ITEM_SUB * item_j, 0, ITEM_SUB)
    nonempty = (items_e > 0).astype(jnp.int32)
    ord_e = _cumsum_small(nonempty) - 1
    later = (ex[None, :] > ex[:, None]) & (nonempty[None, :] > 0)
    nxt_e = jnp.min(jnp.where(later, ex[None, :], N_EXPERTS), -1)
    nxt_e = jnp.where(nxt_e == N_EXPERTS, -1, nxt_e)
    item_ord = _lookup(ord_e, item_e)
    item_nxt = _lookup(nxt_e, item_e)

    xb = _dispatch_call((pad_start + counts).astype(jnp.int32), pad_end.astype(jnp.int32), dest3, t, p_rows)
    yb = _expert_call(item_e, item_b.astype(jnp.int32), item_ns.astype(jnp.int32), item_ord.astype(jnp.int32),
                      item_nxt.astype(jnp.int32), item_end[-1:].astype(jnp.int32),
                      xb, exp_w_gate[0], exp_w_up[0], exp_w_down[0])
    out = _combine_call(dest3, yb, x1, wts, g2, ln2_g[0][None, :], ln2_b[0][None, :], l // tm)
    return out.reshape(b, l, d)
```

```python
import math

import jax
import jax.numpy as jnp
from jax import lax
from jax.experimental import pallas as pl
from jax.experimental.pallas import tpu as pltpu

F32 = jnp.float32
BF16 = jnp.bfloat16

D_MODEL = 1024
GRID_W = 64
DA_HEAD_DIM = 64
DA_V_DIM = 128
DA_WIDTH = 512
DA_HEADS = 4
QK_COLS = 256
SG_CHUNK = 128
SG_WIDTH = 512
SG_GROUPS = 4
KV_LO = 512
KV_HI = 1536
IN_COLS = 2560
ROPE_THETA = 10000.0
MOE_GROUPS = 4
MOE_EXPERTS_PER_GROUP = 8
N_EXPERTS = 32
MOE_BLOCK = 128
EPS = 1e-5
DEPTH = 1
DEEPNORM_ALPHA = (2.0 * DEPTH) ** 0.25
LAM_INIT = 0.8 - 0.6 * math.exp(-0.3 * 0)
Q_SCALE = DA_HEAD_DIM ** -0.5
LOG2E = math.log2(math.e)
SQRT_HALF = math.sqrt(0.5)

LANES = 128
TOK_TILE = 512
IN_SUB = 512
ATTN_TQ = 2048
ATTN_SUB = 256
VMEM_LIMIT = 56 * 1024 * 1024


def _cparams(sem):
    return pltpu.CompilerParams(dimension_semantics=sem, vmem_limit_bytes=VMEM_LIMIT)


def _dot(a, b):
    return jnp.dot(a, b, preferred_element_type=F32)


def _dot_hi(a, b):
    return jnp.dot(a, b, preferred_element_type=F32, precision=lax.Precision.HIGHEST)


def _layer_norm(y, g, b):
    mu = jnp.mean(y, -1, keepdims=True)
    yc = y - mu
    var = jnp.mean(yc * yc, -1, keepdims=True)
    return yc * lax.rsqrt(var + EPS) * g + b


def _row(ref, n):
    return ref.at[pl.ds(n, 1)]


def _mod_kernel(c_ref, w_ref, b_ref, o_ref):
    c = c_ref[...]
    s = c * (1.0 / (1.0 + jnp.exp(-c)))
    o_ref[...] = _dot_hi(s, w_ref[...]) + b_ref[...]


def _mod_call(cc, w_mod, b_mod):
    rows = cc.shape[0]
    n_out = w_mod.shape[1]
    bn = 1024
    return pl.pallas_call(
        _mod_kernel,
        grid=(n_out // bn,),
        in_specs=[
            pl.BlockSpec((rows, D_MODEL), lambda j: (0, 0)),
            pl.BlockSpec((D_MODEL, bn), lambda j: (0, j)),
            pl.BlockSpec((1, bn), lambda j: (0, j)),
        ],
        out_specs=pl.BlockSpec((rows, bn), lambda j: (0, j)),
        out_shape=jax.ShapeDtypeStruct((rows, n_out), F32),
        compiler_params=_cparams(("arbitrary",)),
        name="mod",
    )(cc, w_mod, b_mod)


CTX_BATCH = 4


def _ctx_kernel(c_ref, sc_ref, sh_ref, w_ref, kc_ref, vc_ref):
    nb, cl, _ = c_ref.shape
    c = c_ref[...].reshape(nb * cl, D_MODEL)
    h = (c * (1.0 + sc_ref[...]) + sh_ref[...]).astype(BF16)
    p = _dot(h, w_ref[...])
    kc_ref[...] = p[:, :2 * QK_COLS].astype(BF16).reshape(nb, cl, 2 * QK_COLS)
    vc_ref[...] = p[:, 2 * QK_COLS:].astype(BF16).reshape(nb, cl, DA_WIDTH)


def _ctx_call(ctx, csc1, csh1, w_kv):
    b, cl, _ = ctx.shape
    nb = CTX_BATCH
    return pl.pallas_call(
        _ctx_kernel,
        grid=(b // nb,),
        in_specs=[
            pl.BlockSpec((nb, cl, D_MODEL), lambda i: (i, 0, 0)),
            pl.BlockSpec((1, D_MODEL), lambda i: (0, 0)),
            pl.BlockSpec((1, D_MODEL), lambda i: (0, 0)),
            pl.BlockSpec((D_MODEL, 2 * QK_COLS + DA_WIDTH), lambda i: (0, 0)),
        ],
        out_specs=[
            pl.BlockSpec((nb, cl, 2 * QK_COLS), lambda i: (i, 0, 0)),
            pl.BlockSpec((nb, cl, DA_WIDTH), lambda i: (i, 0, 0)),
        ],
        out_shape=[
            jax.ShapeDtypeStruct((b, cl, 2 * QK_COLS), BF16),
            jax.ShapeDtypeStruct((b, cl, DA_WIDTH), BF16),
        ],
        compiler_params=_cparams(("arbitrary",)),
        name="ctx_kv",
    )(ctx, csc1, csh1, w_kv)


def _in_kernel(x_ref, sc_ref, sh_ref, w_ref, cos_ref, sin_ref, lng_ref, lnb_ref,
               sw_ref, sb_ref, q_ref, k_ref, v_ref, sg_ref):
    tm = x_ref.shape[1]
    sub = IN_SUB
    lane = lax.broadcasted_iota(jnp.int32, (sub, LANES), 1)
    first = (lane % 32) < 16

    for r0 in range(0, tm, sub):
        rr = pl.ds(r0, sub)
        h = (x_ref[0, rr, :] * (1.0 + sc_ref[0]) + sh_ref[0]).astype(BF16)
        p = _dot(h, w_ref[...])

        cos = cos_ref[rr, :]
        sin = sin_ref[rr, :]

        def rope(t):
            partner = jnp.where(first, pltpu.roll(t, LANES - 16, 1), pltpu.roll(t, 16, 1))
            return t * cos + partner * sin

        for c in range(4):
            cs = slice(c * LANES, (c + 1) * LANES)
            q_ref[0, rr, cs] = rope(p[:, cs] * (Q_SCALE * LOG2E)).astype(BF16)
            k_ref[0, rr, cs] = rope(p[:, KV_LO + c * LANES:KV_LO + (c + 1) * LANES]).astype(BF16)
        v_ref[0, rr, :] = p[:, 2 * KV_LO:KV_HI].astype(BF16)

        z = p[:, KV_HI:]
        gz = 0.5 * z * (1.0 + lax.erf(z * SQRT_HALF))
        u = gz[:, :SG_WIDTH]
        vn = _layer_norm(gz[:, SG_WIDTH:], lng_ref[...], lnb_ref[...]).astype(BF16)
        for c in range(sub // SG_CHUNK):
            rs = slice(c * SG_CHUNK, (c + 1) * SG_CHUNK)
            for g in range(SG_GROUPS):
                cs = slice(g * LANES, (g + 1) * LANES)
                s = _dot(sw_ref[g], vn[rs, cs]) + sb_ref[:, cs]
                sg_ref[0, pl.ds(r0 + c * SG_CHUNK, SG_CHUNK), cs] = (u[rs, cs] * s).astype(BF16)


def _in_call(x, sc1, sh1, w_in, cos_t, sin_t, lng, lnb, sw, sbias):
    b, l, _ = x.shape
    tm = 2 * TOK_TILE
    full = lambda bi, i: (0, 0)
    return pl.pallas_call(
        _in_kernel,
        grid=(b, l // tm),
        in_specs=[
            pl.BlockSpec((1, tm, D_MODEL), lambda bi, i: (bi, i, 0)),
            pl.BlockSpec((1, 1, D_MODEL), lambda bi, i: (bi, 0, 0)),
            pl.BlockSpec((1, 1, D_MODEL), lambda bi, i: (bi, 0, 0)),
            pl.BlockSpec((D_MODEL, IN_COLS), full),
            pl.BlockSpec((tm, LANES), lambda bi, i: (i, 0)),
            pl.BlockSpec((tm, LANES), lambda bi, i: (i, 0)),
            pl.BlockSpec((1, SG_WIDTH), full),
            pl.BlockSpec((1, SG_WIDTH), full),
            pl.BlockSpec((SG_GROUPS, SG_CHUNK, SG_CHUNK), lambda bi, i: (0, 0, 0)),
            pl.BlockSpec((SG_CHUNK, SG_WIDTH), full),
        ],
        out_specs=[pl.BlockSpec((1, tm, 512), lambda bi, i: (bi, i, 0))] * 4,
        out_shape=[jax.ShapeDtypeStruct((b, l, 512), BF16)] * 4,
        compiler_params=_cparams(("arbitrary", "arbitrary")),
        name="in_proj",
    )(x, sc1, sh1, w_in, cos_t, sin_t, lng, lnb, sw, sbias)


def _attn_kernel(lam_ref, q_ref, k_ref, kc_ref, v_ref, vc_ref, g_ref, o_ref, kall, vall):
    lv = lam_ref[...]
    lam = (jnp.exp(jnp.sum(lv[0:1] * lv[1:2], -1, keepdims=True))
           - jnp.exp(jnp.sum(lv[2:3] * lv[3:4], -1, keepdims=True)) + LAM_INIT)
    l = k_ref.shape[1]

    @pl.when(pl.program_id(2) == 0)
    def _():
        kall[:l, :] = k_ref[0]
        kall[l:, :] = kc_ref[0]
        vall[:l, :DA_V_DIM] = v_ref[0]
        vall[l:, :DA_V_DIM] = vc_ref[0]
        vall[:, DA_V_DIM:] = jnp.ones((vall.shape[0], DA_V_DIM), BF16)

    nt = (((1,), (1,)), ((), ()))

    def branch(qm):
        s = lax.dot_general(qm, kall[...], nt, preferred_element_type=F32)
        m = jnp.max(s, -1, keepdims=True)
        e = jnp.exp2((s - m).astype(BF16))
        oe = _dot(e, vall[...])
        return oe[:, :DA_V_DIM] / oe[:, DA_V_DIM:DA_V_DIM + 1]

    for r in range(q_ref.shape[1] // ATTN_SUB):
        rs = pl.ds(r * ATTN_SUB, ATTN_SUB)
        q = q_ref[0, rs, :]
        lane = lax.broadcasted_iota(jnp.int32, q.shape, 1)
        zero = jnp.zeros_like(q)
        o = branch(jnp.where(lane < DA_HEAD_DIM, q, zero)) - lam * branch(jnp.where(lane >= DA_HEAD_DIM, q, zero))
        of = o * lax.rsqrt(jnp.mean(o * o, -1, keepdims=True) + EPS) * g_ref[...]
        o_ref[0, rs, :] = (of * (1.0 - LAM_INIT)).astype(BF16)


def _attn_call(lamv, q, k, kc, v, vc, subln_g):
    b, l, _ = q.shape
    cl = kc.shape[1]
    tq = ATTN_TQ
    return pl.pallas_call(
        _attn_kernel,
        grid=(b, DA_HEADS, l // tq),
        in_specs=[
            pl.BlockSpec((4, DA_HEAD_DIM), lambda bi, h, i: (0, 0)),
            pl.BlockSpec((1, tq, LANES), lambda bi, h, i: (bi, i, h)),
            pl.BlockSpec((1, l, LANES), lambda bi, h, i: (bi, 0, h)),
            pl.BlockSpec((1, cl, LANES), lambda bi, h, i: (bi, 0, h)),
            pl.BlockSpec((1, l, LANES), lambda bi, h, i: (bi, 0, h)),
            pl.BlockSpec((1, cl, LANES), lambda bi, h, i: (bi, 0, h)),
            pl.BlockSpec((1, DA_V_DIM), lambda bi, h, i: (0, 0)),
        ],
        out_specs=pl.BlockSpec((1, tq, LANES), lambda bi, h, i: (bi, i, h)),
        out_shape=jax.ShapeDtypeStruct((b, l, DA_WIDTH), BF16),
        scratch_shapes=[pltpu.VMEM((l + cl, LANES), BF16), pltpu.VMEM((l + cl, 2 * DA_V_DIM), BF16)],
        compiler_params=_cparams(("arbitrary", "arbitrary", "arbitrary")),
        name="attn",
    )(lamv, q, k, kc, v, vc, subln_g)


def _out_kernel(da_ref, sg_ref, x_ref, g1_ref, sc_ref, sh_ref, w_ref, lg_ref, lb_ref,
                wr_ref, br_ref, x1_ref, t_ref, route_ref, wts_ref, cnt_ref):
    tm = x_ref.shape[0]
    y = _dot(da_ref[...], w_ref[:DA_WIDTH, :]) + _dot(sg_ref[...], w_ref[DA_WIDTH:, :])
    x1 = _layer_norm(DEEPNORM_ALPHA * x_ref[...] + g1_ref[0] * y, lg_ref[...], lb_ref[...])
    x1_ref[...] = x1
    t = x1 * (1.0 + sc_ref[0]) + sh_ref[0]
    t_ref[...] = t

    t_hi = t.astype(BF16)
    t_lo = (t - t_hi.astype(F32)).astype(BF16)
    nt_dims = (((1,), (1,)), ((), ()))
    hw = lax.dot_general(wr_ref[...], t_hi, nt_dims, preferred_element_type=F32)
    lt = (hw[:LANES] + hw[LANES:] + lax.dot_general(wr_ref[:LANES, :], t_lo, nt_dims, preferred_element_type=F32)
          + br_ref[...])
    epg = MOE_EXPERTS_PER_GROUP
    row8 = lax.broadcasted_iota(jnp.int32, (epg, tm), 0).astype(F32)
    ninf = jnp.float32(-jnp.inf)
    big = jnp.float32(epg)
    gmask = row8 < MOE_GROUPS
    gl = jnp.where(gmask, lt[0:epg], ninf)
    gmax = jnp.max(gl, 0, keepdims=True)
    gsel = jnp.min(jnp.where(gl == gmax, row8, big), 0, keepdims=True)
    gsum = jnp.sum(jnp.where(gmask, jnp.exp(gl - gmax), 0.0), 0, keepdims=True)
    gw = 1.0 / gsum
    el = lt[epg:2 * epg]
    for g in range(1, MOE_GROUPS):
        el = jnp.where(gsel == g, lt[epg * (g + 1):epg * (g + 2)], el)
    v1 = jnp.max(el, 0, keepdims=True)
    i1 = jnp.min(jnp.where(el == v1, row8, big), 0, keepdims=True)
    el2 = jnp.where(row8 == i1, ninf, el)
    v2 = jnp.max(el2, 0, keepdims=True)
    i2 = jnp.min(jnp.where(el2 == v2, row8, big), 0, keepdims=True)
    e = jnp.exp(v2 - v1)
    w1 = gw / (1.0 + e)
    w2 = gw * e / (1.0 + e)
    e1 = gsel * epg + i1
    e2 = gsel * epg + i2

    rowe = lax.broadcasted_iota(jnp.int32, (N_EXPERTS, tm), 0).astype(F32)
    oh1 = rowe == e1
    oh2 = rowe == e2
    oh = jnp.where(oh1 | oh2, 1.0, 0.0)
    r_i = lax.broadcasted_iota(jnp.int32, (tm, tm), 0)
    c_i = lax.broadcasted_iota(jnp.int32, (tm, tm), 1)
    upper = jnp.where(r_i < c_i, 1.0, 0.0).astype(BF16)
    pref = _dot(oh.astype(BF16), upper)
    r1 = jnp.sum(jnp.where(oh1, pref, 0.0), 0, keepdims=True)
    r2 = jnp.sum(jnp.where(oh2, pref, 0.0), 0, keepdims=True)
    cnt_ref[0] = jnp.broadcast_to(jnp.sum(oh, 1, keepdims=True), (N_EXPERTS, LANES))

    route_ref[...] = jnp.where(row8 == 0, e1, jnp.where(row8 == 1, e2, jnp.where(row8 == 2, r1, jnp.where(row8 == 3, r2, 0.0))))
    rowl = lax.broadcasted_iota(jnp.int32, (LANES, tm), 0)
    wts_ref[...] = jnp.where(rowl < 64, w1, w2).T


def _out_call(da, sg, x2d, g1, sc2, sh2, w_out, lg, lb, wr, br, tiles_per_batch):
    n = x2d.shape[0]
    tm = TOK_TILE
    nt = n // tm
    tpb = tiles_per_batch
    row = lambda i: (i, 0)
    full = lambda i: (0, 0)
    per_b = lambda i: (i // tpb, 0, 0)
    return pl.pallas_call(
        _out_kernel,
        grid=(nt,),
        in_specs=[
            pl.BlockSpec((tm, DA_WIDTH), row),
            pl.BlockSpec((tm, SG_WIDTH), row),
            pl.BlockSpec((tm, D_MODEL), row),
            pl.BlockSpec((1, 1, D_MODEL), per_b),
            pl.BlockSpec((1, 1, D_MODEL), per_b),
            pl.BlockSpec((1, 1, D_MODEL), per_b),
            pl.BlockSpec((D_MODEL, D_MODEL), full),
            pl.BlockSpec((1, D_MODEL), full),
            pl.BlockSpec((1, D_MODEL), full),
            pl.BlockSpec((2 * LANES, D_MODEL), full),
            pl.BlockSpec((LANES, tm), full),
        ],
        out_specs=[
            pl.BlockSpec((tm, D_MODEL), row),
            pl.BlockSpec((tm, D_MODEL), row),
            pl.BlockSpec((8, tm), lambda i: (0, i)),
            pl.BlockSpec((tm, LANES), row),
            pl.BlockSpec((1, N_EXPERTS, LANES), lambda i: (i, 0, 0)),
        ],
        out_shape=[
            jax.ShapeDtypeStruct((n, D_MODEL), F32),
            jax.ShapeDtypeStruct((n, D_MODEL), F32),
            jax.ShapeDtypeStruct((8, n), F32),
            jax.ShapeDtypeStruct((n, LANES), F32),
            jax.ShapeDtypeStruct((nt, N_EXPERTS, LANES), F32),
        ],
        compiler_params=_cparams(("arbitrary",)),
        name="out_proj",
    )(da, sg, x2d, g1, sc2, sh2, w_out, lg, lb, wr, br)


ISSUE_UNROLL = 8


def _dispatch_kernel(zs_ref, ze_ref, dest_ref, t_ref, xb_ref, zero_ref, sem):
    i = pl.program_id(0)
    tm = t_ref.shape[0]

    @pl.when(i == 0)
    def _():
        zero_ref[...] = jnp.zeros_like(zero_ref)

        def per_expert(e, carry):
            def start(r, c):
                pltpu.make_async_copy(_row(zero_ref, 0), _row(xb_ref, r), sem).start()
                return c

            def wait(r, c):
                pltpu.make_async_copy(_row(zero_ref, 0), _row(xb_ref, r), sem).wait()
                return c

            lax.fori_loop(zs_ref[e], ze_ref[e], start, 0)
            lax.fori_loop(zs_ref[e], ze_ref[e], wait, 0)
            return carry

        lax.fori_loop(0, N_EXPERTS, per_expert, 0)

        def blk_copy(j):
            return pltpu.make_async_copy(zero_ref, xb_ref.at[pl.ds(pl.multiple_of(j * MOE_BLOCK, MOE_BLOCK), MOE_BLOCK)], sem)

        def blk_start(j, c):
            blk_copy(j).start()
            return c

        def blk_wait(j, c):
            blk_copy(j).wait()
            return c

        first_unused = ze_ref[N_EXPERTS - 1] // MOE_BLOCK
        n_blocks = xb_ref.shape[0] // MOE_BLOCK
        lax.fori_loop(first_unused, n_blocks, blk_start, 0)
        lax.fori_loop(first_unused, n_blocks, blk_wait, 0)

    def start(n, c):
        pltpu.make_async_copy(_row(t_ref, n), _row(xb_ref, dest_ref[0, 0, n]), sem).start()
        pltpu.make_async_copy(_row(t_ref, n), _row(xb_ref, dest_ref[0, 0, tm + n]), sem).start()
        return c

    lax.fori_loop(0, tm, start, 0, unroll=ISSUE_UNROLL)
    for _ in range(2):
        pltpu.make_async_copy(t_ref, xb_ref.at[pl.ds(0, tm)], sem).wait()


def _dispatch_call(zs, ze, dest3, t, p_rows):
    n = t.shape[0]
    tm = TOK_TILE
    grid_spec = pltpu.PrefetchScalarGridSpec(
        num_scalar_prefetch=2,
        grid=(n // tm,),
        in_specs=[
            pl.BlockSpec((1, 1, 2 * tm), lambda i, zs, ze: (i, 0, 0), memory_space=pltpu.SMEM),
            pl.BlockSpec((tm, D_MODEL), lambda i, zs, ze: (i, 0)),
        ],
        out_specs=pl.BlockSpec(memory_space=pl.ANY),
        scratch_shapes=[pltpu.VMEM((MOE_BLOCK, D_MODEL), F32), pltpu.SemaphoreType.DMA(())],
    )
    return pl.pallas_call(
        _dispatch_kernel,
        grid_spec=grid_spec,
        out_shape=jax.ShapeDtypeStruct((p_rows, D_MODEL), F32),
        compiler_params=_cparams(("arbitrary",)),
        name="dispatch",
    )(zs, ze, dest3, t)


ITEM_SUB = 6
ITEM_ROWS = ITEM_SUB * MOE_BLOCK


def _expert_kernel(ie_ref, ib_ref, ins_ref, iord_ref, inxt_ref, ni_ref, xb_ref, wg_ref, wu_ref, wd_ref, yb_ref,
                   xbuf, ybuf, wf32, wgb, wub, wdb, insem, outsem, wsem):
    i = pl.program_id(0)
    n_steps = pl.num_programs(0)
    ni = ni_ref[0]
    slot = i % 2

    def w_copies(e, s):
        return [pltpu.make_async_copy(w.at[e], wf32.at[s, k], wsem.at[s]) for k, w in enumerate((wg_ref, wu_ref, wd_ref))]

    def rows_of(item, j):
        return pl.ds(pl.multiple_of((ib_ref[item] + j) * MOE_BLOCK, MOE_BLOCK), MOE_BLOCK)

    def in_copy(item, s, j):
        return pltpu.make_async_copy(xb_ref.at[rows_of(item, j)], xbuf.at[s, pl.ds(j * MOE_BLOCK, MOE_BLOCK)], insem.at[s])

    def out_copy(item, s, j):
        return pltpu.make_async_copy(ybuf.at[s, pl.ds(j * MOE_BLOCK, MOE_BLOCK)], yb_ref.at[rows_of(item, j)], outsem.at[s])

    def for_blocks(item, fn):
        for j in range(ITEM_SUB):
            @pl.when(j < ins_ref[item])
            def _():
                fn(j)

    @pl.when(i == 0)
    def _():
        for c in w_copies(ie_ref[0], 0):
            c.start()
        for_blocks(0, lambda j: in_copy(0, 0, j).start())

    @pl.when(i < ni)
    def _():
        @pl.when(i >= 2)
        def _():
            for_blocks(i - 2, lambda j: out_copy(i - 2, slot, j).wait())

        @pl.when(i + 1 < ni)
        def _():
            for_blocks(i + 1, lambda j: in_copy(i + 1, 1 - slot, j).start())

        changed = (i == 0) | (ie_ref[i] != ie_ref[jnp.maximum(i - 1, 0)])

        @pl.when(changed)
        def _():
            ws = iord_ref[i] % 2
            for c in w_copies(ie_ref[i], ws):
                c.wait()

            @pl.when(inxt_ref[i] >= 0)
            def _():
                for c in w_copies(inxt_ref[i], 1 - ws):
                    c.start()

            wgb[...] = wf32[ws, 0].astype(BF16)
            wub[...] = wf32[ws, 1].astype(BF16)
            wdb[...] = wf32[ws, 2].astype(BF16)

        for_blocks(i, lambda j: in_copy(i, slot, j).wait())

        for ns in range(1, ITEM_SUB + 1):
            @pl.when(ins_ref[i] == ns)
            def _():
                rows = pl.ds(0, ns * MOE_BLOCK)
                x = xbuf[slot, rows].astype(BF16)
                g = _dot(x, wgb[...])
                u = _dot(x, wub[...])
                hid = (g * (1.0 / (1.0 + jnp.exp(-g))) * u).astype(BF16)
                ybuf[slot, rows] = _dot(hid, wdb[...])

        for_blocks(i, lambda j: out_copy(i, slot, j).start())

    @pl.when(i == n_steps - 1)
    def _():
        for back in (2, 1):
            k = ni - back

            @pl.when(k >= 0)
            def _():
                for_blocks(k, lambda j: out_copy(k, k % 2, j).wait())

        ybuf[0, pl.ds(0, MOE_BLOCK)] = jnp.zeros((MOE_BLOCK, D_MODEL), F32)
        last = jnp.maximum(ni - 1, 0)
        first_unused = ib_ref[last] + ins_ref[last]
        n_blocks = yb_ref.shape[0] // MOE_BLOCK

        def zero_copy(b):
            dst = yb_ref.at[pl.ds(pl.multiple_of(b * MOE_BLOCK, MOE_BLOCK), MOE_BLOCK)]
            return pltpu.make_async_copy(ybuf.at[0, pl.ds(0, MOE_BLOCK)], dst, outsem.at[0])

        def z_start(b, c):
            zero_copy(b).start()
            return c

        def z_wait(b, c):
            zero_copy(b).wait()
            return c

        lax.fori_loop(first_unused, n_blocks, z_start, 0)
        lax.fori_loop(first_unused, n_blocks, z_wait, 0)


def _expert_call(item_e, item_b, item_ns, item_ord, item_nxt, n_items, xb, wg, wu, wd):
    p_rows = xb.shape[0]
    max_items = item_e.shape[0]
    grid_spec = pltpu.PrefetchScalarGridSpec(
        num_scalar_prefetch=6,
        grid=(max_items,),
        in_specs=[pl.BlockSpec(memory_space=pl.ANY)] * 4,
        out_specs=pl.BlockSpec(memory_space=pl.ANY),
        scratch_shapes=[
            pltpu.VMEM((2, ITEM_ROWS, D_MODEL), F32),
            pltpu.VMEM((2, ITEM_ROWS, D_MODEL), F32),
            pltpu.VMEM((2, 3, D_MODEL, D_MODEL), F32),
            pltpu.VMEM((D_MODEL, D_MODEL), BF16),
            pltpu.VMEM((D_MODEL, D_MODEL), BF16),
            pltpu.VMEM((D_MODEL, D_MODEL), BF16),
            pltpu.SemaphoreType.DMA((2,)),
            pltpu.SemaphoreType.DMA((2,)),
            pltpu.SemaphoreType.DMA((2,)),
        ],
    )
    return pl.pallas_call(
        _expert_kernel,
        grid_spec=grid_spec,
        out_shape=jax.ShapeDtypeStruct((p_rows, D_MODEL), F32),
        compiler_params=_cparams(("arbitrary",)),
        name="experts",
    )(item_e, item_b, item_ns, item_ord, item_nxt, n_items, xb, wg, wu, wd)


def _combine_kernel(dcur_ref, dnxt_ref, yb_ref, x1_ref, wts_ref, g2_ref, lg_ref, lb_ref, o_ref, buf, sem):
    i = pl.program_id(0)
    tm = x1_ref.shape[0]
    slot = i % 2

    def issue(dref, s):
        def start(n, c):
            pltpu.make_async_copy(_row(yb_ref, dref[0, 0, n]), _row(buf.at[s, 0], n), sem.at[s]).start()
            pltpu.make_async_copy(_row(yb_ref, dref[0, 0, tm + n]), _row(buf.at[s, 1], n), sem.at[s]).start()
            return c

        lax.fori_loop(0, tm, start, 0, unroll=ISSUE_UNROLL)

    @pl.when(i == 0)
    def _():
        issue(dcur_ref, 0)

    @pl.when(i + 1 < pl.num_programs(0))
    def _():
        issue(dnxt_ref, 1 - slot)

    for k in range(2):
        pltpu.make_async_copy(yb_ref.at[pl.ds(0, tm)], buf.at[slot, k], sem.at[slot]).wait()
    w = wts_ref[...]
    f = w[:, 0:1] * buf[slot, 0] + w[:, 64:65] * buf[slot, 1]
    o_ref[...] = _layer_norm(DEEPNORM_ALPHA * x1_ref[...] + g2_ref[0] * f, lg_ref[...], lb_ref[...])


def _combine_call(dest3, yb, x1, wts, g2, lg, lb, tiles_per_batch):
    n = x1.shape[0]
    tm = TOK_TILE
    tpb = tiles_per_batch
    row = lambda i: (i, 0)
    full = lambda i: (0, 0)
    nt = n // tm
    return pl.pallas_call(
        _combine_kernel,
        grid=(nt,),
        in_specs=[
            pl.BlockSpec((1, 1, 2 * tm), lambda i: (i, 0, 0), memory_space=pltpu.SMEM),
            pl.BlockSpec((1, 1, 2 * tm), lambda i: (jnp.minimum(i + 1, nt - 1), 0, 0), memory_space=pltpu.SMEM),
            pl.BlockSpec(memory_space=pl.ANY),
            pl.BlockSpec((tm, D_MODEL), row),
            pl.BlockSpec((tm, LANES), row),
            pl.BlockSpec((1, 1, D_MODEL), lambda i: (i // tpb, 0, 0)),
            pl.BlockSpec((1, D_MODEL), full),
            pl.BlockSpec((1, D_MODEL), full),
        ],
        out_specs=pl.BlockSpec((tm, D_MODEL), row),
        out_shape=jax.ShapeDtypeStruct((n, D_MODEL), F32),
        scratch_shapes=[pltpu.VMEM((2, 2, tm, D_MODEL), F32), pltpu.SemaphoreType.DMA((2,))],
        compiler_params=_cparams(("arbitrary",)),
        name="combine",
    )(dest3, dest3, yb, x1, wts, g2, lg, lb)


def _head_interleave(w, lo):
    blk = w[:, lo:lo + 2 * QK_COLS].reshape(D_MODEL, 2, DA_HEADS, DA_HEAD_DIM)
    return blk.transpose(0, 2, 1, 3).reshape(D_MODEL, 2 * QK_COLS)


def _cumsum_small(x):
    idx = jnp.arange(x.shape[0], dtype=jnp.int32)
    return jnp.sum(jnp.where(idx[None, :] <= idx[:, None], x[None, :], 0), -1)


def _lookup(table, idx):
    return jnp.sum(jnp.where(idx[:, None] == jnp.arange(table.shape[0], dtype=jnp.int32)[None, :], table[None, :], 0), -1)


def _rope_tables(seq):
    rows_n = seq // GRID_W
    rows = jnp.repeat(jnp.arange(rows_n, dtype=F32), GRID_W)
    cols = jnp.tile(jnp.arange(GRID_W, dtype=F32), rows_n)
    half = DA_HEAD_DIM // 4
    inv = ROPE_THETA ** (-jnp.arange(half, dtype=F32) / half)
    ang_r = rows[:, None] * inv[None, :]
    ang_c = cols[:, None] * inv[None, :]
    cos64 = jnp.concatenate([jnp.cos(ang_r), jnp.cos(ang_r), jnp.cos(ang_c), jnp.cos(ang_c)], -1)
    sin64 = jnp.concatenate([-jnp.sin(ang_r), jnp.sin(ang_r), -jnp.sin(ang_c), jnp.sin(ang_c)], -1)
    return jnp.tile(cos64, (1, 2)), jnp.tile(sin64, (1, 2))


def kernel(x, c, ctx, c_ctx, w_mod, b_mod, w_in, lam_q1, lam_k1, lam_q2, lam_k2, subln_g, sg_ln_g, sg_ln_b, sg_w, sg_b, w_out, ln1_g, ln1_b, router_group_w, router_group_b, router_expert_w, router_expert_b, exp_w_gate, exp_w_up, exp_w_down, ln2_g, ln2_b):
    b, l, d = x.shape
    n = b * l
    tm = TOK_TILE
    nt = n // tm

    cc = jnp.zeros((b + 8, d), F32).at[:b].set(c).at[b].set(c_ctx)
    mod = _mod_call(cc, w_mod[0], b_mod[0][None, :])
    sh1, sc1, g1, sh2, sc2, g2 = [mod[:b, j * d:(j + 1) * d].reshape(b, 1, d) for j in range(6)]
    csh1 = mod[b:b + 1, 0:d]
    csc1 = mod[b:b + 1, d:2 * d]

    wi = w_in[0]
    w_all = jnp.concatenate([_head_interleave(wi, 0), _head_interleave(wi, KV_LO), wi[:, 2 * KV_LO:]], -1).astype(BF16)
    kc, vc = _ctx_call(ctx, csc1, csh1, w_all[:, KV_LO:KV_HI])

    cos_t, sin_t = _rope_tables(l)
    sbias = jnp.repeat(sg_b[0].T, LANES, axis=1)
    q, k, v, sg = _in_call(x, sc1, sh1, w_all, cos_t, sin_t, sg_ln_g[0][None, :], sg_ln_b[0][None, :],
                           sg_w[0].astype(BF16), sbias)

    lamv = jnp.stack([lam_q1[0], lam_k1[0], lam_q2[0], lam_k2[0]]).astype(F32)
    da = _attn_call(lamv, q, k, kc, v, vc, subln_g[0][None, :])

    e_lo = MOE_EXPERTS_PER_GROUP
    wr = jnp.zeros((LANES, d), F32).at[:MOE_GROUPS].set(router_group_w[0].T).at[e_lo:e_lo + N_EXPERTS].set(router_expert_w[0].T)
    br = jnp.zeros((LANES,), F32).at[:MOE_GROUPS].set(router_group_b[0]).at[e_lo:e_lo + N_EXPERTS].set(router_expert_b[0])
    br = jnp.broadcast_to(br[:, None], (LANES, tm))
    wr_hi = wr.astype(BF16)
    wr_split = jnp.concatenate([wr_hi, (wr - wr_hi.astype(F32)).astype(BF16)], 0)
    x1, t, route, wts, tcnt = _out_call(da.reshape(n, DA_WIDTH), sg.reshape(n, SG_WIDTH), x.reshape(n, d), g1, sc2, sh2,
                                        w_out[0].astype(BF16), ln1_g[0][None, :], ln1_b[0][None, :], wr_split, br, l // tm)

    cnt_te = tcnt[:, :, 0].astype(jnp.int32)
    counts = jnp.sum(cnt_te, 0)
    padded = (counts + MOE_BLOCK - 1) // MOE_BLOCK * MOE_BLOCK
    pad_end = _cumsum_small(padded)
    pad_start = pad_end - padded
    tix = jnp.arange(nt, dtype=jnp.int32)
    tile_prefix = jnp.sum(jnp.where(tix[None, :, None] < tix[:, None, None], cnt_te[None], 0), 1)
    base = pad_start[None, :] + tile_prefix
    ridx = route[:4].astype(jnp.int32).reshape(4, nt, tm)
    ex = jnp.arange(N_EXPERTS, dtype=jnp.int32)

    def slot_dest(eid, rank):
        return jnp.sum(jnp.where(eid[None] == ex[:, None, None], base.T[:, :, None], 0), 0) + rank

    dest3 = jnp.concatenate([slot_dest(ridx[0], ridx[2]), slot_dest(ridx[1], ridx[3])], -1).reshape(nt, 1, 2 * tm)
    n_blocks = (n * 2) // MOE_BLOCK + N_EXPERTS
    p_rows = n_blocks * MOE_BLOCK
    nb_e = padded // MOE_BLOCK
    items_e = (nb_e + ITEM_SUB - 1) // ITEM_SUB
    item_end = _cumsum_small(items_e)
    max_items = (n_blocks + (ITEM_SUB - 1) * N_EXPERTS) // ITEM_SUB
    it = jnp.arange(max_items, dtype=jnp.int32)
    item_e = jnp.minimum(jnp.sum((it[:, None] >= item_end[None, :]).astype(jnp.int32), -1), N_EXPERTS - 1)
    item_j = it - _lookup(item_end - items_e, item_e)
    item_b = _lookup(pad_start // MOE_BLOCK, item_e) + ITEM_SUB * item_j
    item_ns = jnp.clip(_lookup(nb_e, item_e) - ITEM_SUB * item_j, 0, ITEM_SUB)
    nonempty = (items_e > 0).astype(jnp.int32)
    ord_e = _cumsum_small(nonempty) - 1
    later = (ex[None, :] > ex[:, None]) & (nonempty[None, :] > 0)
    nxt_e = jnp.min(jnp.where(later, ex[None, :], N_EXPERTS), -1)
    nxt_e = jnp.where(nxt_e == N_EXPERTS, -1, nxt_e)
    item_ord = _lookup(ord_e, item_e)
    item_nxt = _lookup(nxt_e, item_e)

    xb = _dispatch_call((pad_start + counts).astype(jnp.int32), pad_end.astype(jnp.int32), dest3, t, p_rows)
    yb = _expert_call(item_e, item_b.astype(jnp.int32), item_ns.astype(jnp.int32), item_ord.astype(jnp.int32),
                      item_nxt.astype(jnp.int32), item_end[-1:].astype(jnp.int32),
                      xb, exp_w_gate[0], exp_w_up[0], exp_w_down[0])
    out = _combine_call(dest3, yb, x1, wts, g2, ln2_g[0][None, :], ln2_b[0][None, :], l // tm)
    return out.reshape(b, l, d)
```

```python
import math

import jax
import jax.numpy as jnp
from jax import lax
from jax.experimental import pallas as pl
from jax.experimental.pallas import tpu as pltpu

F32 = jnp.float32
BF16 = jnp.bfloat16

D_MODEL = 1024
GRID_W = 64
DA_HEAD_DIM = 64
DA_V_DIM = 128
DA_WIDTH = 512
DA_HEADS = 4
QK_COLS = 256
SG_CHUNK = 128
SG_WIDTH = 512
SG_GROUPS = 4
KV_LO = 512
KV_HI = 1536
IN_COLS = 2560
ROPE_THETA = 10000.0
MOE_GROUPS = 4
MOE_EXPERTS_PER_GROUP = 8
N_EXPERTS = 32
MOE_BLOCK = 128
EPS = 1e-5
DEPTH = 1
DEEPNORM_ALPHA = (2.0 * DEPTH) ** 0.25
LAM_INIT = 0.8 - 0.6 * math.exp(-0.3 * 0)
Q_SCALE = DA_HEAD_DIM ** -0.5
LOG2E = math.log2(math.e)
SQRT_HALF = math.sqrt(0.5)

LANES = 128
TOK_TILE = 1024
IN_SUB = 512
ATTN_TQ = 2048
ATTN_SUB = 256
VMEM_LIMIT = 56 * 1024 * 1024


def _cparams(sem):
    return pltpu.CompilerParams(dimension_semantics=sem, vmem_limit_bytes=VMEM_LIMIT)


def _dot(a, b):
    return jnp.dot(a, b, preferred_element_type=F32)


def _dot_hi(a, b):
    return jnp.dot(a, b, preferred_element_type=F32, precision=lax.Precision.HIGHEST)


def _layer_norm(y, g, b):
    mu = jnp.mean(y, -1, keepdims=True)
    yc = y - mu
    var = jnp.mean(yc * yc, -1, keepdims=True)
    return yc * lax.rsqrt(var + EPS) * g + b


def _row(ref, n):
    return ref.at[pl.ds(n, 1)]


def _mod_kernel(c_ref, w_ref, b_ref, o_ref):
    c = c_ref[...]
    s = c * (1.0 / (1.0 + jnp.exp(-c)))
    o_ref[...] = _dot_hi(s, w_ref[...]) + b_ref[...]


def _mod_call(cc, w_mod, b_mod):
    rows = cc.shape[0]
    n_out = w_mod.shape[1]
    bn = 1024
    return pl.pallas_call(
        _mod_kernel,
        grid=(n_out // bn,),
        in_specs=[
            pl.BlockSpec((rows, D_MODEL), lambda j: (0, 0)),
            pl.BlockSpec((D_MODEL, bn), lambda j: (0, j)),
            pl.BlockSpec((1, bn), lambda j: (0, j)),
        ],
        out_specs=pl.BlockSpec((rows, bn), lambda j: (0, j)),
        out_shape=jax.ShapeDtypeStruct((rows, n_out), F32),
        compiler_params=_cparams(("arbitrary",)),
        name="mod",
    )(cc, w_mod, b_mod)


CTX_BATCH = 4


def _ctx_kernel(c_ref, sc_ref, sh_ref, w_ref, kc_ref, vc_ref):
    nb, cl, _ = c_ref.shape
    c = c_ref[...].reshape(nb * cl, D_MODEL)
    h = (c * (1.0 + sc_ref[...]) + sh_ref[...]).astype(BF16)
    p = _dot(h, w_ref[...])
    kc_ref[...] = p[:, :2 * QK_COLS].astype(BF16).reshape(nb, cl, 2 * QK_COLS)
    vc_ref[...] = p[:, 2 * QK_COLS:].astype(BF16).reshape(nb, cl, DA_WIDTH)


def _ctx_call(ctx, csc1, csh1, w_kv):
    b, cl, _ = ctx.shape
    nb = CTX_BATCH
    return pl.pallas_call(
        _ctx_kernel,
        grid=(b // nb,),
        in_specs=[
            pl.BlockSpec((nb, cl, D_MODEL), lambda i: (i, 0, 0)),
            pl.BlockSpec((1, D_MODEL), lambda i: (0, 0)),
            pl.BlockSpec((1, D_MODEL), lambda i: (0, 0)),
            pl.BlockSpec((D_MODEL, 2 * QK_COLS + DA_WIDTH), lambda i: (0, 0)),
        ],
        out_specs=[
            pl.BlockSpec((nb, cl, 2 * QK_COLS), lambda i: (i, 0, 0)),
            pl.BlockSpec((nb, cl, DA_WIDTH), lambda i: (i, 0, 0)),
        ],
        out_shape=[
            jax.ShapeDtypeStruct((b, cl, 2 * QK_COLS), BF16),
            jax.ShapeDtypeStruct((b, cl, DA_WIDTH), BF16),
        ],
        compiler_params=_cparams(("arbitrary",)),
        name="ctx_kv",
    )(ctx, csc1, csh1, w_kv)


def _in_kernel(x_ref, sc_ref, sh_ref, w_ref, cos_ref, sin_ref, lng_ref, lnb_ref,
               sw_ref, sb_ref, q_ref, k_ref, v_ref, sg_ref):
    tm = x_ref.shape[1]
    sub = IN_SUB
    lane = lax.broadcasted_iota(jnp.int32, (sub, LANES), 1)
    first = (lane % 32) < 16

    for r0 in range(0, tm, sub):
        rr = pl.ds(r0, sub)
        h = (x_ref[0, rr, :] * (1.0 + sc_ref[0]) + sh_ref[0]).astype(BF16)
        p = _dot(h, w_ref[...])

        cos = cos_ref[rr, :]
        sin = sin_ref[rr, :]

        def rope(t):
            partner = jnp.where(first, pltpu.roll(t, LANES - 16, 1), pltpu.roll(t, 16, 1))
            return t * cos + partner * sin

        for c in range(4):
            cs = slice(c * LANES, (c + 1) * LANES)
            q_ref[0, rr, cs] = rope(p[:, cs] * (Q_SCALE * LOG2E)).astype(BF16)
            k_ref[0, rr, cs] = rope(p[:, KV_LO + c * LANES:KV_LO + (c + 1) * LANES]).astype(BF16)
        v_ref[0, rr, :] = p[:, 2 * KV_LO:KV_HI].astype(BF16)

        z = p[:, KV_HI:]
        gz = 0.5 * z * (1.0 + lax.erf(z * SQRT_HALF))
        u = gz[:, :SG_WIDTH]
        vn = _layer_norm(gz[:, SG_WIDTH:], lng_ref[...], lnb_ref[...]).astype(BF16)
        for c in range(sub // SG_CHUNK):
            rs = slice(c * SG_CHUNK, (c + 1) * SG_CHUNK)
            for g in range(SG_GROUPS):
                cs = slice(g * LANES, (g + 1) * LANES)
                s = _dot(sw_ref[g], vn[rs, cs]) + sb_ref[:, cs]
                sg_ref[0, pl.ds(r0 + c * SG_CHUNK, SG_CHUNK), cs] = (u[rs, cs] * s).astype(BF16)


def _in_call(x, sc1, sh1, w_in, cos_t, sin_t, lng, lnb, sw, sbias):
    b, l, _ = x.shape
    tm = TOK_TILE
    full = lambda bi, i: (0, 0)
    return pl.pallas_call(
        _in_kernel,
        grid=(b, l // tm),
        in_specs=[
            pl.BlockSpec((1, tm, D_MODEL), lambda bi, i: (bi, i, 0)),
            pl.BlockSpec((1, 1, D_MODEL), lambda bi, i: (bi, 0, 0)),
            pl.BlockSpec((1, 1, D_MODEL), lambda bi, i: (bi, 0, 0)),
            pl.BlockSpec((D_MODEL, IN_COLS), full),
            pl.BlockSpec((tm, LANES), lambda bi, i: (i, 0)),
            pl.BlockSpec((tm, LANES), lambda bi, i: (i, 0)),
            pl.BlockSpec((1, SG_WIDTH), full),
            pl.BlockSpec((1, SG_WIDTH), full),
            pl.BlockSpec((SG_GROUPS, SG_CHUNK, SG_CHUNK), lambda bi, i: (0, 0, 0)),
            pl.BlockSpec((SG_CHUNK, SG_WIDTH), full),
        ],
        out_specs=[pl.BlockSpec((1, tm, 512), lambda bi, i: (bi, i, 0))] * 4,
        out_shape=[jax.ShapeDtypeStruct((b, l, 512), BF16)] * 4,
        compiler_params=_cparams(("arbitrary", "arbitrary")),
        name="in_proj",
    )(x, sc1, sh1, w_in, cos_t, sin_t, lng, lnb, sw, sbias)


def _attn_kernel(lam_ref, q_ref, k_ref, kc_ref, v_ref, vc_ref, g_ref, o_ref, kall, vall):
    lv = lam_ref[...]
    lam = (jnp.exp(jnp.sum(lv[0:1] * lv[1:2], -1, keepdims=True))
           - jnp.exp(jnp.sum(lv[2:3] * lv[3:4], -1, keepdims=True)) + LAM_INIT)
    l = k_ref.shape[1]

    @pl.when(pl.program_id(2) == 0)
    def _():
        kall[:l, :] = k_ref[0]
        kall[l:, :] = kc_ref[0]
        vall[:l, :DA_V_DIM] = v_ref[0]
        vall[l:, :DA_V_DIM] = vc_ref[0]
        vall[:, DA_V_DIM:] = jnp.ones((vall.shape[0], DA_V_DIM), BF16)

    nt = (((1,), (1,)), ((), ()))

    def branch(qm):
        s = lax.dot_general(qm, kall[...], nt, preferred_element_type=F32)
        m = jnp.max(s, -1, keepdims=True)
        e = jnp.exp2((s - m).astype(BF16))
        oe = _dot(e, vall[...])
        return oe[:, :DA_V_DIM] / oe[:, DA_V_DIM:DA_V_DIM + 1]

    for r in range(q_ref.shape[1] // ATTN_SUB):
        rs = pl.ds(r * ATTN_SUB, ATTN_SUB)
        q = q_ref[0, rs, :]
        lane = lax.broadcasted_iota(jnp.int32, q.shape, 1)
        zero = jnp.zeros_like(q)
        o = branch(jnp.where(lane < DA_HEAD_DIM, q, zero)) - lam * branch(jnp.where(lane >= DA_HEAD_DIM, q, zero))
        of = o * lax.rsqrt(jnp.mean(o * o, -1, keepdims=True) + EPS) * g_ref[...]
        o_ref[0, rs, :] = (of * (1.0 - LAM_INIT)).astype(BF16)


def _attn_call(lamv, q, k, kc, v, vc, subln_g):
    b, l, _ = q.shape
    cl = kc.shape[1]
    tq = ATTN_TQ
    return pl.pallas_call(
        _attn_kernel,
        grid=(b, DA_HEADS, l // tq),
        in_specs=[
            pl.BlockSpec((4, DA_HEAD_DIM), lambda bi, h, i: (0, 0)),
            pl.BlockSpec((1, tq, LANES), lambda bi, h, i: (bi, i, h)),
            pl.BlockSpec((1, l, LANES), lambda bi, h, i: (bi, 0, h)),
            pl.BlockSpec((1, cl, LANES), lambda bi, h, i: (bi, 0, h)),
            pl.BlockSpec((1, l, LANES), lambda bi, h, i: (bi, 0, h)),
            pl.BlockSpec((1, cl, LANES), lambda bi, h, i: (bi, 0, h)),
            pl.BlockSpec((1, DA_V_DIM), lambda bi, h, i: (0, 0)),
        ],
        out_specs=pl.BlockSpec((1, tq, LANES), lambda bi, h, i: (bi, i, h)),
        out_shape=jax.ShapeDtypeStruct((b, l, DA_WIDTH), BF16),
        scratch_shapes=[pltpu.VMEM((l + cl, LANES), BF16), pltpu.VMEM((l + cl, 2 * DA_V_DIM), BF16)],
        compiler_params=_cparams(("arbitrary", "arbitrary", "arbitrary")),
        name="attn",
    )(lamv, q, k, kc, v, vc, subln_g)


def _out_kernel(da_ref, sg_ref, x_ref, g1_ref, sc_ref, sh_ref, w_ref, lg_ref, lb_ref,
                wr_ref, br_ref, x1_ref, t_ref, route_ref, wts_ref, cnt_ref):
    tm = x_ref.shape[0]
    y = _dot(da_ref[...], w_ref[:DA_WIDTH, :]) + _dot(sg_ref[...], w_ref[DA_WIDTH:, :])
    x1 = _layer_norm(DEEPNORM_ALPHA * x_ref[...] + g1_ref[0] * y, lg_ref[...], lb_ref[...])
    x1_ref[...] = x1
    t = x1 * (1.0 + sc_ref[0]) + sh_ref[0]
    t_ref[...] = t

    t_hi = t.astype(BF16)
    t_lo = (t - t_hi.astype(F32)).astype(BF16)
    nt_dims = (((1,), (1,)), ((), ()))
    hw = lax.dot_general(wr_ref[...], t_hi, nt_dims, preferred_element_type=F32)
    lt = (hw[:LANES] + hw[LANES:] + lax.dot_general(wr_ref[:LANES, :], t_lo, nt_dims, preferred_element_type=F32)
          + br_ref[...])
    epg = MOE_EXPERTS_PER_GROUP
    row8 = lax.broadcasted_iota(jnp.int32, (epg, tm), 0).astype(F32)
    ninf = jnp.float32(-jnp.inf)
    big = jnp.float32(epg)
    gmask = row8 < MOE_GROUPS
    gl = jnp.where(gmask, lt[0:epg], ninf)
    gmax = jnp.max(gl, 0, keepdims=True)
    gsel = jnp.min(jnp.where(gl == gmax, row8, big), 0, keepdims=True)
    gsum = jnp.sum(jnp.where(gmask, jnp.exp(gl - gmax), 0.0), 0, keepdims=True)
    gw = 1.0 / gsum
    el = lt[epg:2 * epg]
    for g in range(1, MOE_GROUPS):
        el = jnp.where(gsel == g, lt[epg * (g + 1):epg * (g + 2)], el)
    v1 = jnp.max(el, 0, keepdims=True)
    i1 = jnp.min(jnp.where(el == v1, row8, big), 0, keepdims=True)
    el2 = jnp.where(row8 == i1, ninf, el)
    v2 = jnp.max(el2, 0, keepdims=True)
    i2 = jnp.min(jnp.where(el2 == v2, row8, big), 0, keepdims=True)
    e = jnp.exp(v2 - v1)
    w1 = gw / (1.0 + e)
    w2 = gw * e / (1.0 + e)
    e1 = gsel * epg + i1
    e2 = gsel * epg + i2

    rowe = lax.broadcasted_iota(jnp.int32, (N_EXPERTS, tm), 0).astype(F32)
    oh1 = rowe == e1
    oh2 = rowe == e2
    oh = jnp.where(oh1 | oh2, 1.0, 0.0)
    r_i = lax.broadcasted_iota(jnp.int32, (tm, tm), 0)
    c_i = lax.broadcasted_iota(jnp.int32, (tm, tm), 1)
    upper = jnp.where(r_i < c_i, 1.0, 0.0).astype(BF16)
    pref = _dot(oh.astype(BF16), upper)
    r1 = jnp.sum(jnp.where(oh1, pref, 0.0), 0, keepdims=True)
    r2 = jnp.sum(jnp.where(oh2, pref, 0.0), 0, keepdims=True)
    cnt_ref[0] = jnp.broadcast_to(jnp.sum(oh, 1, keepdims=True), (N_EXPERTS, LANES))

    route_ref[...] = jnp.where(row8 == 0, e1, jnp.where(row8 == 1, e2, jnp.where(row8 == 2, r1, jnp.where(row8 == 3, r2, 0.0))))
    rowl = lax.broadcasted_iota(jnp.int32, (LANES, tm), 0)
    wts_ref[...] = jnp.where(rowl < 64, w1, w2).T


def _out_call(da, sg, x2d, g1, sc2, sh2, w_out, lg, lb, wr, br, tiles_per_batch):
    n = x2d.shape[0]
    tm = TOK_TILE
    nt = n // tm
    tpb = tiles_per_batch
    row = lambda i: (i, 0)
    full = lambda i: (0, 0)
    per_b = lambda i: (i // tpb, 0, 0)
    return pl.pallas_call(
        _out_kernel,
        grid=(nt,),
        in_specs=[
            pl.BlockSpec((tm, DA_WIDTH), row),
            pl.BlockSpec((tm, SG_WIDTH), row),
            pl.BlockSpec((tm, D_MODEL), row),
            pl.BlockSpec((1, 1, D_MODEL), per_b),
            pl.BlockSpec((1, 1, D_MODEL), per_b),
            pl.BlockSpec((1, 1, D_MODEL), per_b),
            pl.BlockSpec((D_MODEL, D_MODEL), full),
            pl.BlockSpec((1, D_MODEL), full),
            pl.BlockSpec((1, D_MODEL), full),
            pl.BlockSpec((2 * LANES, D_MODEL), full),
            pl.BlockSpec((LANES, tm), full),
        ],
        out_specs=[
            pl.BlockSpec((tm, D_MODEL), row),
            pl.BlockSpec((tm, D_MODEL), row),
            pl.BlockSpec((8, tm), lambda i: (0, i)),
            pl.BlockSpec((tm, LANES), row),
            pl.BlockSpec((1, N_EXPERTS, LANES), lambda i: (i, 0, 0)),
        ],
        out_shape=[
            jax.ShapeDtypeStruct((n, D_MODEL), F32),
            jax.ShapeDtypeStruct((n, D_MODEL), F32),
            jax.ShapeDtypeStruct((8, n), F32),
            jax.ShapeDtypeStruct((n, LANES), F32),
            jax.ShapeDtypeStruct((nt, N_EXPERTS, LANES), F32),
        ],
        compiler_params=_cparams(("arbitrary",)),
        name="out_proj",
    )(da, sg, x2d, g1, sc2, sh2, w_out, lg, lb, wr, br)


ISSUE_UNROLL = 8


def _dispatch_kernel(zs_ref, ze_ref, dest_ref, t_ref, xb_ref, zero_ref, sem):
    i = pl.program_id(0)
    tm = t_ref.shape[0]

    @pl.when(i == 0)
    def _():
        zero_ref[...] = jnp.zeros_like(zero_ref)

        def per_expert(e, carry):
            def start(r, c):
                pltpu.make_async_copy(_row(zero_ref, 0), _row(xb_ref, r), sem).start()
                return c

            def wait(r, c):
                pltpu.make_async_copy(_row(zero_ref, 0), _row(xb_ref, r), sem).wait()
                return c

            lax.fori_loop(zs_ref[e], ze_ref[e], start, 0)
            lax.fori_loop(zs_ref[e], ze_ref[e], wait, 0)
            return carry

        lax.fori_loop(0, N_EXPERTS, per_expert, 0)

        def blk_copy(j):
            return pltpu.make_async_copy(zero_ref, xb_ref.at[pl.ds(pl.multiple_of(j * MOE_BLOCK, MOE_BLOCK), MOE_BLOCK)], sem)

        def blk_start(j, c):
            blk_copy(j).start()
            return c

        def blk_wait(j, c):
            blk_copy(j).wait()
            return c

        first_unused = ze_ref[N_EXPERTS - 1] // MOE_BLOCK
        n_blocks = xb_ref.shape[0] // MOE_BLOCK
        lax.fori_loop(first_unused, n_blocks, blk_start, 0)
        lax.fori_loop(first_unused, n_blocks, blk_wait, 0)

    def start(n, c):
        pltpu.make_async_copy(_row(t_ref, n), _row(xb_ref, dest_ref[0, 0, n]), sem).start()
        pltpu.make_async_copy(_row(t_ref, n), _row(xb_ref, dest_ref[0, 0, tm + n]), sem).start()
        return c

    lax.fori_loop(0, tm, start, 0, unroll=ISSUE_UNROLL)
    for _ in range(2):
        pltpu.make_async_copy(t_ref, xb_ref.at[pl.ds(0, tm)], sem).wait()


def _dispatch_call(zs, ze, dest3, t, p_rows):
    n = t.shape[0]
    tm = TOK_TILE
    grid_spec = pltpu.PrefetchScalarGridSpec(
        num_scalar_prefetch=2,
        grid=(n // tm,),
        in_specs=[
            pl.BlockSpec((1, 1, 2 * tm), lambda i, zs, ze: (i, 0, 0), memory_space=pltpu.SMEM),
            pl.BlockSpec((tm, D_MODEL), lambda i, zs, ze: (i, 0)),
        ],
        out_specs=pl.BlockSpec(memory_space=pl.ANY),
        scratch_shapes=[pltpu.VMEM((MOE_BLOCK, D_MODEL), F32), pltpu.SemaphoreType.DMA(())],
    )
    return pl.pallas_call(
        _dispatch_kernel,
        grid_spec=grid_spec,
        out_shape=jax.ShapeDtypeStruct((p_rows, D_MODEL), F32),
        compiler_params=_cparams(("arbitrary",)),
        name="dispatch",
    )(zs, ze, dest3, t)


ITEM_SUB = 6
ITEM_ROWS = ITEM_SUB * MOE_BLOCK


def _expert_kernel(ie_ref, ib_ref, ins_ref, iord_ref, inxt_ref, ni_ref, xb_ref, wg_ref, wu_ref, wd_ref, yb_ref,
                   xbuf, ybuf, wf32, wgb, wub, wdb, insem, outsem, wsem):
    i = pl.program_id(0)
    n_steps = pl.num_programs(0)
    ni = ni_ref[0]
    slot = i % 2

    def w_copies(e, s):
        return [pltpu.make_async_copy(w.at[e], wf32.at[s, k], wsem.at[s]) for k, w in enumerate((wg_ref, wu_ref, wd_ref))]

    def rows_of(item, j):
        return pl.ds(pl.multiple_of((ib_ref[item] + j) * MOE_BLOCK, MOE_BLOCK), MOE_BLOCK)

    def in_copy(item, s, j):
        return pltpu.make_async_copy(xb_ref.at[rows_of(item, j)], xbuf.at[s, pl.ds(j * MOE_BLOCK, MOE_BLOCK)], insem.at[s])

    def out_copy(item, s, j):
        return pltpu.make_async_copy(ybuf.at[s, pl.ds(j * MOE_BLOCK, MOE_BLOCK)], yb_ref.at[rows_of(item, j)], outsem.at[s])

    def for_blocks(item, fn):
        for j in range(ITEM_SUB):
            @pl.when(j < ins_ref[item])
            def _():
                fn(j)

    @pl.when(i == 0)
    def _():
        for c in w_copies(ie_ref[0], 0):
            c.start()
        for_blocks(0, lambda j: in_copy(0, 0, j).start())

    @pl.when(i < ni)
    def _():
        @pl.when(i >= 2)
        def _():
            for_blocks(i - 2, lambda j: out_copy(i - 2, slot, j).wait())

        @pl.when(i + 1 < ni)
        def _():
            for_blocks(i + 1, lambda j: in_copy(i + 1, 1 - slot, j).start())

        changed = (i == 0) | (ie_ref[i] != ie_ref[jnp.maximum(i - 1, 0)])

        @pl.when(changed)
        def _():
            ws = iord_ref[i] % 2
            for c in w_copies(ie_ref[i], ws):
                c.wait()

            @pl.when(inxt_ref[i] >= 0)
            def _():
                for c in w_copies(inxt_ref[i], 1 - ws):
                    c.start()

            wgb[...] = wf32[ws, 0].astype(BF16)
            wub[...] = wf32[ws, 1].astype(BF16)
            wdb[...] = wf32[ws, 2].astype(BF16)

        for_blocks(i, lambda j: in_copy(i, slot, j).wait())

        for ns in range(1, ITEM_SUB + 1):
            @pl.when(ins_ref[i] == ns)
            def _():
                rows = pl.ds(0, ns * MOE_BLOCK)
                x = xbuf[slot, rows].astype(BF16)
                g = _dot(x, wgb[...])
                u = _dot(x, wub[...])
                hid = (g * (1.0 / (1.0 + jnp.exp(-g))) * u).astype(BF16)
                ybuf[slot, rows] = _dot(hid, wdb[...])

        for_blocks(i, lambda j: out_copy(i, slot, j).start())

    @pl.when(i == n_steps - 1)
    def _():
        for back in (2, 1):
            k = ni - back

            @pl.when(k >= 0)
            def _():
                for_blocks(k, lambda j: out_copy(k, k % 2, j).wait())

        ybuf[0, pl.ds(0, MOE_BLOCK)] = jnp.zeros((MOE_BLOCK, D_MODEL), F32)
        last = jnp.maximum(ni - 1, 0)
        first_unused = ib_ref[last] + ins_ref[last]
        n_blocks = yb_ref.shape[0] // MOE_BLOCK

        def zero_copy(b):
            dst = yb_ref.at[pl.ds(pl.multiple_of(b * MOE_BLOCK, MOE_BLOCK), MOE_BLOCK)]
            return pltpu.make_async_copy(ybuf.at[0, pl.ds(0, MOE_BLOCK)], dst, outsem.at[0])

        def z_start(b, c):
            zero_copy(b).start()
            return c

        def z_wait(b, c):
            zero_copy(b).wait()
            return c

        lax.fori_loop(first_unused, n_blocks, z_start, 0)
        lax.fori_loop(first_unused, n_blocks, z_wait, 0)


def _expert_call(item_e, item_b, item_ns, item_ord, item_nxt, n_items, xb, wg, wu, wd):
    p_rows = xb.shape[0]
    max_items = item_e.shape[0]
    grid_spec = pltpu.PrefetchScalarGridSpec(
        num_scalar_prefetch=6,
        grid=(max_items,),
        in_specs=[pl.BlockSpec(memory_space=pl.ANY)] * 4,
        out_specs=pl.BlockSpec(memory_space=pl.ANY),
        scratch_shapes=[
            pltpu.VMEM((2, ITEM_ROWS, D_MODEL), F32),
            pltpu.VMEM((2, ITEM_ROWS, D_MODEL), F32),
            pltpu.VMEM((2, 3, D_MODEL, D_MODEL), F32),
            pltpu.VMEM((D_MODEL, D_MODEL), BF16),
            pltpu.VMEM((D_MODEL, D_MODEL), BF16),
            pltpu.VMEM((D_MODEL, D_MODEL), BF16),
            pltpu.SemaphoreType.DMA((2,)),
            pltpu.SemaphoreType.DMA((2,)),
            pltpu.SemaphoreType.DMA((2,)),
        ],
    )
    return pl.pallas_call(
        _expert_kernel,
        grid_spec=grid_spec,
        out_shape=jax.ShapeDtypeStruct((p_rows, D_MODEL), F32),
        compiler_params=_cparams(("arbitrary",)),
        name="experts",
    )(item_e, item_b, item_ns, item_ord, item_nxt, n_items, xb, wg, wu, wd)


def _combine_kernel(dcur_ref, dnxt_ref, yb_ref, x1_ref, wts_ref, g2_ref, lg_ref, lb_ref, o_ref, buf, sem):
    i = pl.program_id(0)
    tm = x1_ref.shape[0]
    slot = i % 2

    def issue(dref, s):
        def start(n, c):
            pltpu.make_async_copy(_row(yb_ref, dref[0, 0, n]), _row(buf.at[s, 0], n), sem.at[s]).start()
            pltpu.make_async_copy(_row(yb_ref, dref[0, 0, tm + n]), _row(buf.at[s, 1], n), sem.at[s]).start()
            return c

        lax.fori_loop(0, tm, start, 0, unroll=ISSUE_UNROLL)

    @pl.when(i == 0)
    def _():
        issue(dcur_ref, 0)

    @pl.when(i + 1 < pl.num_programs(0))
    def _():
        issue(dnxt_ref, 1 - slot)

    for k in range(2):
        pltpu.make_async_copy(yb_ref.at[pl.ds(0, tm)], buf.at[slot, k], sem.at[slot]).wait()
    w = wts_ref[...]
    f = w[:, 0:1] * buf[slot, 0] + w[:, 64:65] * buf[slot, 1]
    o_ref[...] = _layer_norm(DEEPNORM_ALPHA * x1_ref[...] + g2_ref[0] * f, lg_ref[...], lb_ref[...])


def _combine_call(dest3, yb, x1, wts, g2, lg, lb, tiles_per_batch):
    n = x1.shape[0]
    tm = TOK_TILE
    tpb = tiles_per_batch
    row = lambda i: (i, 0)
    full = lambda i: (0, 0)
    nt = n // tm
    return pl.pallas_call(
        _combine_kernel,
        grid=(nt,),
        in_specs=[
            pl.BlockSpec((1, 1, 2 * tm), lambda i: (i, 0, 0), memory_space=pltpu.SMEM),
            pl.BlockSpec((1, 1, 2 * tm), lambda i: (jnp.minimum(i + 1, nt - 1), 0, 0), memory_space=pltpu.SMEM),
            pl.BlockSpec(memory_space=pl.ANY),
            pl.BlockSpec((tm, D_MODEL), row),
            pl.BlockSpec((tm, LANES), row),
            pl.BlockSpec((1, 1, D_MODEL), lambda i: (i // tpb, 0, 0)),
            pl.BlockSpec((1, D_MODEL), full),
            pl.BlockSpec((1, D_MODEL), full),
        ],
        out_specs=pl.BlockSpec((tm, D_MODEL), row),
        out_shape=jax.ShapeDtypeStruct((n, D_MODEL), F32),
        scratch_shapes=[pltpu.VMEM((2, 2, tm, D_MODEL), F32), pltpu.SemaphoreType.DMA((2,))],
        compiler_params=_cparams(("arbitrary",)),
        name="combine",
    )(dest3, dest3, yb, x1, wts, g2, lg, lb)


def _head_interleave(w, lo):
    blk = w[:, lo:lo + 2 * QK_COLS].reshape(D_MODEL, 2, DA_HEADS, DA_HEAD_DIM)
    return blk.transpose(0, 2, 1, 3).reshape(D_MODEL, 2 * QK_COLS)


def _cumsum_small(x):
    idx = jnp.arange(x.shape[0], dtype=jnp.int32)
    return jnp.sum(jnp.where(idx[None, :] <= idx[:, None], x[None, :], 0), -1)


def _lookup(table, idx):
    return jnp.sum(jnp.where(idx[:, None] == jnp.arange(table.shape[0], dtype=jnp.int32)[None, :], table[None, :], 0), -1)


def _rope_tables(seq):
    rows_n = seq // GRID_W
    rows = jnp.repeat(jnp.arange(rows_n, dtype=F32), GRID_W)
    cols = jnp.tile(jnp.arange(GRID_W, dtype=F32), rows_n)
    half = DA_HEAD_DIM // 4
    inv = ROPE_THETA ** (-jnp.arange(half, dtype=F32) / half)
    ang_r = rows[:, None] * inv[None, :]
    ang_c = cols[:, None] * inv[None, :]
    cos64 = jnp.concatenate([jnp.cos(ang_r), jnp.cos(ang_r), jnp.cos(ang_c), jnp.cos(ang_c)], -1)
    sin64 = jnp.concatenate([-jnp.sin(ang_r), jnp.sin(ang_r), -jnp.sin(ang_c), jnp.sin(ang_c)], -1)
    return jnp.tile(cos64, (1, 2)), jnp.tile(sin64, (1, 2))


def kernel(x, c, ctx, c_ctx, w_mod, b_mod, w_in, lam_q1, lam_k1, lam_q2, lam_k2, subln_g, sg_ln_g, sg_ln_b, sg_w, sg_b, w_out, ln1_g, ln1_b, router_group_w, router_group_b, router_expert_w, router_expert_b, exp_w_gate, exp_w_up, exp_w_down, ln2_g, ln2_b):
    b, l, d = x.shape
    n = b * l
    tm = TOK_TILE
    nt = n // tm

    cc = jnp.zeros((b + 8, d), F32).at[:b].set(c).at[b].set(c_ctx)
    mod = _mod_call(cc, w_mod[0], b_mod[0][None, :])
    sh1, sc1, g1, sh2, sc2, g2 = [mod[:b, j * d:(j + 1) * d].reshape(b, 1, d) for j in range(6)]
    csh1 = mod[b:b + 1, 0:d]
    csc1 = mod[b:b + 1, d:2 * d]

    wi = w_in[0]
    w_all = jnp.concatenate([_head_interleave(wi, 0), _head_interleave(wi, KV_LO), wi[:, 2 * KV_LO:]], -1).astype(BF16)
    kc, vc = _ctx_call(ctx, csc1, csh1, w_all[:, KV_LO:KV_HI])

    cos_t, sin_t = _rope_tables(l)
    sbias = jnp.repeat(sg_b[0].T, LANES, axis=1)
    q, k, v, sg = _in_call(x, sc1, sh1, w_all, cos_t, sin_t, sg_ln_g[0][None, :], sg_ln_b[0][None, :],
                           sg_w[0].astype(BF16), sbias)

    lamv = jnp.stack([lam_q1[0], lam_k1[0], lam_q2[0], lam_k2[0]]).astype(F32)
    da = _attn_call(lamv, q, k, kc, v, vc, subln_g[0][None, :])

    e_lo = MOE_EXPERTS_PER_GROUP
    wr = jnp.zeros((LANES, d), F32).at[:MOE_GROUPS].set(router_group_w[0].T).at[e_lo:e_lo + N_EXPERTS].set(router_expert_w[0].T)
    br = jnp.zeros((LANES,), F32).at[:MOE_GROUPS].set(router_group_b[0]).at[e_lo:e_lo + N_EXPERTS].set(router_expert_b[0])
    br = jnp.broadcast_to(br[:, None], (LANES, tm))
    wr_hi = wr.astype(BF16)
    wr_split = jnp.concatenate([wr_hi, (wr - wr_hi.astype(F32)).astype(BF16)], 0)
    x1, t, route, wts, tcnt = _out_call(da.reshape(n, DA_WIDTH), sg.reshape(n, SG_WIDTH), x.reshape(n, d), g1, sc2, sh2,
                                        w_out[0].astype(BF16), ln1_g[0][None, :], ln1_b[0][None, :], wr_split, br, l // tm)

    cnt_te = tcnt[:, :, 0].astype(jnp.int32)
    counts = jnp.sum(cnt_te, 0)
    padded = (counts + MOE_BLOCK - 1) // MOE_BLOCK * MOE_BLOCK
    pad_end = _cumsum_small(padded)
    pad_start = pad_end - padded
    tix = jnp.arange(nt, dtype=jnp.int32)
    tile_prefix = jnp.sum(jnp.where(tix[None, :, None] < tix[:, None, None], cnt_te[None], 0), 1)
    base = pad_start[None, :] + tile_prefix
    ridx = route[:4].astype(jnp.int32).reshape(4, nt, tm)
    ex = jnp.arange(N_EXPERTS, dtype=jnp.int32)

    def slot_dest(eid, rank):
        return jnp.sum(jnp.where(eid[None] == ex[:, None, None], base.T[:, :, None], 0), 0) + rank

    dest3 = jnp.concatenate([slot_dest(ridx[0], ridx[2]), slot_dest(ridx[1], ridx[3])], -1).reshape(nt, 1, 2 * tm)
    n_blocks = (n * 2) // MOE_BLOCK + N_EXPERTS
    p_rows = n_blocks * MOE_BLOCK
    nb_e = padded // MOE_BLOCK
    items_e = (nb_e + ITEM_SUB - 1) // ITEM_SUB
    item_end = _cumsum_small(items_e)
    max_items = (n_blocks + (ITEM_SUB - 1) * N_EXPERTS) // ITEM_SUB
    it = jnp.arange(max_items, dtype=jnp.int32)
    item_e = jnp.minimum(jnp.sum((it[:, None] >= item_end[None, :]).astype(jnp.int32), -1), N_EXPERTS - 1)
    item_j = it - _lookup(item_end - items_e, item_e)
    item_b = _lookup(pad_start // MOE_BLOCK, item_e) + ITEM_SUB * item_j
    item_ns = jnp.clip(_lookup(nb_e, item_e) - ITEM_SUB * item_j, 0, ITEM_SUB)
    nonempty = (items_e > 0).astype(jnp.int32)
    ord_e = _cumsum_small(nonempty) - 1
    later = (ex[None, :] > ex[:, None]) & (nonempty[None, :] > 0)
    nxt_e = jnp.min(jnp.where(later, ex[None, :], N_EXPERTS), -1)
    nxt_e = jnp.where(nxt_e == N_EXPERTS, -1, nxt_e)
    item_ord = _lookup(ord_e, item_e)
    item_nxt = _lookup(nxt_e, item_e)

    xb = _dispatch_call((pad_start + counts).astype(jnp.int32), pad_end.astype(jnp.int32), dest3, t, p_rows)
    yb = _expert_call(item_e, item_b.astype(jnp.int32), item_ns.astype(jnp.int32), item_ord.astype(jnp.int32),
                      item_nxt.astype(jnp.int32), item_end[-1:].astype(jnp.int32),
                      xb, exp_w_gate[0], exp_w_up[0], exp_w_down[0])
    out = _combine_call(dest3, yb, x1, wts, g2, ln2_g[0][None, :], ln2_b[0][None, :], l // tm)
    return out.reshape(b, l, d)
```

```python
import math

import jax
import jax.numpy as jnp
from jax import lax
from jax.experimental import pallas as pl
from jax.experimental.pallas import tpu as pltpu

F32 = jnp.float32
BF16 = jnp.bfloat16

D_MODEL = 1024
GRID_W = 64
DA_HEAD_DIM = 64
DA_V_DIM = 128
DA_WIDTH = 512
DA_HEADS = 4
QK_COLS = 256
SG_CHUNK = 128
SG_WIDTH = 512
SG_GROUPS = 4
KV_LO = 512
KV_HI = 1536
IN_COLS = 2560
ROPE_THETA = 10000.0
MOE_GROUPS = 4
MOE_EXPERTS_PER_GROUP = 8
N_EXPERTS = 32
MOE_BLOCK = 128
EPS = 1e-5
DEPTH = 1
DEEPNORM_ALPHA = (2.0 * DEPTH) ** 0.25
LAM_INIT = 0.8 - 0.6 * math.exp(-0.3 * 0)
Q_SCALE = DA_HEAD_DIM ** -0.5
LOG2E = math.log2(math.e)
SQRT_HALF = math.sqrt(0.5)

LANES = 128
TOK_TILE = 1024
DISPATCH_TILE = 2048
COMBINE_TILE = 512
IN_SUB = 512
ATTN_TQ = 2048
ATTN_SUB = 256
VMEM_LIMIT = 56 * 1024 * 1024


def _cparams(sem):
    return pltpu.CompilerParams(dimension_semantics=sem, vmem_limit_bytes=VMEM_LIMIT)


def _dot(a, b):
    return jnp.dot(a, b, preferred_element_type=F32)


def _dot_hi(a, b):
    return jnp.dot(a, b, preferred_element_type=F32, precision=lax.Precision.HIGHEST)


def _layer_norm(y, g, b):
    mu = jnp.mean(y, -1, keepdims=True)
    yc = y - mu
    var = jnp.mean(yc * yc, -1, keepdims=True)
    return yc * lax.rsqrt(var + EPS) * g + b


def _row(ref, n):
    return ref.at[pl.ds(n, 1)]


def _mod_kernel(c_ref, w_ref, b_ref, o_ref):
    c = c_ref[...]
    s = c * (1.0 / (1.0 + jnp.exp(-c)))
    o_ref[...] = _dot_hi(s, w_ref[...]) + b_ref[...]


def _mod_call(cc, w_mod, b_mod):
    rows = cc.shape[0]
    n_out = w_mod.shape[1]
    bn = 1024
    return pl.pallas_call(
        _mod_kernel,
        grid=(n_out // bn,),
        in_specs=[
            pl.BlockSpec((rows, D_MODEL), lambda j: (0, 0)),
            pl.BlockSpec((D_MODEL, bn), lambda j: (0, j)),
            pl.BlockSpec((1, bn), lambda j: (0, j)),
        ],
        out_specs=pl.BlockSpec((rows, bn), lambda j: (0, j)),
        out_shape=jax.ShapeDtypeStruct((rows, n_out), F32),
        compiler_params=_cparams(("arbitrary",)),
        name="mod",
    )(cc, w_mod, b_mod)


CTX_BATCH = 4


def _ctx_kernel(c_ref, sc_ref, sh_ref, w_ref, kc_ref, vc_ref):
    nb, cl, _ = c_ref.shape
    c = c_ref[...].reshape(nb * cl, D_MODEL)
    h = (c * (1.0 + sc_ref[...]) + sh_ref[...]).astype(BF16)
    p = _dot(h, w_ref[...])
    kc_ref[...] = p[:, :2 * QK_COLS].astype(BF16).reshape(nb, cl, 2 * QK_COLS)
    vc_ref[...] = p[:, 2 * QK_COLS:].astype(BF16).reshape(nb, cl, DA_WIDTH)


def _ctx_call(ctx, csc1, csh1, w_kv):
    b, cl, _ = ctx.shape
    nb = CTX_BATCH
    return pl.pallas_call(
        _ctx_kernel,
        grid=(b // nb,),
        in_specs=[
            pl.BlockSpec((nb, cl, D_MODEL), lambda i: (i, 0, 0)),
            pl.BlockSpec((1, D_MODEL), lambda i: (0, 0)),
            pl.BlockSpec((1, D_MODEL), lambda i: (0, 0)),
            pl.BlockSpec((D_MODEL, 2 * QK_COLS + DA_WIDTH), lambda i: (0, 0)),
        ],
        out_specs=[
            pl.BlockSpec((nb, cl, 2 * QK_COLS), lambda i: (i, 0, 0)),
            pl.BlockSpec((nb, cl, DA_WIDTH), lambda i: (i, 0, 0)),
        ],
        out_shape=[
            jax.ShapeDtypeStruct((b, cl, 2 * QK_COLS), BF16),
            jax.ShapeDtypeStruct((b, cl, DA_WIDTH), BF16),
        ],
        compiler_params=_cparams(("arbitrary",)),
        name="ctx_kv",
    )(ctx, csc1, csh1, w_kv)


def _in_kernel(x_ref, sc_ref, sh_ref, w_ref, cos_ref, sin_ref, lng_ref, lnb_ref,
               sw_ref, sb_ref, q_ref, k_ref, v_ref, sg_ref):
    tm = x_ref.shape[1]
    sub = IN_SUB
    lane = lax.broadcasted_iota(jnp.int32, (sub, LANES), 1)
    first = (lane % 32) < 16

    for r0 in range(0, tm, sub):
        rr = pl.ds(r0, sub)
        h = (x_ref[0, rr, :] * (1.0 + sc_ref[0]) + sh_ref[0]).astype(BF16)
        p = _dot(h, w_ref[...])

        cos = cos_ref[rr, :]
        sin = sin_ref[rr, :]

        def rope(t):
            partner = jnp.where(first, pltpu.roll(t, LANES - 16, 1), pltpu.roll(t, 16, 1))
            return t * cos + partner * sin

        for c in range(4):
            cs = slice(c * LANES, (c + 1) * LANES)
            q_ref[0, rr, cs] = rope(p[:, cs] * (Q_SCALE * LOG2E)).astype(BF16)
            k_ref[0, rr, cs] = rope(p[:, KV_LO + c * LANES:KV_LO + (c + 1) * LANES]).astype(BF16)
        v_ref[0, rr, :] = p[:, 2 * KV_LO:KV_HI].astype(BF16)

        z = p[:, KV_HI:]
        gz = 0.5 * z * (1.0 + lax.erf(z * SQRT_HALF))
        u = gz[:, :SG_WIDTH]
        vn = _layer_norm(gz[:, SG_WIDTH:], lng_ref[...], lnb_ref[...]).astype(BF16)
        for c in range(sub // SG_CHUNK):
            rs = slice(c * SG_CHUNK, (c + 1) * SG_CHUNK)
            for g in range(SG_GROUPS):
                cs = slice(g * LANES, (g + 1) * LANES)
                s = _dot(sw_ref[g], vn[rs, cs]) + sb_ref[:, cs]
                sg_ref[0, pl.ds(r0 + c * SG_CHUNK, SG_CHUNK), cs] = (u[rs, cs] * s).astype(BF16)


def _in_call(x, sc1, sh1, w_in, cos_t, sin_t, lng, lnb, sw, sbias):
    b, l, _ = x.shape
    tm = TOK_TILE
    full = lambda bi, i: (0, 0)
    return pl.pallas_call(
        _in_kernel,
        grid=(b, l // tm),
        in_specs=[
            pl.BlockSpec((1, tm, D_MODEL), lambda bi, i: (bi, i, 0)),
            pl.BlockSpec((1, 1, D_MODEL), lambda bi, i: (bi, 0, 0)),
            pl.BlockSpec((1, 1, D_MODEL), lambda bi, i: (bi, 0, 0)),
            pl.BlockSpec((D_MODEL, IN_COLS), full),
            pl.BlockSpec((tm, LANES), lambda bi, i: (i, 0)),
            pl.BlockSpec((tm, LANES), lambda bi, i: (i, 0)),
            pl.BlockSpec((1, SG_WIDTH), full),
            pl.BlockSpec((1, SG_WIDTH), full),
            pl.BlockSpec((SG_GROUPS, SG_CHUNK, SG_CHUNK), lambda bi, i: (0, 0, 0)),
            pl.BlockSpec((SG_CHUNK, SG_WIDTH), full),
        ],
        out_specs=[pl.BlockSpec((1, tm, 512), lambda bi, i: (bi, i, 0))] * 4,
        out_shape=[jax.ShapeDtypeStruct((b, l, 512), BF16)] * 4,
        compiler_params=_cparams(("arbitrary", "arbitrary")),
        name="in_proj",
    )(x, sc1, sh1, w_in, cos_t, sin_t, lng, lnb, sw, sbias)


def _attn_kernel(lam_ref, q_ref, k_ref, kc_ref, v_ref, vc_ref, g_ref, o_ref, kall, vall):
    lv = lam_ref[...]
    lam = (jnp.exp(jnp.sum(lv[0:1] * lv[1:2], -1, keepdims=True))
           - jnp.exp(jnp.sum(lv[2:3] * lv[3:4], -1, keepdims=True)) + LAM_INIT)
    l = k_ref.shape[1]

    @pl.when(pl.program_id(2) == 0)
    def _():
        kall[:l, :] = k_ref[0]
        kall[l:, :] = kc_ref[0]
        vall[:l, :DA_V_DIM] = v_ref[0]
        vall[l:, :DA_V_DIM] = vc_ref[0]
        vall[:, DA_V_DIM:] = jnp.ones((vall.shape[0], DA_V_DIM), BF16)

    nt = (((1,), (1,)), ((), ()))

    def branch(qm):
        s = lax.dot_general(qm, kall[...], nt, preferred_element_type=F32)
        m = jnp.max(s, -1, keepdims=True)
        e = jnp.exp2((s - m).astype(BF16))
        oe = _dot(e, vall[...])
        return oe[:, :DA_V_DIM] / oe[:, DA_V_DIM:DA_V_DIM + 1]

    for r in range(q_ref.shape[1] // ATTN_SUB):
        rs = pl.ds(r * ATTN_SUB, ATTN_SUB)
        q = q_ref[0, rs, :]
        lane = lax.broadcasted_iota(jnp.int32, q.shape, 1)
        zero = jnp.zeros_like(q)
        o = branch(jnp.where(lane < DA_HEAD_DIM, q, zero)) - lam * branch(jnp.where(lane >= DA_HEAD_DIM, q, zero))
        of = o * lax.rsqrt(jnp.mean(o * o, -1, keepdims=True) + EPS) * g_ref[...]
        o_ref[0, rs, :] = (of * (1.0 - LAM_INIT)).astype(BF16)


def _attn_call(lamv, q, k, kc, v, vc, subln_g):
    b, l, _ = q.shape
    cl = kc.shape[1]
    tq = ATTN_TQ
    return pl.pallas_call(
        _attn_kernel,
        grid=(b, DA_HEADS, l // tq),
        in_specs=[
            pl.BlockSpec((4, DA_HEAD_DIM), lambda bi, h, i: (0, 0)),
            pl.BlockSpec((1, tq, LANES), lambda bi, h, i: (bi, i, h)),
            pl.BlockSpec((1, l, LANES), lambda bi, h, i: (bi, 0, h)),
            pl.BlockSpec((1, cl, LANES), lambda bi, h, i: (bi, 0, h)),
            pl.BlockSpec((1, l, LANES), lambda bi, h, i: (bi, 0, h)),
            pl.BlockSpec((1, cl, LANES), lambda bi, h, i: (bi, 0, h)),
            pl.BlockSpec((1, DA_V_DIM), lambda bi, h, i: (0, 0)),
        ],
        out_specs=pl.BlockSpec((1, tq, LANES), lambda bi, h, i: (bi, i, h)),
        out_shape=jax.ShapeDtypeStruct((b, l, DA_WIDTH), BF16),
        scratch_shapes=[pltpu.VMEM((l + cl, LANES), BF16), pltpu.VMEM((l + cl, 2 * DA_V_DIM), BF16)],
        compiler_params=_cparams(("arbitrary", "arbitrary", "arbitrary")),
        name="attn",
    )(lamv, q, k, kc, v, vc, subln_g)


def _out_kernel(da_ref, sg_ref, x_ref, g1_ref, sc_ref, sh_ref, w_ref, lg_ref, lb_ref,
                wr_ref, br_ref, x1_ref, t_ref, route_ref, wts_ref, cnt_ref):
    tm = x_ref.shape[0]
    y = _dot(da_ref[...], w_ref[:DA_WIDTH, :]) + _dot(sg_ref[...], w_ref[DA_WIDTH:, :])
    x1 = _layer_norm(DEEPNORM_ALPHA * x_ref[...] + g1_ref[0] * y, lg_ref[...], lb_ref[...])
    x1_ref[...] = x1
    t = x1 * (1.0 + sc_ref[0]) + sh_ref[0]
    t_ref[...] = t

    t_hi = t.astype(BF16)
    t_lo = (t - t_hi.astype(F32)).astype(BF16)
    nt_dims = (((1,), (1,)), ((), ()))
    hw = lax.dot_general(wr_ref[...], t_hi, nt_dims, preferred_element_type=F32)
    lt = (hw[:LANES] + hw[LANES:] + lax.dot_general(wr_ref[:LANES, :], t_lo, nt_dims, preferred_element_type=F32)
          + br_ref[...])
    epg = MOE_EXPERTS_PER_GROUP
    row8 = lax.broadcasted_iota(jnp.int32, (epg, tm), 0).astype(F32)
    ninf = jnp.float32(-jnp.inf)
    big = jnp.float32(epg)
    gmask = row8 < MOE_GROUPS
    gl = jnp.where(gmask, lt[0:epg], ninf)
    gmax = jnp.max(gl, 0, keepdims=True)
    gsel = jnp.min(jnp.where(gl == gmax, row8, big), 0, keepdims=True)
    gsum = jnp.sum(jnp.where(gmask, jnp.exp(gl - gmax), 0.0), 0, keepdims=True)
    gw = 1.0 / gsum
    el = lt[epg:2 * epg]
    for g in range(1, MOE_GROUPS):
        el = jnp.where(gsel == g, lt[epg * (g + 1):epg * (g + 2)], el)
    v1 = jnp.max(el, 0, keepdims=True)
    i1 = jnp.min(jnp.where(el == v1, row8, big), 0, keepdims=True)
    el2 = jnp.where(row8 == i1, ninf, el)
    v2 = jnp.max(el2, 0, keepdims=True)
    i2 = jnp.min(jnp.where(el2 == v2, row8, big), 0, keepdims=True)
    e = jnp.exp(v2 - v1)
    w1 = gw / (1.0 + e)
    w2 = gw * e / (1.0 + e)
    e1 = gsel * epg + i1
    e2 = gsel * epg + i2

    rowe = lax.broadcasted_iota(jnp.int32, (N_EXPERTS, tm), 0).astype(F32)
    oh1 = rowe == e1
    oh2 = rowe == e2
    oh = jnp.where(oh1 | oh2, 1.0, 0.0)
    r_i = lax.broadcasted_iota(jnp.int32, (tm, tm), 0)
    c_i = lax.broadcasted_iota(jnp.int32, (tm, tm), 1)
    upper = jnp.where(r_i < c_i, 1.0, 0.0).astype(BF16)
    pref = _dot(oh.astype(BF16), upper)
    r1 = jnp.sum(jnp.where(oh1, pref, 0.0), 0, keepdims=True)
    r2 = jnp.sum(jnp.where(oh2, pref, 0.0), 0, keepdims=True)
    cnt_ref[0] = jnp.broadcast_to(jnp.sum(oh, 1, keepdims=True), (N_EXPERTS, LANES))

    route_ref[...] = jnp.where(row8 == 0, e1, jnp.where(row8 == 1, e2, jnp.where(row8 == 2, r1, jnp.where(row8 == 3, r2, 0.0))))
    rowl = lax.broadcasted_iota(jnp.int32, (LANES, tm), 0)
    wts_ref[...] = jnp.where(rowl < 64, w1, w2).T


def _out_call(da, sg, x2d, g1, sc2, sh2, w_out, lg, lb, wr, br, tiles_per_batch):
    n = x2d.shape[0]
    tm = TOK_TILE
    nt = n // tm
    tpb = tiles_per_batch
    row = lambda i: (i, 0)
    full = lambda i: (0, 0)
    per_b = lambda i: (i // tpb, 0, 0)
    return pl.pallas_call(
        _out_kernel,
        grid=(nt,),
        in_specs=[
            pl.BlockSpec((tm, DA_WIDTH), row),
            pl.BlockSpec((tm, SG_WIDTH), row),
            pl.BlockSpec((tm, D_MODEL), row),
            pl.BlockSpec((1, 1, D_MODEL), per_b),
            pl.BlockSpec((1, 1, D_MODEL), per_b),
            pl.BlockSpec((1, 1, D_MODEL), per_b),
            pl.BlockSpec((D_MODEL, D_MODEL), full),
            pl.BlockSpec((1, D_MODEL), full),
            pl.BlockSpec((1, D_MODEL), full),
            pl.BlockSpec((2 * LANES, D_MODEL), full),
            pl.BlockSpec((LANES, tm), full),
        ],
        out_specs=[
            pl.BlockSpec((tm, D_MODEL), row),
            pl.BlockSpec((tm, D_MODEL), row),
            pl.BlockSpec((8, tm), lambda i: (0, i)),
            pl.BlockSpec((tm, LANES), row),
            pl.BlockSpec((1, N_EXPERTS, LANES), lambda i: (i, 0, 0)),
        ],
        out_shape=[
            jax.ShapeDtypeStruct((n, D_MODEL), F32),
            jax.ShapeDtypeStruct((n, D_MODEL), F32),
            jax.ShapeDtypeStruct((8, n), F32),
            jax.ShapeDtypeStruct((n, LANES), F32),
            jax.ShapeDtypeStruct((nt, N_EXPERTS, LANES), F32),
        ],
        compiler_params=_cparams(("arbitrary",)),
        name="out_proj",
    )(da, sg, x2d, g1, sc2, sh2, w_out, lg, lb, wr, br)


ISSUE_UNROLL = 8


def _dispatch_kernel(zs_ref, ze_ref, dest_ref, t_ref, xb_ref, zero_ref, sem):
    i = pl.program_id(0)
    tm = t_ref.shape[0]

    @pl.when(i == 0)
    def _():
        zero_ref[...] = jnp.zeros_like(zero_ref)

        def per_expert(e, carry):
            def start(r, c):
                pltpu.make_async_copy(_row(zero_ref, 0), _row(xb_ref, r), sem).start()
                return c

            def wait(r, c):
                pltpu.make_async_copy(_row(zero_ref, 0), _row(xb_ref, r), sem).wait()
                return c

            lax.fori_loop(zs_ref[e], ze_ref[e], start, 0)
            lax.fori_loop(zs_ref[e], ze_ref[e], wait, 0)
            return carry

        lax.fori_loop(0, N_EXPERTS, per_expert, 0)

        def blk_copy(j):
            return pltpu.make_async_copy(zero_ref, xb_ref.at[pl.ds(pl.multiple_of(j * MOE_BLOCK, MOE_BLOCK), MOE_BLOCK)], sem)

        def blk_start(j, c):
            blk_copy(j).start()
            return c

        def blk_wait(j, c):
            blk_copy(j).wait()
            return c

        first_unused = ze_ref[N_EXPERTS - 1] // MOE_BLOCK
        n_blocks = xb_ref.shape[0] // MOE_BLOCK
        lax.fori_loop(first_unused, n_blocks, blk_start, 0)
        lax.fori_loop(first_unused, n_blocks, blk_wait, 0)

    def start(n, c):
        pltpu.make_async_copy(_row(t_ref, n), _row(xb_ref, dest_ref[0, 0, n]), sem).start()
        pltpu.make_async_copy(_row(t_ref, n), _row(xb_ref, dest_ref[0, 0, tm + n]), sem).start()
        return c

    lax.fori_loop(0, tm, start, 0, unroll=ISSUE_UNROLL)
    for _ in range(2):
        pltpu.make_async_copy(t_ref, xb_ref.at[pl.ds(0, tm)], sem).wait()


def _dispatch_call(zs, ze, dest3, t, p_rows):
    n = t.shape[0]
    tm = DISPATCH_TILE
    grid_spec = pltpu.PrefetchScalarGridSpec(
        num_scalar_prefetch=2,
        grid=(n // tm,),
        in_specs=[
            pl.BlockSpec((1, 1, 2 * tm), lambda i, zs, ze: (i, 0, 0), memory_space=pltpu.SMEM),
            pl.BlockSpec((tm, D_MODEL), lambda i, zs, ze: (i, 0)),
        ],
        out_specs=pl.BlockSpec(memory_space=pl.ANY),
        scratch_shapes=[pltpu.VMEM((MOE_BLOCK, D_MODEL), F32), pltpu.SemaphoreType.DMA(())],
    )
    return pl.pallas_call(
        _dispatch_kernel,
        grid_spec=grid_spec,
        out_shape=jax.ShapeDtypeStruct((p_rows, D_MODEL), F32),
        compiler_params=_cparams(("arbitrary",)),
        name="dispatch",
    )(zs, ze, dest3, t)


ITEM_SUB = 6
ITEM_ROWS = ITEM_SUB * MOE_BLOCK


def _expert_kernel(ie_ref, ib_ref, ins_ref, iord_ref, inxt_ref, ni_ref, xb_ref, wg_ref, wu_ref, wd_ref, yb_ref,
                   xbuf, ybuf, wf32, wgb, wub, wdb, insem, outsem, wsem):
    i = pl.program_id(0)
    n_steps = pl.num_programs(0)
    ni = ni_ref[0]
    slot = i % 2

    def w_copies(e, s):
        return [pltpu.make_async_copy(w.at[e], wf32.at[s, k], wsem.at[s]) for k, w in enumerate((wg_ref, wu_ref, wd_ref))]

    def rows_of(item, j):
        return pl.ds(pl.multiple_of((ib_ref[item] + j) * MOE_BLOCK, MOE_BLOCK), MOE_BLOCK)

    def in_copy(item, s, j):
        return pltpu.make_async_copy(xb_ref.at[rows_of(item, j)], xbuf.at[s, pl.ds(j * MOE_BLOCK, MOE_BLOCK)], insem.at[s])

    def out_copy(item, s, j):
        return pltpu.make_async_copy(ybuf.at[s, pl.ds(j * MOE_BLOCK, MOE_BLOCK)], yb_ref.at[rows_of(item, j)], outsem.at[s])

    def for_blocks(item, fn):
        for j in range(ITEM_SUB):
            @pl.when(j < ins_ref[item])
            def _():
                fn(j)

    @pl.when(i == 0)
    def _():
        for c in w_copies(ie_ref[0], 0):
            c.start()
        for_blocks(0, lambda j: in_copy(0, 0, j).start())

    @pl.when(i < ni)
    def _():
        @pl.when(i >= 2)
        def _():
            for_blocks(i - 2, lambda j: out_copy(i - 2, slot, j).wait())

        @pl.when(i + 1 < ni)
        def _():
            for_blocks(i + 1, lambda j: in_copy(i + 1, 1 - slot, j).start())

        changed = (i == 0) | (ie_ref[i] != ie_ref[jnp.maximum(i - 1, 0)])

        @pl.when(changed)
        def _():
            ws = iord_ref[i] % 2
            for c in w_copies(ie_ref[i], ws):
                c.wait()

            @pl.when(inxt_ref[i] >= 0)
            def _():
                for c in w_copies(inxt_ref[i], 1 - ws):
                    c.start()

            wgb[...] = wf32[ws, 0].astype(BF16)
            wub[...] = wf32[ws, 1].astype(BF16)
            wdb[...] = wf32[ws, 2].astype(BF16)

        for_blocks(i, lambda j: in_copy(i, slot, j).wait())

        for ns in range(1, ITEM_SUB + 1):
            @pl.when(ins_ref[i] == ns)
            def _():
                rows = pl.ds(0, ns * MOE_BLOCK)
                x = xbuf[slot, rows].astype(BF16)
                g = _dot(x, wgb[...])
                u = _dot(x, wub[...])
                hid = (g * (1.0 / (1.0 + jnp.exp(-g))) * u).astype(BF16)
                ybuf[slot, rows] = _dot(hid, wdb[...])

        for_blocks(i, lambda j: out_copy(i, slot, j).start())

    @pl.when(i == n_steps - 1)
    def _():
        for back in (2, 1):
            k = ni - back

            @pl.when(k >= 0)
            def _():
                for_blocks(k, lambda j: out_copy(k, k % 2, j).wait())

        ybuf[0, pl.ds(0, MOE_BLOCK)] = jnp.zeros((MOE_BLOCK, D_MODEL), F32)
        last = jnp.maximum(ni - 1, 0)
        first_unused = ib_ref[last] + ins_ref[last]
        n_blocks = yb_ref.shape[0] // MOE_BLOCK

        def zero_copy(b):
            dst = yb_ref.at[pl.ds(pl.multiple_of(b * MOE_BLOCK, MOE_BLOCK), MOE_BLOCK)]
            return pltpu.make_async_copy(ybuf.at[0, pl.ds(0, MOE_BLOCK)], dst, outsem.at[0])

        def z_start(b, c):
            zero_copy(b).start()
            return c

        def z_wait(b, c):
            zero_copy(b).wait()
            return c

        lax.fori_loop(first_unused, n_blocks, z_start, 0)
        lax.fori_loop(first_unused, n_blocks, z_wait, 0)


def _expert_call(item_e, item_b, item_ns, item_ord, item_nxt, n_items, xb, wg, wu, wd):
    p_rows = xb.shape[0]
    max_items = item_e.shape[0]
    grid_spec = pltpu.PrefetchScalarGridSpec(
        num_scalar_prefetch=6,
        grid=(max_items,),
        in_specs=[pl.BlockSpec(memory_space=pl.ANY)] * 4,
        out_specs=pl.BlockSpec(memory_space=pl.ANY),
        scratch_shapes=[
            pltpu.VMEM((2, ITEM_ROWS, D_MODEL), F32),
            pltpu.VMEM((2, ITEM_ROWS, D_MODEL), F32),
            pltpu.VMEM((2, 3, D_MODEL, D_MODEL), F32),
            pltpu.VMEM((D_MODEL, D_MODEL), BF16),
            pltpu.VMEM((D_MODEL, D_MODEL), BF16),
            pltpu.VMEM((D_MODEL, D_MODEL), BF16),
            pltpu.SemaphoreType.DMA((2,)),
            pltpu.SemaphoreType.DMA((2,)),
            pltpu.SemaphoreType.DMA((2,)),
        ],
    )
    return pl.pallas_call(
        _expert_kernel,
        grid_spec=grid_spec,
        out_shape=jax.ShapeDtypeStruct((p_rows, D_MODEL), F32),
        compiler_params=_cparams(("arbitrary",)),
        name="experts",
    )(item_e, item_b, item_ns, item_ord, item_nxt, n_items, xb, wg, wu, wd)


def _combine_kernel(dcur_ref, dnxt_ref, yb_ref, x1_ref, wts_ref, g2_ref, lg_ref, lb_ref, o_ref, buf, sem):
    i = pl.program_id(0)
    tm = x1_ref.shape[0]
    slot = i % 2

    def issue(dref, s):
        def start(n, c):
            pltpu.make_async_copy(_row(yb_ref, dref[0, 0, n]), _row(buf.at[s, 0], n), sem.at[s]).start()
            pltpu.make_async_copy(_row(yb_ref, dref[0, 0, tm + n]), _row(buf.at[s, 1], n), sem.at[s]).start()
            return c

        lax.fori_loop(0, tm, start, 0, unroll=ISSUE_UNROLL)

    @pl.when(i == 0)
    def _():
        issue(dcur_ref, 0)

    @pl.when(i + 1 < pl.num_programs(0))
    def _():
        issue(dnxt_ref, 1 - slot)

    for k in range(2):
        pltpu.make_async_copy(yb_ref.at[pl.ds(0, tm)], buf.at[slot, k], sem.at[slot]).wait()
    w = wts_ref[...]
    f = w[:, 0:1] * buf[slot, 0] + w[:, 64:65] * buf[slot, 1]
    o_ref[...] = _layer_norm(DEEPNORM_ALPHA * x1_ref[...] + g2_ref[0] * f, lg_ref[...], lb_ref[...])


def _combine_call(dest3, yb, x1, wts, g2, lg, lb, tiles_per_batch):
    n = x1.shape[0]
    tm = COMBINE_TILE
    tpb = tiles_per_batch
    row = lambda i: (i, 0)
    full = lambda i: (0, 0)
    nt = n // tm
    return pl.pallas_call(
        _combine_kernel,
        grid=(nt,),
        in_specs=[
            pl.BlockSpec((1, 1, 2 * tm), lambda i: (i, 0, 0), memory_space=pltpu.SMEM),
            pl.BlockSpec((1, 1, 2 * tm), lambda i: (jnp.minimum(i + 1, nt - 1), 0, 0), memory_space=pltpu.SMEM),
            pl.BlockSpec(memory_space=pl.ANY),
            pl.BlockSpec((tm, D_MODEL), row),
            pl.BlockSpec((tm, LANES), row),
            pl.BlockSpec((1, 1, D_MODEL), lambda i: (i // tpb, 0, 0)),
            pl.BlockSpec((1, D_MODEL), full),
            pl.BlockSpec((1, D_MODEL), full),
        ],
        out_specs=pl.BlockSpec((tm, D_MODEL), row),
        out_shape=jax.ShapeDtypeStruct((n, D_MODEL), F32),
        scratch_shapes=[pltpu.VMEM((2, 2, tm, D_MODEL), F32), pltpu.SemaphoreType.DMA((2,))],
        compiler_params=_cparams(("arbitrary",)),
        name="combine",
    )(dest3, dest3, yb, x1, wts, g2, lg, lb)


def _head_interleave(w, lo):
    blk = w[:, lo:lo + 2 * QK_COLS].reshape(D_MODEL, 2, DA_HEADS, DA_HEAD_DIM)
    return blk.transpose(0, 2, 1, 3).reshape(D_MODEL, 2 * QK_COLS)


def _cumsum_small(x):
    idx = jnp.arange(x.shape[0], dtype=jnp.int32)
    return jnp.sum(jnp.where(idx[None, :] <= idx[:, None], x[None, :], 0), -1)


def _lookup(table, idx):
    return jnp.sum(jnp.where(idx[:, None] == jnp.arange(table.shape[0], dtype=jnp.int32)[None, :], table[None, :], 0), -1)


def _rope_tables(seq):
    rows_n = seq // GRID_W
    rows = jnp.repeat(jnp.arange(rows_n, dtype=F32), GRID_W)
    cols = jnp.tile(jnp.arange(GRID_W, dtype=F32), rows_n)
    half = DA_HEAD_DIM // 4
    inv = ROPE_THETA ** (-jnp.arange(half, dtype=F32) / half)
    ang_r = rows[:, None] * inv[None, :]
    ang_c = cols[:, None] * inv[None, :]
    cos64 = jnp.concatenate([jnp.cos(ang_r), jnp.cos(ang_r), jnp.cos(ang_c), jnp.cos(ang_c)], -1)
    sin64 = jnp.concatenate([-jnp.sin(ang_r), jnp.sin(ang_r), -jnp.sin(ang_c), jnp.sin(ang_c)], -1)
    return jnp.tile(cos64, (1, 2)), jnp.tile(sin64, (1, 2))


def kernel(x, c, ctx, c_ctx, w_mod, b_mod, w_in, lam_q1, lam_k1, lam_q2, lam_k2, subln_g, sg_ln_g, sg_ln_b, sg_w, sg_b, w_out, ln1_g, ln1_b, router_group_w, router_group_b, router_expert_w, router_expert_b, exp_w_gate, exp_w_up, exp_w_down, ln2_g, ln2_b):
    b, l, d = x.shape
    n = b * l
    tm = TOK_TILE
    nt = n // tm

    cc = jnp.zeros((b + 8, d), F32).at[:b].set(c).at[b].set(c_ctx)
    mod = _mod_call(cc, w_mod[0], b_mod[0][None, :])
    sh1, sc1, g1, sh2, sc2, g2 = [mod[:b, j * d:(j + 1) * d].reshape(b, 1, d) for j in range(6)]
    csh1 = mod[b:b + 1, 0:d]
    csc1 = mod[b:b + 1, d:2 * d]

    wi = w_in[0]
    w_all = jnp.concatenate([_head_interleave(wi, 0), _head_interleave(wi, KV_LO), wi[:, 2 * KV_LO:]], -1).astype(BF16)
    kc, vc = _ctx_call(ctx, csc1, csh1, w_all[:, KV_LO:KV_HI])

    cos_t, sin_t = _rope_tables(l)
    sbias = jnp.repeat(sg_b[0].T, LANES, axis=1)
    q, k, v, sg = _in_call(x, sc1, sh1, w_all, cos_t, sin_t, sg_ln_g[0][None, :], sg_ln_b[0][None, :],
                           sg_w[0].astype(BF16), sbias)

    lamv = jnp.stack([lam_q1[0], lam_k1[0], lam_q2[0], lam_k2[0]]).astype(F32)
    da = _attn_call(lamv, q, k, kc, v, vc, subln_g[0][None, :])

    e_lo = MOE_EXPERTS_PER_GROUP
    wr = jnp.zeros((LANES, d), F32).at[:MOE_GROUPS].set(router_group_w[0].T).at[e_lo:e_lo + N_EXPERTS].set(router_expert_w[0].T)
    br = jnp.zeros((LANES,), F32).at[:MOE_GROUPS].set(router_group_b[0]).at[e_lo:e_lo + N_EXPERTS].set(router_expert_b[0])
    br = jnp.broadcast_to(br[:, None], (LANES, tm))
    wr_hi = wr.astype(BF16)
    wr_split = jnp.concatenate([wr_hi, (wr - wr_hi.astype(F32)).astype(BF16)], 0)
    x1, t, route, wts, tcnt = _out_call(da.reshape(n, DA_WIDTH), sg.reshape(n, SG_WIDTH), x.reshape(n, d), g1, sc2, sh2,
                                        w_out[0].astype(BF16), ln1_g[0][None, :], ln1_b[0][None, :], wr_split, br, l // tm)

    cnt_te = tcnt[:, :, 0].astype(jnp.int32)
    counts = jnp.sum(cnt_te, 0)
    padded = (counts + MOE_BLOCK - 1) // MOE_BLOCK * MOE_BLOCK
    pad_end = _cumsum_small(padded)
    pad_start = pad_end - padded
    tix = jnp.arange(nt, dtype=jnp.int32)
    tile_prefix = jnp.sum(jnp.where(tix[None, :, None] < tix[:, None, None], cnt_te[None], 0), 1)
    base = pad_start[None, :] + tile_prefix
    ridx = route[:4].astype(jnp.int32).reshape(4, nt, tm)
    ex = jnp.arange(N_EXPERTS, dtype=jnp.int32)

    def slot_dest(eid, rank):
        return jnp.sum(jnp.where(eid[None] == ex[:, None, None], base.T[:, :, None], 0), 0) + rank

    dest_k = (slot_dest(ridx[0], ridx[2]), slot_dest(ridx[1], ridx[3]))

    def tile_dest(tile):
        return jnp.concatenate([dk.reshape(-1, tile) for dk in dest_k], -1).reshape(-1, 1, 2 * tile)
    n_blocks = (n * 2) // MOE_BLOCK + N_EXPERTS
    p_rows = n_blocks * MOE_BLOCK
    nb_e = padded // MOE_BLOCK
    items_e = (nb_e + ITEM_SUB - 1) // ITEM_SUB
    item_end = _cumsum_small(items_e)
    max_items = (n_blocks + (ITEM_SUB - 1) * N_EXPERTS) // ITEM_SUB
    it = jnp.arange(max_items, dtype=jnp.int32)
    item_e = jnp.minimum(jnp.sum((it[:, None] >= item_end[None, :]).astype(jnp.int32), -1), N_EXPERTS - 1)
    item_j = it - _lookup(item_end - items_e, item_e)
    item_b = _lookup(pad_start // MOE_BLOCK, item_e) + ITEM_SUB * item_j
    item_ns = jnp.clip(_lookup(nb_e, item_e) - ITEM_SUB * item_j, 0, ITEM_SUB)
    nonempty = (items_e > 0).astype(jnp.int32)
    ord_e = _cumsum_small(nonempty) - 1
    later = (ex[None, :] > ex[:, None]) & (nonempty[None, :] > 0)
    nxt_e = jnp.min(jnp.where(later, ex[None, :], N_EXPERTS), -1)
    nxt_e = jnp.where(nxt_e == N_EXPERTS, -1, nxt_e)
    item_ord = _lookup(ord_e, item_e)
    item_nxt = _lookup(nxt_e, item_e)

    xb = _dispatch_call((pad_start + counts).astype(jnp.int32), pad_end.astype(jnp.int32), tile_dest(DISPATCH_TILE),
                        t, p_rows)
    yb = _expert_call(item_e, item_b.astype(jnp.int32), item_ns.astype(jnp.int32), item_ord.astype(jnp.int32),
                      item_nxt.astype(jnp.int32), item_end[-1:].astype(jnp.int32),
                      xb, exp_w_gate[0], exp_w_up[0], exp_w_down[0])
    out = _combine_call(tile_dest(COMBINE_TILE), yb, x1, wts, g2, ln2_g[0][None, :], ln2_b[0][None, :],
                        l // COMBINE_TILE)
    return out.reshape(b, l, d)
```

```python
import math

import jax
import jax.numpy as jnp
from jax import lax
from jax.experimental import pallas as pl
from jax.experimental.pallas import tpu as pltpu

F32 = jnp.float32
BF16 = jnp.bfloat16

D_MODEL = 1024
GRID_W = 64
DA_HEAD_DIM = 64
DA_V_DIM = 128
DA_WIDTH = 512
DA_HEADS = 4
QK_COLS = 256
SG_CHUNK = 128
SG_WIDTH = 512
SG_GROUPS = 4
KV_LO = 512
KV_HI = 1536
IN_COLS = 2560
ROPE_THETA = 10000.0
MOE_GROUPS = 4
MOE_EXPERTS_PER_GROUP = 8
N_EXPERTS = 32
MOE_BLOCK = 128
EPS = 1e-5
DEPTH = 1
DEEPNORM_ALPHA = (2.0 * DEPTH) ** 0.25
LAM_INIT = 0.8 - 0.6 * math.exp(-0.3 * 0)
Q_SCALE = DA_HEAD_DIM ** -0.5
LOG2E = math.log2(math.e)
SQRT_HALF = math.sqrt(0.5)

LANES = 128
TOK_TILE = 1024
DISPATCH_TILE = 4096
COMBINE_TILE = 512
IN_SUB = 512
ATTN_TQ = 2048
ATTN_SUB = 256
VMEM_LIMIT = 56 * 1024 * 1024


def _cparams(sem):
    return pltpu.CompilerParams(dimension_semantics=sem, vmem_limit_bytes=VMEM_LIMIT)


def _dot(a, b):
    return jnp.dot(a, b, preferred_element_type=F32)


def _dot_hi(a, b):
    return jnp.dot(a, b, preferred_element_type=F32, precision=lax.Precision.HIGHEST)


def _layer_norm(y, g, b):
    mu = jnp.mean(y, -1, keepdims=True)
    yc = y - mu
    var = jnp.mean(yc * yc, -1, keepdims=True)
    return yc * lax.rsqrt(var + EPS) * g + b


def _row(ref, n):
    return ref.at[pl.ds(n, 1)]


def _mod_kernel(c_ref, w_ref, b_ref, o_ref):
    c = c_ref[...]
    s = c * (1.0 / (1.0 + jnp.exp(-c)))
    o_ref[...] = _dot_hi(s, w_ref[...]) + b_ref[...]


def _mod_call(cc, w_mod, b_mod):
    rows = cc.shape[0]
    n_out = w_mod.shape[1]
    bn = 1024
    return pl.pallas_call(
        _mod_kernel,
        grid=(n_out // bn,),
        in_specs=[
            pl.BlockSpec((rows, D_MODEL), lambda j: (0, 0)),
            pl.BlockSpec((D_MODEL, bn), lambda j: (0, j)),
            pl.BlockSpec((1, bn), lambda j: (0, j)),
        ],
        out_specs=pl.BlockSpec((rows, bn), lambda j: (0, j)),
        out_shape=jax.ShapeDtypeStruct((rows, n_out), F32),
        compiler_params=_cparams(("arbitrary",)),
        name="mod",
    )(cc, w_mod, b_mod)


CTX_BATCH = 4


def _ctx_kernel(c_ref, sc_ref, sh_ref, w_ref, kc_ref, vc_ref):
    nb, cl, _ = c_ref.shape
    c = c_ref[...].reshape(nb * cl, D_MODEL)
    h = (c * (1.0 + sc_ref[...]) + sh_ref[...]).astype(BF16)
    p = _dot(h, w_ref[...])
    kc_ref[...] = p[:, :2 * QK_COLS].astype(BF16).reshape(nb, cl, 2 * QK_COLS)
    vc_ref[...] = p[:, 2 * QK_COLS:].astype(BF16).reshape(nb, cl, DA_WIDTH)


def _ctx_call(ctx, csc1, csh1, w_kv):
    b, cl, _ = ctx.shape
    nb = CTX_BATCH
    return pl.pallas_call(
        _ctx_kernel,
        grid=(b // nb,),
        in_specs=[
            pl.BlockSpec((nb, cl, D_MODEL), lambda i: (i, 0, 0)),
            pl.BlockSpec((1, D_MODEL), lambda i: (0, 0)),
            pl.BlockSpec((1, D_MODEL), lambda i: (0, 0)),
            pl.BlockSpec((D_MODEL, 2 * QK_COLS + DA_WIDTH), lambda i: (0, 0)),
        ],
        out_specs=[
            pl.BlockSpec((nb, cl, 2 * QK_COLS), lambda i: (i, 0, 0)),
            pl.BlockSpec((nb, cl, DA_WIDTH), lambda i: (i, 0, 0)),
        ],
        out_shape=[
            jax.ShapeDtypeStruct((b, cl, 2 * QK_COLS), BF16),
            jax.ShapeDtypeStruct((b, cl, DA_WIDTH), BF16),
        ],
        compiler_params=_cparams(("arbitrary",)),
        name="ctx_kv",
    )(ctx, csc1, csh1, w_kv)


def _in_kernel(x_ref, sc_ref, sh_ref, w_ref, cos_ref, sin_ref, lng_ref, lnb_ref,
               sw_ref, sb_ref, q_ref, k_ref, v_ref, sg_ref):
    tm = x_ref.shape[1]
    sub = IN_SUB
    lane = lax.broadcasted_iota(jnp.int32, (sub, LANES), 1)
    first = (lane % 32) < 16

    for r0 in range(0, tm, sub):
        rr = pl.ds(r0, sub)
        h = (x_ref[0, rr, :] * (1.0 + sc_ref[0]) + sh_ref[0]).astype(BF16)
        p = _dot(h, w_ref[...])

        cos = cos_ref[rr, :]
        sin = sin_ref[rr, :]

        def rope(t):
            partner = jnp.where(first, pltpu.roll(t, LANES - 16, 1), pltpu.roll(t, 16, 1))
            return t * cos + partner * sin

        for c in range(4):
            cs = slice(c * LANES, (c + 1) * LANES)
            q_ref[0, rr, cs] = rope(p[:, cs] * (Q_SCALE * LOG2E)).astype(BF16)
            k_ref[0, rr, cs] = rope(p[:, KV_LO + c * LANES:KV_LO + (c + 1) * LANES]).astype(BF16)
        v_ref[0, rr, :] = p[:, 2 * KV_LO:KV_HI].astype(BF16)

        z = p[:, KV_HI:]
        gz = 0.5 * z * (1.0 + lax.erf(z * SQRT_HALF))
        u = gz[:, :SG_WIDTH]
        vn = _layer_norm(gz[:, SG_WIDTH:], lng_ref[...], lnb_ref[...]).astype(BF16)
        for c in range(sub // SG_CHUNK):
            rs = slice(c * SG_CHUNK, (c + 1) * SG_CHUNK)
            for g in range(SG_GROUPS):
                cs = slice(g * LANES, (g + 1) * LANES)
                s = _dot(sw_ref[g], vn[rs, cs]) + sb_ref[:, cs]
                sg_ref[0, pl.ds(r0 + c * SG_CHUNK, SG_CHUNK), cs] = (u[rs, cs] * s).astype(BF16)


def _in_call(x, sc1, sh1, w_in, cos_t, sin_t, lng, lnb, sw, sbias):
    b, l, _ = x.shape
    tm = TOK_TILE
    full = lambda bi, i: (0, 0)
    return pl.pallas_call(
        _in_kernel,
        grid=(b, l // tm),
        in_specs=[
            pl.BlockSpec((1, tm, D_MODEL), lambda bi, i: (bi, i, 0)),
            pl.BlockSpec((1, 1, D_MODEL), lambda bi, i: (bi, 0, 0)),
            pl.BlockSpec((1, 1, D_MODEL), lambda bi, i: (bi, 0, 0)),
            pl.BlockSpec((D_MODEL, IN_COLS), full),
            pl.BlockSpec((tm, LANES), lambda bi, i: (i, 0)),
            pl.BlockSpec((tm, LANES), lambda bi, i: (i, 0)),
            pl.BlockSpec((1, SG_WIDTH), full),
            pl.BlockSpec((1, SG_WIDTH), full),
            pl.BlockSpec((SG_GROUPS, SG_CHUNK, SG_CHUNK), lambda bi, i: (0, 0, 0)),
            pl.BlockSpec((SG_CHUNK, SG_WIDTH), full),
        ],
        out_specs=[pl.BlockSpec((1, tm, 512), lambda bi, i: (bi, i, 0))] * 4,
        out_shape=[jax.ShapeDtypeStruct((b, l, 512), BF16)] * 4,
        compiler_params=_cparams(("arbitrary", "arbitrary")),
        name="in_proj",
    )(x, sc1, sh1, w_in, cos_t, sin_t, lng, lnb, sw, sbias)


def _attn_kernel(lam_ref, q_ref, k_ref, kc_ref, v_ref, vc_ref, g_ref, o_ref, kall, vall):
    lv = lam_ref[...]
    lam = (jnp.exp(jnp.sum(lv[0:1] * lv[1:2], -1, keepdims=True))
           - jnp.exp(jnp.sum(lv[2:3] * lv[3:4], -1, keepdims=True)) + LAM_INIT)
    l = k_ref.shape[1]

    @pl.when(pl.program_id(2) == 0)
    def _():
        kall[:l, :] = k_ref[0]
        kall[l:, :] = kc_ref[0]
        vall[:l, :DA_V_DIM] = v_ref[0]
        vall[l:, :DA_V_DIM] = vc_ref[0]
        vall[:, DA_V_DIM:] = jnp.ones((vall.shape[0], DA_V_DIM), BF16)

    nt = (((1,), (1,)), ((), ()))

    def branch(qm):
        s = lax.dot_general(qm, kall[...], nt, preferred_element_type=F32)
        m = jnp.max(s, -1, keepdims=True)
        e = jnp.exp2((s - m).astype(BF16))
        oe = _dot(e, vall[...])
        return oe[:, :DA_V_DIM] / oe[:, DA_V_DIM:DA_V_DIM + 1]

    for r in range(q_ref.shape[1] // ATTN_SUB):
        rs = pl.ds(r * ATTN_SUB, ATTN_SUB)
        q = q_ref[0, rs, :]
        lane = lax.broadcasted_iota(jnp.int32, q.shape, 1)
        zero = jnp.zeros_like(q)
        o = branch(jnp.where(lane < DA_HEAD_DIM, q, zero)) - lam * branch(jnp.where(lane >= DA_HEAD_DIM, q, zero))
        of = o * lax.rsqrt(jnp.mean(o * o, -1, keepdims=True) + EPS) * g_ref[...]
        o_ref[0, rs, :] = (of * (1.0 - LAM_INIT)).astype(BF16)


def _attn_call(lamv, q, k, kc, v, vc, subln_g):
    b, l, _ = q.shape
    cl = kc.shape[1]
    tq = ATTN_TQ
    return pl.pallas_call(
        _attn_kernel,
        grid=(b, DA_HEADS, l // tq),
        in_specs=[
            pl.BlockSpec((4, DA_HEAD_DIM), lambda bi, h, i: (0, 0)),
            pl.BlockSpec((1, tq, LANES), lambda bi, h, i: (bi, i, h)),
            pl.BlockSpec((1, l, LANES), lambda bi, h, i: (bi, 0, h)),
            pl.BlockSpec((1, cl, LANES), lambda bi, h, i: (bi, 0, h)),
            pl.BlockSpec((1, l, LANES), lambda bi, h, i: (bi, 0, h)),
            pl.BlockSpec((1, cl, LANES), lambda bi, h, i: (bi, 0, h)),
            pl.BlockSpec((1, DA_V_DIM), lambda bi, h, i: (0, 0)),
        ],
        out_specs=pl.BlockSpec((1, tq, LANES), lambda bi, h, i: (bi, i, h)),
        out_shape=jax.ShapeDtypeStruct((b, l, DA_WIDTH), BF16),
        scratch_shapes=[pltpu.VMEM((l + cl, LANES), BF16), pltpu.VMEM((l + cl, 2 * DA_V_DIM), BF16)],
        compiler_params=_cparams(("arbitrary", "arbitrary", "arbitrary")),
        name="attn",
    )(lamv, q, k, kc, v, vc, subln_g)


def _out_kernel(da_ref, sg_ref, x_ref, g1_ref, sc_ref, sh_ref, w_ref, lg_ref, lb_ref,
                wr_ref, br_ref, x1_ref, t_ref, route_ref, wts_ref, cnt_ref):
    tm = x_ref.shape[0]
    y = _dot(da_ref[...], w_ref[:DA_WIDTH, :]) + _dot(sg_ref[...], w_ref[DA_WIDTH:, :])
    x1 = _layer_norm(DEEPNORM_ALPHA * x_ref[...] + g1_ref[0] * y, lg_ref[...], lb_ref[...])
    x1_ref[...] = x1
    t = x1 * (1.0 + sc_ref[0]) + sh_ref[0]
    t_ref[...] = t

    t_hi = t.astype(BF16)
    t_lo = (t - t_hi.astype(F32)).astype(BF16)
    nt_dims = (((1,), (1,)), ((), ()))
    hw = lax.dot_general(wr_ref[...], t_hi, nt_dims, preferred_element_type=F32)
    lt = (hw[:LANES] + hw[LANES:] + lax.dot_general(wr_ref[:LANES, :], t_lo, nt_dims, preferred_element_type=F32)
          + br_ref[...])
    epg = MOE_EXPERTS_PER_GROUP
    row8 = lax.broadcasted_iota(jnp.int32, (epg, tm), 0).astype(F32)
    ninf = jnp.float32(-jnp.inf)
    big = jnp.float32(epg)
    gmask = row8 < MOE_GROUPS
    gl = jnp.where(gmask, lt[0:epg], ninf)
    gmax = jnp.max(gl, 0, keepdims=True)
    gsel = jnp.min(jnp.where(gl == gmax, row8, big), 0, keepdims=True)
    gsum = jnp.sum(jnp.where(gmask, jnp.exp(gl - gmax), 0.0), 0, keepdims=True)
    gw = 1.0 / gsum
    el = lt[epg:2 * epg]
    for g in range(1, MOE_GROUPS):
        el = jnp.where(gsel == g, lt[epg * (g + 1):epg * (g + 2)], el)
    v1 = jnp.max(el, 0, keepdims=True)
    i1 = jnp.min(jnp.where(el == v1, row8, big), 0, keepdims=True)
    el2 = jnp.where(row8 == i1, ninf, el)
    v2 = jnp.max(el2, 0, keepdims=True)
    i2 = jnp.min(jnp.where(el2 == v2, row8, big), 0, keepdims=True)
    e = jnp.exp(v2 - v1)
    w1 = gw / (1.0 + e)
    w2 = gw * e / (1.0 + e)
    e1 = gsel * epg + i1
    e2 = gsel * epg + i2

    rowe = lax.broadcasted_iota(jnp.int32, (N_EXPERTS, tm), 0).astype(F32)
    oh1 = rowe == e1
    oh2 = rowe == e2
    oh = jnp.where(oh1 | oh2, 1.0, 0.0)
    r_i = lax.broadcasted_iota(jnp.int32, (tm, tm), 0)
    c_i = lax.broadcasted_iota(jnp.int32, (tm, tm), 1)
    upper = jnp.where(r_i < c_i, 1.0, 0.0).astype(BF16)
    pref = _dot(oh.astype(BF16), upper)
    r1 = jnp.sum(jnp.where(oh1, pref, 0.0), 0, keepdims=True)
    r2 = jnp.sum(jnp.where(oh2, pref, 0.0), 0, keepdims=True)
    cnt_ref[0] = jnp.broadcast_to(jnp.sum(oh, 1, keepdims=True), (N_EXPERTS, LANES))

    route_ref[...] = jnp.where(row8 == 0, e1, jnp.where(row8 == 1, e2, jnp.where(row8 == 2, r1, jnp.where(row8 == 3, r2, 0.0))))
    rowl = lax.broadcasted_iota(jnp.int32, (LANES, tm), 0)
    wts_ref[...] = jnp.where(rowl < 64, w1, w2).T


def _out_call(da, sg, x2d, g1, sc2, sh2, w_out, lg, lb, wr, br, tiles_per_batch):
    n = x2d.shape[0]
    tm = TOK_TILE
    nt = n // tm
    tpb = tiles_per_batch
    row = lambda i: (i, 0)
    full = lambda i: (0, 0)
    per_b = lambda i: (i // tpb, 0, 0)
    return pl.pallas_call(
        _out_kernel,
        grid=(nt,),
        in_specs=[
            pl.BlockSpec((tm, DA_WIDTH), row),
            pl.BlockSpec((tm, SG_WIDTH), row),
            pl.BlockSpec((tm, D_MODEL), row),
            pl.BlockSpec((1, 1, D_MODEL), per_b),
            pl.BlockSpec((1, 1, D_MODEL), per_b),
            pl.BlockSpec((1, 1, D_MODEL), per_b),
            pl.BlockSpec((D_MODEL, D_MODEL), full),
            pl.BlockSpec((1, D_MODEL), full),
            pl.BlockSpec((1, D_MODEL), full),
            pl.BlockSpec((2 * LANES, D_MODEL), full),
            pl.BlockSpec((LANES, tm), full),
        ],
        out_specs=[
            pl.BlockSpec((tm, D_MODEL), row),
            pl.BlockSpec((tm, D_MODEL), row),
            pl.BlockSpec((8, tm), lambda i: (0, i)),
            pl.BlockSpec((tm, LANES), row),
            pl.BlockSpec((1, N_EXPERTS, LANES), lambda i: (i, 0, 0)),
        ],
        out_shape=[
            jax.ShapeDtypeStruct((n, D_MODEL), F32),
            jax.ShapeDtypeStruct((n, D_MODEL), F32),
            jax.ShapeDtypeStruct((8, n), F32),
            jax.ShapeDtypeStruct((n, LANES), F32),
            jax.ShapeDtypeStruct((nt, N_EXPERTS, LANES), F32),
        ],
        compiler_params=_cparams(("arbitrary",)),
        name="out_proj",
    )(da, sg, x2d, g1, sc2, sh2, w_out, lg, lb, wr, br)


ISSUE_UNROLL = 8


def _dispatch_kernel(zs_ref, ze_ref, dest_ref, t_ref, xb_ref, zero_ref, sem):
    i = pl.program_id(0)
    tm = t_ref.shape[0]

    @pl.when(i == 0)
    def _():
        zero_ref[...] = jnp.zeros_like(zero_ref)

        def per_expert(e, carry):
            def start(r, c):
                pltpu.make_async_copy(_row(zero_ref, 0), _row(xb_ref, r), sem).start()
                return c

            def wait(r, c):
                pltpu.make_async_copy(_row(zero_ref, 0), _row(xb_ref, r), sem).wait()
                return c

            lax.fori_loop(zs_ref[e], ze_ref[e], start, 0)
            lax.fori_loop(zs_ref[e], ze_ref[e], wait, 0)
            return carry

        lax.fori_loop(0, N_EXPERTS, per_expert, 0)

        def blk_copy(j):
            return pltpu.make_async_copy(zero_ref, xb_ref.at[pl.ds(pl.multiple_of(j * MOE_BLOCK, MOE_BLOCK), MOE_BLOCK)], sem)

        def blk_start(j, c):
            blk_copy(j).start()
            return c

        def blk_wait(j, c):
            blk_copy(j).wait()
            return c

        first_unused = ze_ref[N_EXPERTS - 1] // MOE_BLOCK
        n_blocks = xb_ref.shape[0] // MOE_BLOCK
        lax.fori_loop(first_unused, n_blocks, blk_start, 0)
        lax.fori_loop(first_unused, n_blocks, blk_wait, 0)

    def start(n, c):
        pltpu.make_async_copy(_row(t_ref, n), _row(xb_ref, dest_ref[0, 0, n]), sem).start()
        pltpu.make_async_copy(_row(t_ref, n), _row(xb_ref, dest_ref[0, 0, tm + n]), sem).start()
        return c

    lax.fori_loop(0, tm, start, 0, unroll=ISSUE_UNROLL)
    for _ in range(2):
        pltpu.make_async_copy(t_ref, xb_ref.at[pl.ds(0, tm)], sem).wait()


def _dispatch_call(zs, ze, dest3, t, p_rows):
    n = t.shape[0]
    tm = DISPATCH_TILE
    grid_spec = pltpu.PrefetchScalarGridSpec(
        num_scalar_prefetch=2,
        grid=(n // tm,),
        in_specs=[
            pl.BlockSpec((1, 1, 2 * tm), lambda i, zs, ze: (i, 0, 0), memory_space=pltpu.SMEM),
            pl.BlockSpec((tm, D_MODEL), lambda i, zs, ze: (i, 0)),
        ],
        out_specs=pl.BlockSpec(memory_space=pl.ANY),
        scratch_shapes=[pltpu.VMEM((MOE_BLOCK, D_MODEL), F32), pltpu.SemaphoreType.DMA(())],
    )
    return pl.pallas_call(
        _dispatch_kernel,
        grid_spec=grid_spec,
        out_shape=jax.ShapeDtypeStruct((p_rows, D_MODEL), F32),
        compiler_params=_cparams(("arbitrary",)),
        name="dispatch",
    )(zs, ze, dest3, t)


ITEM_SUB = 6
ITEM_ROWS = ITEM_SUB * MOE_BLOCK


def _expert_kernel(ie_ref, ib_ref, ins_ref, iord_ref, inxt_ref, ni_ref, xb_ref, wg_ref, wu_ref, wd_ref, yb_ref,
                   xbuf, ybuf, wf32, wgb, wub, wdb, insem, outsem, wsem):
    i = pl.program_id(0)
    n_steps = pl.num_programs(0)
    ni = ni_ref[0]
    slot = i % 2

    def w_copies(e, s):
        return [pltpu.make_async_copy(w.at[e], wf32.at[s, k], wsem.at[s]) for k, w in enumerate((wg_ref, wu_ref, wd_ref))]

    def rows_of(item, j):
        return pl.ds(pl.multiple_of((ib_ref[item] + j) * MOE_BLOCK, MOE_BLOCK), MOE_BLOCK)

    def in_copy(item, s, j):
        return pltpu.make_async_copy(xb_ref.at[rows_of(item, j)], xbuf.at[s, pl.ds(j * MOE_BLOCK, MOE_BLOCK)], insem.at[s])

    def out_copy(item, s, j):
        return pltpu.make_async_copy(ybuf.at[s, pl.ds(j * MOE_BLOCK, MOE_BLOCK)], yb_ref.at[rows_of(item, j)], outsem.at[s])

    def for_blocks(item, fn):
        for j in range(ITEM_SUB):
            @pl.when(j < ins_ref[item])
            def _():
                fn(j)

    @pl.when(i == 0)
    def _():
        for c in w_copies(ie_ref[0], 0):
            c.start()
        for_blocks(0, lambda j: in_copy(0, 0, j).start())

    @pl.when(i < ni)
    def _():
        @pl.when(i >= 2)
        def _():
            for_blocks(i - 2, lambda j: out_copy(i - 2, slot, j).wait())

        @pl.when(i + 1 < ni)
        def _():
            for_blocks(i + 1, lambda j: in_copy(i + 1, 1 - slot, j).start())

        changed = (i == 0) | (ie_ref[i] != ie_ref[jnp.maximum(i - 1, 0)])

        @pl.when(changed)
        def _():
            ws = iord_ref[i] % 2
            for c in w_copies(ie_ref[i], ws):
                c.wait()

            @pl.when(inxt_ref[i] >= 0)
            def _():
                for c in w_copies(inxt_ref[i], 1 - ws):
                    c.start()

            wgb[...] = wf32[ws, 0].astype(BF16)
            wub[...] = wf32[ws, 1].astype(BF16)
            wdb[...] = wf32[ws, 2].astype(BF16)

        for_blocks(i, lambda j: in_copy(i, slot, j).wait())

        for ns in range(1, ITEM_SUB + 1):
            @pl.when(ins_ref[i] == ns)
            def _():
                rows = pl.ds(0, ns * MOE_BLOCK)
                x = xbuf[slot, rows].astype(BF16)
                g = _dot(x, wgb[...])
                u = _dot(x, wub[...])
                hid = (g * (1.0 / (1.0 + jnp.exp(-g))) * u).astype(BF16)
                ybuf[slot, rows] = _dot(hid, wdb[...])

        for_blocks(i, lambda j: out_copy(i, slot, j).start())

    @pl.when(i == n_steps - 1)
    def _():
        for back in (2, 1):
            k = ni - back

            @pl.when(k >= 0)
            def _():
                for_blocks(k, lambda j: out_copy(k, k % 2, j).wait())

        ybuf[0, pl.ds(0, MOE_BLOCK)] = jnp.zeros((MOE_BLOCK, D_MODEL), F32)
        last = jnp.maximum(ni - 1, 0)
        first_unused = ib_ref[last] + ins_ref[last]
        n_blocks = yb_ref.shape[0] // MOE_BLOCK

        def zero_copy(b):
            dst = yb_ref.at[pl.ds(pl.multiple_of(b * MOE_BLOCK, MOE_BLOCK), MOE_BLOCK)]
            return pltpu.make_async_copy(ybuf.at[0, pl.ds(0, MOE_BLOCK)], dst, outsem.at[0])

        def z_start(b, c):
            zero_copy(b).start()
            return c

        def z_wait(b, c):
            zero_copy(b).wait()
            return c

        lax.fori_loop(first_unused, n_blocks, z_start, 0)
        lax.fori_loop(first_unused, n_blocks, z_wait, 0)


def _expert_call(item_e, item_b, item_ns, item_ord, item_nxt, n_items, xb, wg, wu, wd):
    p_rows = xb.shape[0]
    max_items = item_e.shape[0]
    grid_spec = pltpu.PrefetchScalarGridSpec(
        num_scalar_prefetch=6,
        grid=(max_items,),
        in_specs=[pl.BlockSpec(memory_space=pl.ANY)] * 4,
        out_specs=pl.BlockSpec(memory_space=pl.ANY),
        scratch_shapes=[
            pltpu.VMEM((2, ITEM_ROWS, D_MODEL), F32),
            pltpu.VMEM((2, ITEM_ROWS, D_MODEL), F32),
            pltpu.VMEM((2, 3, D_MODEL, D_MODEL), F32),
            pltpu.VMEM((D_MODEL, D_MODEL), BF16),
            pltpu.VMEM((D_MODEL, D_MODEL), BF16),
            pltpu.VMEM((D_MODEL, D_MODEL), BF16),
            pltpu.SemaphoreType.DMA((2,)),
            pltpu.SemaphoreType.DMA((2,)),
            pltpu.SemaphoreType.DMA((2,)),
        ],
    )
    return pl.pallas_call(
        _expert_kernel,
        grid_spec=grid_spec,
        out_shape=jax.ShapeDtypeStruct((p_rows, D_MODEL), F32),
        compiler_params=_cparams(("arbitrary",)),
        name="experts",
    )(item_e, item_b, item_ns, item_ord, item_nxt, n_items, xb, wg, wu, wd)


def _combine_kernel(dcur_ref, dnxt_ref, yb_ref, x1_ref, wts_ref, g2_ref, lg_ref, lb_ref, o_ref, buf, sem):
    i = pl.program_id(0)
    tm = x1_ref.shape[0]
    slot = i % 2

    def issue(dref, s):
        def start(n, c):
            pltpu.make_async_copy(_row(yb_ref, dref[0, 0, n]), _row(buf.at[s, 0], n), sem.at[s]).start()
            pltpu.make_async_copy(_row(yb_ref, dref[0, 0, tm + n]), _row(buf.at[s, 1], n), sem.at[s]).start()
            return c

        lax.fori_loop(0, tm, start, 0, unroll=ISSUE_UNROLL)

    @pl.when(i == 0)
    def _():
        issue(dcur_ref, 0)

    @pl.when(i + 1 < pl.num_programs(0))
    def _():
        issue(dnxt_ref, 1 - slot)

    for k in range(2):
        pltpu.make_async_copy(yb_ref.at[pl.ds(0, tm)], buf.at[slot, k], sem.at[slot]).wait()
    w = wts_ref[...]
    f = w[:, 0:1] * buf[slot, 0] + w[:, 64:65] * buf[slot, 1]
    o_ref[...] = _layer_norm(DEEPNORM_ALPHA * x1_ref[...] + g2_ref[0] * f, lg_ref[...], lb_ref[...])


def _combine_call(dest3, yb, x1, wts, g2, lg, lb, tiles_per_batch):
    n = x1.shape[0]
    tm = COMBINE_TILE
    tpb = tiles_per_batch
    row = lambda i: (i, 0)
    full = lambda i: (0, 0)
    nt = n // tm
    return pl.pallas_call(
        _combine_kernel,
        grid=(nt,),
        in_specs=[
            pl.BlockSpec((1, 1, 2 * tm), lambda i: (i, 0, 0), memory_space=pltpu.SMEM),
            pl.BlockSpec((1, 1, 2 * tm), lambda i: (jnp.minimum(i + 1, nt - 1), 0, 0), memory_space=pltpu.SMEM),
            pl.BlockSpec(memory_space=pl.ANY),
            pl.BlockSpec((tm, D_MODEL), row),
            pl.BlockSpec((tm, LANES), row),
            pl.BlockSpec((1, 1, D_MODEL), lambda i: (i // tpb, 0, 0)),
            pl.BlockSpec((1, D_MODEL), full),
            pl.BlockSpec((1, D_MODEL), full),
        ],
        out_specs=pl.BlockSpec((tm, D_MODEL), row),
        out_shape=jax.ShapeDtypeStruct((n, D_MODEL), F32),
        scratch_shapes=[pltpu.VMEM((2, 2, tm, D_MODEL), F32), pltpu.SemaphoreType.DMA((2,))],
        compiler_params=_cparams(("arbitrary",)),
        name="combine",
    )(dest3, dest3, yb, x1, wts, g2, lg, lb)


def _head_interleave(w, lo):
    blk = w[:, lo:lo + 2 * QK_COLS].reshape(D_MODEL, 2, DA_HEADS, DA_HEAD_DIM)
    return blk.transpose(0, 2, 1, 3).reshape(D_MODEL, 2 * QK_COLS)


def _cumsum_small(x):
    idx = jnp.arange(x.shape[0], dtype=jnp.int32)
    return jnp.sum(jnp.where(idx[None, :] <= idx[:, None], x[None, :], 0), -1)


def _lookup(table, idx):
    return jnp.sum(jnp.where(idx[:, None] == jnp.arange(table.shape[0], dtype=jnp.int32)[None, :], table[None, :], 0), -1)


def _rope_tables(seq):
    rows_n = seq // GRID_W
    rows = jnp.repeat(jnp.arange(rows_n, dtype=F32), GRID_W)
    cols = jnp.tile(jnp.arange(GRID_W, dtype=F32), rows_n)
    half = DA_HEAD_DIM // 4
    inv = ROPE_THETA ** (-jnp.arange(half, dtype=F32) / half)
    ang_r = rows[:, None] * inv[None, :]
    ang_c = cols[:, None] * inv[None, :]
    cos64 = jnp.concatenate([jnp.cos(ang_r), jnp.cos(ang_r), jnp.cos(ang_c), jnp.cos(ang_c)], -1)
    sin64 = jnp.concatenate([-jnp.sin(ang_r), jnp.sin(ang_r), -jnp.sin(ang_c), jnp.sin(ang_c)], -1)
    return jnp.tile(cos64, (1, 2)), jnp.tile(sin64, (1, 2))


def kernel(x, c, ctx, c_ctx, w_mod, b_mod, w_in, lam_q1, lam_k1, lam_q2, lam_k2, subln_g, sg_ln_g, sg_ln_b, sg_w, sg_b, w_out, ln1_g, ln1_b, router_group_w, router_group_b, router_expert_w, router_expert_b, exp_w_gate, exp_w_up, exp_w_down, ln2_g, ln2_b):
    b, l, d = x.shape
    n = b * l
    tm = TOK_TILE
    nt = n // tm

    cc = jnp.zeros((b + 8, d), F32).at[:b].set(c).at[b].set(c_ctx)
    mod = _mod_call(cc, w_mod[0], b_mod[0][None, :])
    sh1, sc1, g1, sh2, sc2, g2 = [mod[:b, j * d:(j + 1) * d].reshape(b, 1, d) for j in range(6)]
    csh1 = mod[b:b + 1, 0:d]
    csc1 = mod[b:b + 1, d:2 * d]

    wi = w_in[0]
    w_all = jnp.concatenate([_head_interleave(wi, 0), _head_interleave(wi, KV_LO), wi[:, 2 * KV_LO:]], -1).astype(BF16)
    kc, vc = _ctx_call(ctx, csc1, csh1, w_all[:, KV_LO:KV_HI])

    cos_t, sin_t = _rope_tables(l)
    sbias = jnp.repeat(sg_b[0].T, LANES, axis=1)
    q, k, v, sg = _in_call(x, sc1, sh1, w_all, cos_t, sin_t, sg_ln_g[0][None, :], sg_ln_b[0][None, :],
                           sg_w[0].astype(BF16), sbias)

    lamv = jnp.stack([lam_q1[0], lam_k1[0], lam_q2[0], lam_k2[0]]).astype(F32)
    da = _attn_call(lamv, q, k, kc, v, vc, subln_g[0][None, :])

    e_lo = MOE_EXPERTS_PER_GROUP
    wr = jnp.zeros((LANES, d), F32).at[:MOE_GROUPS].set(router_group_w[0].T).at[e_lo:e_lo + N_EXPERTS].set(router_expert_w[0].T)
    br = jnp.zeros((LANES,), F32).at[:MOE_GROUPS].set(router_group_b[0]).at[e_lo:e_lo + N_EXPERTS].set(router_expert_b[0])
    br = jnp.broadcast_to(br[:, None], (LANES, tm))
    wr_hi = wr.astype(BF16)
    wr_split = jnp.concatenate([wr_hi, (wr - wr_hi.astype(F32)).astype(BF16)], 0)
    x1, t, route, wts, tcnt = _out_call(da.reshape(n, DA_WIDTH), sg.reshape(n, SG_WIDTH), x.reshape(n, d), g1, sc2, sh2,
                                        w_out[0].astype(BF16), ln1_g[0][None, :], ln1_b[0][None, :], wr_split, br, l // tm)

    cnt_te = tcnt[:, :, 0].astype(jnp.int32)
    counts = jnp.sum(cnt_te, 0)
    padded = (counts + MOE_BLOCK - 1) // MOE_BLOCK * MOE_BLOCK
    pad_end = _cumsum_small(padded)
    pad_start = pad_end - padded
    tix = jnp.arange(nt, dtype=jnp.int32)
    tile_prefix = jnp.sum(jnp.where(tix[None, :, None] < tix[:, None, None], cnt_te[None], 0), 1)
    base = pad_start[None, :] + tile_prefix
    ridx = route[:4].astype(jnp.int32).reshape(4, nt, tm)
    ex = jnp.arange(N_EXPERTS, dtype=jnp.int32)

    def slot_dest(eid, rank):
        return jnp.sum(jnp.where(eid[None] == ex[:, None, None], base.T[:, :, None], 0), 0) + rank

    dest_k = (slot_dest(ridx[0], ridx[2]), slot_dest(ridx[1], ridx[3]))

    def tile_dest(tile):
        return jnp.concatenate([dk.reshape(-1, tile) for dk in dest_k], -1).reshape(-1, 1, 2 * tile)
    n_blocks = (n * 2) // MOE_BLOCK + N_EXPERTS
    p_rows = n_blocks * MOE_BLOCK
    nb_e = padded // MOE_BLOCK
    items_e = (nb_e + ITEM_SUB - 1) // ITEM_SUB
    item_end = _cumsum_small(items_e)
    max_items = (n_blocks + (ITEM_SUB - 1) * N_EXPERTS) // ITEM_SUB
    it = jnp.arange(max_items, dtype=jnp.int32)
    item_e = jnp.minimum(jnp.sum((it[:, None] >= item_end[None, :]).astype(jnp.int32), -1), N_EXPERTS - 1)
    item_j = it - _lookup(item_end - items_e, item_e)
    item_b = _lookup(pad_start // MOE_BLOCK, item_e) + ITEM_SUB * item_j
    item_ns = jnp.clip(_lookup(nb_e, item_e) - ITEM_SUB * item_j, 0, ITEM_SUB)
    nonempty = (items_e > 0).astype(jnp.int32)
    ord_e = _cumsum_small(nonempty) - 1
    later = (ex[None, :] > ex[:, None]) & (nonempty[None, :] > 0)
    nxt_e = jnp.min(jnp.where(later, ex[None, :], N_EXPERTS), -1)
    nxt_e = jnp.where(nxt_e == N_EXPERTS, -1, nxt_e)
    item_ord = _lookup(ord_e, item_e)
    item_nxt = _lookup(nxt_e, item_e)

    xb = _dispatch_call((pad_start + counts).astype(jnp.int32), pad_end.astype(jnp.int32), tile_dest(DISPATCH_TILE),
                        t, p_rows)
    yb = _expert_call(item_e, item_b.astype(jnp.int32), item_ns.astype(jnp.int32), item_ord.astype(jnp.int32),
                      item_nxt.astype(jnp.int32), item_end[-1:].astype(jnp.int32),
                      xb, exp_w_gate[0], exp_w_up[0], exp_w_down[0])
    out = _combine_call(tile_dest(COMBINE_TILE), yb, x1, wts, g2, ln2_g[0][None, :], ln2_b[0][None, :],
                        l // COMBINE_TILE)
    return out.reshape(b, l, d)
```
